```python
import jax, jax.numpy as jnp
from jax import lax
import numpy as np

D_MODEL = 1024
BATCH = 16
SEQ = 2048
DEPTH = 1

PLE_DIM = 256
RMS_EPS = 1e-6
DN_HEADS = 4
DN_DK = 128
DN_DV = 128
DN_CONV = 4
DN_CHUNK = 64
DN_QK = DN_HEADS * DN_DK
DN_V = DN_HEADS * DN_DV
DN_CONV_CH = 2 * DN_QK + DN_V
SA_HEADS = 8
SA_HEAD_DIM = 64
SA_QBLOCK = 128
IDX_HEADS = 4
IDX_DIM = 64
IDX_TOPK_MAX = 256
ROPE_THETA = 500000.0
ROPE_FRACTION = 4
NEG_INF = -1e30
D_FF = ((8 * D_MODEL + 3 * 256 - 1) // (3 * 256)) * 256
IN_SPLIT_SIZES = (DN_CONV_CH, DN_V, DN_HEADS, DN_HEADS,
                  SA_HEADS * SA_HEAD_DIM, SA_HEAD_DIM, SA_HEAD_DIM,
                  IDX_HEADS * IDX_DIM, IDX_DIM, IDX_HEADS)
D_IN = sum(IN_SPLIT_SIZES)

kernel_name = 'hybrid_gdn_dsa_gated_merge_block'


def rmsnorm(x, g, eps=RMS_EPS):
    xf = x.astype(jnp.float32)
    y = xf * lax.rsqrt(jnp.mean(xf * xf, axis=-1, keepdims=True) + eps)
    return (y * g.astype(jnp.float32)).astype(x.dtype)


def l2norm(x, eps=1e-6):
    return x * lax.rsqrt(jnp.sum(x * x, axis=-1, keepdims=True) + eps)


def rope_tables(positions, rot_dim):
    inv_freq = ROPE_THETA ** (-jnp.arange(0, rot_dim, 2, dtype=jnp.float32) / rot_dim)
    ang = positions.astype(jnp.float32)[..., None] * inv_freq
    return jnp.cos(ang), jnp.sin(ang)


def apply_partial_rope(x, cos, sin):
    half = cos.shape[-1]
    xf = x.astype(jnp.float32)
    x1, x2, rest = xf[..., :half], xf[..., half:2 * half], xf[..., 2 * half:]
    out = jnp.concatenate([x1 * cos - x2 * sin, x2 * cos + x1 * sin, rest], axis=-1)
    return out.astype(x.dtype)


def causal_depthwise_conv(x, w):
    k = w.shape[0]
    return lax.conv_general_dilated(
        x, w[:, None, :].astype(x.dtype), window_strides=(1,), padding=[(k - 1, 0)],
        dimension_numbers=('NWC', 'WIO', 'NWC'), feature_group_count=x.shape[-1])


def chunk_gated_delta_rule(q, k, v, g, beta):
    B, S, H, DK = q.shape
    DV = v.shape[-1]
    C = DN_CHUNK
    N = S // C
    q = q * (DK ** -0.5)

    def chunk(t):
        return t.reshape(B, N, C, H, t.shape[-1]).transpose(0, 1, 3, 2, 4)

    qc, kc, vc = chunk(q), chunk(k), chunk(v)
    gc = jnp.cumsum(g.reshape(B, N, C, H).transpose(0, 1, 3, 2), axis=-1)
    bc = beta.reshape(B, N, C, H).transpose(0, 1, 3, 2)
    lower = jnp.tril(jnp.ones((C, C), dtype=bool))
    strict = jnp.tril(jnp.ones((C, C), dtype=bool), k=-1)
    diff = gc[..., :, None] - gc[..., None, :]
    decay = jnp.where(lower, jnp.exp(jnp.minimum(diff, 0.0)), 0.0)
    kb = kc * bc[..., None]
    lmat = jnp.where(strict, jnp.einsum('bnhcd,bnhed->bnhce', kb, kc) * decay, 0.0)
    amat = lmat + jnp.eye(C, dtype=jnp.float32)
    rhs = jnp.concatenate([vc * bc[..., None], kb * jnp.exp(gc)[..., None]], axis=-1)
    sol = lax.linalg.triangular_solve(amat, rhs, left_side=True, lower=True,
                                      transpose_a=False, conjugate_a=False, unit_diagonal=True)
    u, w = sol[..., :DV], sol[..., DV:]
    intra = jnp.where(lower, jnp.einsum('bnhcd,bnhed->bnhce', qc, kc) * decay, 0.0)

    def step(state, xs):
        q_i, k_i, u_i, w_i, g_i, a_i = xs
        v_new = u_i - jnp.einsum('bhcd,bhde->bhce', w_i, state)
        o = (jnp.einsum('bhcd,bhde->bhce', q_i * jnp.exp(g_i)[..., None], state)
             + jnp.einsum('bhce,bhef->bhcf', a_i, v_new))
        g_last = g_i[..., -1]
        k_dec = k_i * jnp.exp(g_last[..., None] - g_i)[..., None]
        state = state * jnp.exp(g_last)[..., None, None] + jnp.einsum('bhcd,bhce->bhde', k_dec, v_new)
        return state, o

    xs = tuple(jnp.moveaxis(t, 1, 0) for t in (qc, kc, u, w, gc, intra))
    state0 = jnp.zeros((B, H, DK, DV), jnp.float32)
    _, o = lax.scan(step, state0, xs)
    return o.transpose(1, 0, 3, 2, 4).reshape(B, S, H, DV)


def gated_deltanet(qkv, z, b, a, conv_w, a_log, dt_bias, norm_g):
    B, S, _ = qkv.shape
    qkv = jax.nn.silu(causal_depthwise_conv(qkv, conv_w)).astype(jnp.float32)
    q = l2norm(qkv[..., :DN_QK].reshape(B, S, DN_HEADS, DN_DK))
    k = l2norm(qkv[..., DN_QK:2 * DN_QK].reshape(B, S, DN_HEADS, DN_DK))
    v = qkv[..., 2 * DN_QK:].reshape(B, S, DN_HEADS, DN_DV)
    beta = jax.nn.sigmoid(b.astype(jnp.float32))
    g = -jnp.exp(a_log.astype(jnp.float32)) * jax.nn.softplus(a.astype(jnp.float32) + dt_bias.astype(jnp.float32))
    o = chunk_gated_delta_rule(q, k, v, g, beta)
    zf = z.astype(jnp.float32).reshape(B, S, DN_HEADS, DN_DV)
    o = rmsnorm(o, norm_g) * jax.nn.silu(zf)
    return o.reshape(B, S, DN_V).astype(z.dtype)


def dsa_sparse_attention(q, k, v, qi, ki, wi, topk):
    B, S, H, D = q.shape
    nb = S // SA_QBLOCK
    kv = jnp.concatenate([k, v], axis=-1)
    wi = wi.astype(jnp.float32) * (IDX_HEADS ** -0.5) * (IDX_DIM ** -0.5)
    key_pos = jnp.arange(S, dtype=jnp.int32)

    def blocks(t):
        return jnp.moveaxis(t.reshape((B, nb, SA_QBLOCK) + t.shape[2:]), 1, 0)

    def one_block(args):
        qb, qib, wb, start = args
        qpos = start + jnp.arange(SA_QBLOCK, dtype=jnp.int32)
        causal = key_pos[None, None, :] <= qpos[None, :, None]
        logits = jnp.einsum('bqhd,bsd->bqhs', qib, ki).astype(jnp.float32)
        score = jnp.einsum('bqhs,bqh->bqs', jax.nn.relu(logits), wb)
        score = jnp.where(causal, score, NEG_INF)
        _, idx = lax.top_k(score, topk)
        valid = idx <= qpos[None, :, None]
        kv_sel = jax.vmap(lambda tab, ix: tab[ix])(kv, idx)
        k_sel, v_sel = kv_sel[..., :D], kv_sel[..., D:]
        att = jnp.einsum('bqhd,bqkd->bqhk', qb, k_sel).astype(jnp.float32) * (D ** -0.5)
        att = jnp.where(valid[:, :, None, :], att, NEG_INF)
        prob = jax.nn.softmax(att, axis=-1).astype(v_sel.dtype)
        return jnp.einsum('bqhk,bqkd->bqhd', prob, v_sel)

    starts = jnp.arange(nb, dtype=jnp.int32) * SA_QBLOCK
    out = lax.map(one_block, (blocks(q), blocks(qi), blocks(wi), starts))
    return jnp.moveaxis(out, 0, 1).reshape(B, S, H * D)


def setup_inputs(seed: int = 0) -> dict:
    key = jax.random.key(seed)
    ks = jax.random.split(key, 24)
    f32 = jnp.float32

    def dense(k, fan_in, fan_out):
        return jax.random.normal(k, (DEPTH, fan_in, fan_out), f32) * (fan_in ** -0.5)

    def gain(k, n):
        return 1.0 + 0.01 * jax.random.normal(k, (DEPTH, n), f32)

    x = jax.random.normal(ks[0], (BATCH, SEQ, D_MODEL), f32)
    p = jax.random.normal(ks[1], (DEPTH, BATCH, SEQ, PLE_DIM), f32)
    positions = jnp.broadcast_to(jnp.arange(SEQ, dtype=jnp.int32), (BATCH, SEQ))
    dt = jnp.exp(jax.random.uniform(ks[6], (DEPTH, DN_HEADS), f32, jnp.log(0.001), jnp.log(0.1)))
    return {
        'x': x,
        'p': p,
        'positions': positions,
        'attn_norm': gain(ks[2], D_MODEL),
        'w_in': dense(ks[3], D_MODEL, D_IN),
        'conv_w': jax.random.normal(ks[4], (DEPTH, DN_CONV, DN_CONV_CH), f32) * (DN_CONV ** -0.5),
        'a_log': jnp.log(jax.random.uniform(ks[5], (DEPTH, DN_HEADS), f32, 1.0, 16.0)),
        'dt_bias': dt + jnp.log(-jnp.expm1(-dt)),
        'dn_norm': gain(ks[7], DN_DV),
        'q_norm': gain(ks[8], SA_HEAD_DIM),
        'k_norm': gain(ks[9], SA_HEAD_DIM),
        'w_branch_a': dense(ks[10], DN_V, D_MODEL),
        'w_branch_b': dense(ks[11], SA_HEADS * SA_HEAD_DIM, D_MODEL),
        'w_gate': dense(ks[12], D_MODEL, 2 * D_MODEL),
        'b_gate': 0.01 * jax.random.normal(ks[13], (DEPTH, 2 * D_MODEL), f32),
        'w_o': dense(ks[14], D_MODEL, D_MODEL),
        'ffn_norm': gain(ks[15], D_MODEL),
        'w_ffn_gate': dense(ks[16], D_MODEL, D_FF),
        'w_ffn_up': dense(ks[17], D_MODEL, D_FF),
        'w_ffn_down': dense(ks[18], D_FF, D_MODEL),
        'ple_norm': gain(ks[19], D_MODEL),
        'w_ple_gate': dense(ks[20], D_MODEL, D_MODEL),
        'w_ple': dense(ks[21], PLE_DIM, D_MODEL),
    }


def reference(x, p, positions, attn_norm, w_in, conv_w, a_log, dt_bias, dn_norm, q_norm, k_norm,
              w_branch_a, w_branch_b, w_gate, b_gate, w_o, ffn_norm, w_ffn_gate, w_ffn_up,
              w_ffn_down, ple_norm, w_ple_gate, w_ple):
    B, S, _ = x.shape
    topk = min(IDX_TOPK_MAX, S // 4)
    cos_a, sin_a = rope_tables(positions, SA_HEAD_DIM // ROPE_FRACTION)
    cos_i, sin_i = rope_tables(positions, IDX_DIM // ROPE_FRACTION)
    offsets = np.cumsum(IN_SPLIT_SIZES)[:-1].tolist()
    for i in range(DEPTH):
        h = rmsnorm(x, attn_norm[i])
        proj = h @ w_in[i].astype(h.dtype)
        (dn_qkv, dn_z, dn_b, dn_a, sa_q, sa_k, sa_v, ix_q, ix_k, ix_w) = jnp.split(proj, offsets, axis=-1)
        o_a = gated_deltanet(dn_qkv, dn_z, dn_b, dn_a, conv_w[i], a_log[i], dt_bias[i], dn_norm[i])
        q = rmsnorm(sa_q.reshape(B, S, SA_HEADS, SA_HEAD_DIM), q_norm[i])
        q = apply_partial_rope(q, cos_a[:, :, None, :], sin_a[:, :, None, :])
        k = apply_partial_rope(rmsnorm(sa_k, k_norm[i]), cos_a, sin_a)
        qi = apply_partial_rope(ix_q.reshape(B, S, IDX_HEADS, IDX_DIM), cos_i[:, :, None, :], sin_i[:, :, None, :])
        ki = apply_partial_rope(ix_k, cos_i, sin_i)
        o_b = dsa_sparse_attention(q, k, sa_v, qi, ki, ix_w, topk)
        y_a = o_a @ w_branch_a[i].astype(o_a.dtype)
        y_b = o_b @ w_branch_b[i].astype(o_b.dtype)
        gates = jax.nn.sigmoid(h @ w_gate[i].astype(h.dtype) + b_gate[i].astype(h.dtype))
        g_a, g_b = gates[..., :D_MODEL], gates[..., D_MODEL:]
        x = x + (g_a * y_a + g_b * y_b) @ w_o[i].astype(x.dtype)
        h2 = rmsnorm(x, ffn_norm[i])
        ff = jax.nn.silu(h2 @ w_ffn_gate[i].astype(h2.dtype)) * (h2 @ w_ffn_up[i].astype(h2.dtype))
        x = x + ff @ w_ffn_down[i].astype(ff.dtype)
        ple_g = jax.nn.sigmoid(rmsnorm(x, ple_norm[i]) @ w_ple_gate[i].astype(x.dtype))
        x = x + ple_g * (p[i].astype(x.dtype) @ w_ple[i].astype(x.dtype))
    return x
```

```python
import functools

import jax
import jax.numpy as jnp
from jax import lax
from jax.experimental import pallas as pl
from jax.experimental.pallas import tpu as pltpu

F32 = jnp.float32
BF16 = jnp.bfloat16
I32 = jnp.int32

D_MODEL = 1024
PLE_DIM = 256
RMS_EPS = 1e-6
DN_HEADS = 4
DN_DK = 128
DN_DV = 128
DN_CONV = 4
DN_QK = DN_HEADS * DN_DK
DN_V = DN_HEADS * DN_DV
DN_CONV_CH = 2 * DN_QK + DN_V
SA_HEADS = 8
SA_HEAD_DIM = 64
IDX_HEADS = 4
IDX_DIM = 64
IDX_TOPK_MAX = 256
ROPE_THETA = 500000.0
ROPE_FRACTION = 4
NEG_INF = -1e30
D_FF = 2816

LANES = 128
SUBLANES = 8
CHUNK = 128
BISECT_STEPS = 22

C_QKV = 0
C_Z = DN_CONV_CH
C_SAQ = C_Z + DN_V
C_IXQ = C_SAQ + SA_HEADS * SA_HEAD_DIM
C_KV = C_IXQ + IDX_HEADS * IDX_DIM
C_MISC = C_KV + 2 * SA_HEAD_DIM
PROJ_W = C_MISC + LANES
M_IXK = 0
M_B = IDX_DIM
M_A = M_B + DN_HEADS
M_W = M_A + DN_HEADS

ROT = SA_HEAD_DIM // ROPE_FRACTION
HALF = ROT // 2

VMEM_LIMIT = 56 * 1024 * 1024


def _const_spec(shape):
    nd = len(shape)
    return pl.BlockSpec(shape, lambda *_: (0,) * nd, pipeline_mode=pl.Buffered(1))


def _rms(x, g):
    return x * lax.rsqrt(jnp.mean(x * x, axis=-1, keepdims=True) + RMS_EPS) * g


def _mm(a, b):
    return jnp.dot(a.astype(BF16), b.astype(BF16), preferred_element_type=F32)


def _mm_nt(a, b):
    return lax.dot_general(a.astype(BF16), b.astype(BF16), (((1,), (1,)), ((), ())),
                           preferred_element_type=F32)


def _sigmoid(x):
    return 1.0 / (1.0 + jnp.exp(-x))


def _silu(x):
    return x * _sigmoid(x)


def _softplus(x):
    return jnp.maximum(x, 0.0) + jnp.log1p(jnp.exp(-jnp.abs(x)))


def _in_proj_kernel(x_ref, g_ref, w_ref, o_ref):
    hb = _rms(x_ref[...], g_ref[...]).astype(BF16)
    step = 512
    for n in range(0, PROJ_W, step):
        o_ref[:, n:n + step] = jnp.dot(hb, w_ref[:, n:n + step], preferred_element_type=F32)


def _in_proj(x2d, gain, w_perm, tm):
    t = x2d.shape[0]
    return pl.pallas_call(
        _in_proj_kernel,
        grid=(t // tm,),
        in_specs=[pl.BlockSpec((tm, D_MODEL), lambda i: (i, 0)),
                  _const_spec((1, D_MODEL)),
                  _const_spec((D_MODEL, PROJ_W))],
        out_specs=pl.BlockSpec((tm, PROJ_W), lambda i: (i, 0)),
        out_shape=jax.ShapeDtypeStruct((t, PROJ_W), F32),
        compiler_params=pltpu.CompilerParams(dimension_semantics=("arbitrary",),
                                             vmem_limit_bytes=VMEM_LIMIT),
        name="in_proj",
    )(x2d, gain, w_perm)


def _dn_kernel(q_ref, k_ref, v_ref, z_ref, misc_ref, cwq_ref, cwk_ref, cwv_ref, alog_ref, dtb_ref,
               ng_ref, o_ref,
               pad_s, cq_s, ck_s, cv_s, brow_s, grow_s, u_s, w_s, a_s, qg_s, kdt_s, el_s):
    seq = q_ref.shape[1]
    nc = seq // CHUNK
    h = pl.program_id(1)
    row = lax.broadcasted_iota(I32, (CHUNK, CHUNK), 0)
    col = lax.broadcasted_iota(I32, (CHUNK, CHUNK), 1)

    pad_s[0:SUBLANES, :] = jnp.zeros((SUBLANES, LANES), F32)
    for src, cw_ref, dst, kind in ((q_ref, cwq_ref, cq_s, "q"), (k_ref, cwk_ref, ck_s, "k"),
                                   (v_ref, cwv_ref, cv_s, "v")):
        pad_s[SUBLANES:seq + SUBLANES, :] = src[0]
        for n in range(nc):
            base = n * CHUNK + SUBLANES
            y = cw_ref[DN_CONV - 1:DN_CONV, :] * pad_s[base:base + CHUNK, :]
            for j in range(1, DN_CONV):
                y = y + (cw_ref[DN_CONV - 1 - j:DN_CONV - j, :]
                         * pad_s[base - j:base - j + CHUNK, :])
            y = _silu(y)
            if kind != "v":
                y = y * lax.rsqrt(jnp.sum(y * y, axis=-1, keepdims=True) + 1e-6)
            if kind == "q":
                y = y * (DN_DK ** -0.5)
            dst[n * CHUNK:(n + 1) * CHUNK, :] = y

    upper = jnp.where(row <= col, 1.0, 0.0).astype(F32)
    sub8 = lax.broadcasted_iota(I32, (SUBLANES, LANES), 0)
    for n in range(nc):
        mt = misc_ref[0, n * CHUNK:(n + 1) * CHUNK, :].T
        slab = mt[M_B:M_B + SUBLANES, :]
        beta8 = _sigmoid(slab)
        g8 = -jnp.exp(alog_ref[...]) * _softplus(slab + dtb_ref[...])
        gc8 = jnp.dot(g8, upper, preferred_element_type=F32, precision=lax.Precision.HIGHEST)
        b_row = jnp.sum(jnp.where(sub8 == h, beta8, 0.0), axis=0, keepdims=True)
        g_row = jnp.sum(jnp.where(sub8 == h + DN_HEADS, gc8, 0.0), axis=0, keepdims=True)
        brow_s[n] = jnp.broadcast_to(b_row, (SUBLANES, LANES))
        grow_s[n] = jnp.broadcast_to(g_row, (SUBLANES, LANES))

    eye = jnp.where(row == col, 1.0, 0.0).astype(F32)

    def prep(n, carry):
        r0 = pl.multiple_of(n * CHUNK, CHUNK)
        q = cq_s[pl.ds(r0, CHUNK), :]
        k = ck_s[pl.ds(r0, CHUNK), :]
        v = cv_s[pl.ds(r0, CHUNK), :]
        g_r = jnp.broadcast_to(grow_s[n][0:1, :], (CHUNK, CHUNK))
        g_c = g_r.T
        b_c = jnp.broadcast_to(brow_s[n][0:1, :], (CHUNK, CHUNK)).T
        g_last = jnp.broadcast_to(g_c[CHUNK - 1:CHUNK, :], (CHUNK, CHUNK))
        decay = jnp.where(row >= col, jnp.exp(jnp.minimum(g_c - g_r, 0.0)), 0.0)
        kb = k * b_c
        lmat = jnp.where(row > col, _mm_nt(kb, k) * decay, 0.0)
        x = eye - lmat
        m = _mm(lmat, lmat)
        npow = CHUNK.bit_length() - 2
        for i in range(npow):
            x = x + _mm(x, m)
            if i + 1 < npow:
                m = _mm(m, m)
        eg = jnp.exp(g_c)
        sol = _mm(x, jnp.concatenate([v * b_c, kb * eg], axis=1))
        u_s[n] = sol[:, :DN_DV]
        w_s[n] = sol[:, DN_DV:]
        a_s[n] = jnp.where(row >= col, _mm_nt(q, k) * decay, 0.0)
        qg_s[n] = q * eg
        kdt_s[n] = (k * jnp.exp(g_last - g_c)).T
        el_s[n] = jnp.exp(g_last)[0:SUBLANES, :]
        return carry

    lax.fori_loop(0, nc, prep, 0)

    def step(n, state):
        r0 = pl.multiple_of(n * CHUNK, CHUNK)
        v_new = u_s[n] - _mm(w_s[n], state)
        o = _mm(qg_s[n], state) + _mm(a_s[n], v_new)
        el = jnp.broadcast_to(el_s[n][0:1, :], (CHUNK, CHUNK))
        state = state * el + _mm(kdt_s[n], v_new)
        z = z_ref[0, pl.ds(r0, CHUNK), :]
        o_ref[0, pl.ds(r0, CHUNK), :] = (_rms(o, ng_ref[...]) * _silu(z)).astype(o_ref.dtype)
        return state

    lax.fori_loop(0, nc, step, jnp.zeros((DN_DK, DN_DV), F32))


def _deltanet(proj, conv_w, alog8, dtb8, norm_g):
    bsz, seq, _ = proj.shape
    nc = seq // CHUNK
    qb, kb, vb, zb = (C_QKV // LANES, (C_QKV + DN_QK) // LANES, (C_QKV + 2 * DN_QK) // LANES,
                      C_Z // LANES)
    col = lambda off: pl.BlockSpec((1, seq, LANES), lambda b, h, off=off: (b, 0, off + h))
    cw = lambda off: pl.BlockSpec((DN_CONV, LANES), lambda b, h, off=off: (0, off + h))
    mat = lambda: pltpu.VMEM((nc, CHUNK, CHUNK), F32)
    return pl.pallas_call(
        _dn_kernel,
        grid=(bsz, DN_HEADS),
        in_specs=[col(qb), col(kb), col(vb), col(zb),
                  pl.BlockSpec((1, seq, LANES), lambda b, h: (b, 0, C_MISC // LANES)),
                  cw(qb), cw(kb), cw(vb),
                  _const_spec((SUBLANES, LANES)), _const_spec((SUBLANES, LANES)),
                  _const_spec((1, DN_DV))],
        out_specs=pl.BlockSpec((1, seq, LANES), lambda b, h: (b, 0, h)),
        out_shape=jax.ShapeDtypeStruct((bsz, seq, DN_V), BF16),
        scratch_shapes=[pltpu.VMEM((seq + SUBLANES, LANES), F32),
                        pltpu.VMEM((seq, LANES), F32), pltpu.VMEM((seq, LANES), F32),
                        pltpu.VMEM((seq, LANES), F32),
                        pltpu.VMEM((nc, SUBLANES, LANES), F32), pltpu.VMEM((nc, SUBLANES, LANES), F32),
                        mat(), mat(), mat(), mat(), mat(),
                        pltpu.VMEM((nc, SUBLANES, LANES), F32)],
        compiler_params=pltpu.CompilerParams(dimension_semantics=("arbitrary", "arbitrary"),
                                             vmem_limit_bytes=VMEM_LIMIT),
        name="deltanet",
    )(proj, proj, proj, proj, proj, conv_w, conv_w, conv_w, alog8, dtb8, norm_g)


def _rope(x, c, s, s1, s2):
    w = x.shape[-1]
    return x * c + pltpu.roll(x, w - HALF, 1) * (s * s1) + pltpu.roll(x, HALF, 1) * (s * s2)


def _dsa_kernel(topk, q_ref, qi_ref, misct_ref, kv_ref, miscf_ref, pos_ref, qg_ref, kg_ref, rp_ref,
                o_ref,
                cos_s, sin_s, kbf_s, vbf_s, kibf_s, qh_s, qih_s, wrow_s, sc_s, sel_s,
                mx_s, ls_s, acc_s):
    seq = kv_ref.shape[1]
    nc = seq // CHUNK
    qb = pl.program_id(1)
    nck = qb + 1
    r0 = pl.multiple_of(qb * CHUNK, CHUNK)
    row = lax.broadcasted_iota(I32, (CHUNK, CHUNK), 0)
    col = lax.broadcasted_iota(I32, (CHUNK, CHUNK), 1)
    freq = rp_ref[0:1, :]
    s1 = rp_ref[1:2, :]
    s2 = rp_ref[2:3, :]

    @pl.when(qb == 0)
    def _():
        for n in range(nc):
            rows = slice(n * CHUNK, (n + 1) * CHUNK)
            ang = pos_ref[0, rows, :].astype(F32) * freq
            c = jnp.cos(ang)
            s = jnp.sin(ang)
            cos_s[rows, :] = c
            sin_s[rows, :] = s
            ck = jnp.where(col < ROT, c, 1.0)
            sk = jnp.where(col < ROT, s, 0.0)
            kv = kv_ref[0, rows, :]
            ss = jnp.sum(jnp.where(col < SA_HEAD_DIM, kv * kv, 0.0), axis=-1, keepdims=True)
            r = lax.rsqrt(ss * (1.0 / SA_HEAD_DIM) + RMS_EPS)
            y = jnp.where(col < SA_HEAD_DIM, kv * r * kg_ref[...], kv)
            y = _rope(y, ck, sk, s1, s2)
            kbf_s[rows, :] = y[:, :SA_HEAD_DIM].astype(BF16)
            vbf_s[rows, :] = y[:, SA_HEAD_DIM:].astype(BF16)
            yi = _rope(miscf_ref[0, rows, :], ck, sk, s1, s2)
            kibf_s[rows, :] = yi[:, :IDX_DIM].astype(BF16)

    c1 = cos_s[pl.ds(r0, CHUNK), :]
    sn1 = sin_s[pl.ds(r0, CHUNK), :]
    rep = lambda a, k: jnp.concatenate([a] * k, axis=1)
    nq = SA_HEADS * SA_HEAD_DIM // LANES
    xq = q_ref[0]
    yq = _rope(xq * qg_ref[...], rep(c1, nq), rep(sn1, nq), rep(s1, nq), rep(s2, nq))
    for hh in range(SA_HEADS):
        lanes = slice(hh * SA_HEAD_DIM, (hh + 1) * SA_HEAD_DIM)
        xh = xq[:, lanes]
        r = lax.rsqrt(jnp.mean(xh * xh, axis=-1, keepdims=True) + RMS_EPS)
        qh_s[hh] = (yq[:, lanes] * (r * (SA_HEAD_DIM ** -0.5))).astype(BF16)
    ni = IDX_HEADS * IDX_DIM // LANES
    yi = _rope(qi_ref[0], rep(c1, ni), rep(sn1, ni), rep(s1, ni), rep(s2, ni))
    for hh in range(IDX_HEADS):
        qih_s[hh] = yi[:, hh * IDX_DIM:(hh + 1) * IDX_DIM].astype(BF16)
    mt = misct_ref[0].T
    wrow_s[...] = mt[M_W:M_W + SUBLANES, :] * ((IDX_HEADS ** -0.5) * (IDX_DIM ** -0.5))

    qpos = r0 + col
    big = -NEG_INF

    def idx_body(c, carry):
        k0 = pl.multiple_of(c * CHUNK, CHUNK)
        ki = kibf_s[pl.ds(k0, CHUNK), :]
        acc = jnp.zeros((CHUNK, CHUNK), F32)
        for hh in range(IDX_HEADS):
            lt = _mm_nt(ki, qih_s[hh])
            acc = acc + jnp.maximum(lt, 0.0) * wrow_s[hh:hh + 1, :]
        causal = k0 + row <= qpos
        sc_s[c] = jnp.where(causal, acc, NEG_INF)
        lo, hi = carry
        lo = jnp.minimum(lo, jnp.min(jnp.where(causal, acc, big), axis=0, keepdims=True))
        hi = jnp.maximum(hi, jnp.max(jnp.where(causal, acc, NEG_INF), axis=0, keepdims=True))
        return lo, hi

    lo, hi = lax.fori_loop(0, nck, idx_body, (jnp.full((1, CHUNK), big, F32),
                                              jnp.full((1, CHUNK), NEG_INF, F32)))

    def over_keys(fn, init, combine):
        def body(c, acc):
            return combine(acc, fn(sc_s[c]))
        return lax.fori_loop(0, nck, body, init)

    def count(pred):
        return over_keys(lambda s: jnp.sum(jnp.where(pred(s), 1.0, 0.0), axis=0, keepdims=True),
                         jnp.zeros((1, CHUNK), F32), jnp.add)

    def max_where(pred):
        return over_keys(lambda s: jnp.max(jnp.where(pred(s), s, NEG_INF), axis=0, keepdims=True),
                         jnp.full((1, CHUNK), NEG_INF, F32), jnp.maximum)

    kf = jnp.minimum(float(topk), (qpos[0:1, :] + 1).astype(F32))

    def bisect(_, bounds):
        lo, hi = bounds
        mid = lo + 0.5 * (hi - lo)
        above = count(lambda s: s > mid) >= kf
        return jnp.where(above, mid, lo), jnp.where(above, hi, mid)

    lo, hi = lax.fori_loop(0, BISECT_STEPS, bisect, (lo, hi))

    thr = max_where(lambda s: s <= hi)
    cge = count(lambda s: s >= thr)

    def short(cge):
        return jnp.max(jnp.where(cge < kf, 1.0, 0.0)) > 0.5

    def peel(state):
        thr, cge, _ = state
        nxt = max_where(lambda s: s < thr)
        cnx = count(lambda s: s >= nxt)
        step = cge < kf
        thr = jnp.where(step, nxt, thr)
        cge = jnp.where(step, cnx, cge)
        return thr, cge, short(cge)

    thr, cge, _ = lax.while_loop(lambda st: st[2], peel, (thr, cge, short(cge)))
    need = kf - count(lambda s: s > thr)

    lower = jnp.where(row >= col, 1.0, 0.0).astype(BF16)

    def sel_body(c, off):
        k0 = pl.multiple_of(c * CHUNK, CHUNK)
        s = sc_s[c]
        eqf = jnp.where(s == thr, 1.0, 0.0)
        pref = jnp.dot(lower, eqf.astype(BF16), preferred_element_type=F32)
        tie = jnp.where(pref + off <= need, eqf, 0.0)
        sf = jnp.where(s > thr, 1.0, tie)
        sf = jnp.where(k0 + row <= qpos, sf, 0.0)
        sel_s[c] = sf.T
        return off + pref[CHUNK - 1:CHUNK, :]

    lax.fori_loop(0, nck, sel_body, jnp.zeros((1, CHUNK), F32))

    mx_s[...] = jnp.full(mx_s.shape, NEG_INF, F32)

    def p1(c, carry):
        k0 = pl.multiple_of(c * CHUNK, CHUNK)
        kc = kbf_s[pl.ds(k0, CHUNK), :]
        sel = sel_s[c] > 0.5
        for hh in range(SA_HEADS):
            s = _mm_nt(qh_s[hh], kc)
            mx_s[hh] = jnp.maximum(mx_s[hh], jnp.where(sel, s, NEG_INF))
        return carry

    lax.fori_loop(0, nck, p1, 0)
    for hh in range(SA_HEADS):
        mx_s[hh] = jnp.broadcast_to(jnp.max(mx_s[hh], axis=-1, keepdims=True), (CHUNK, CHUNK))
    ls_s[...] = jnp.zeros(ls_s.shape, F32)
    acc_s[...] = jnp.zeros(acc_s.shape, F32)

    def p2(c, carry):
        k0 = pl.multiple_of(c * CHUNK, CHUNK)
        kc = kbf_s[pl.ds(k0, CHUNK), :]
        vc = vbf_s[pl.ds(k0, CHUNK), :]
        sel = sel_s[c] > 0.5
        for hh in range(SA_HEADS):
            s = _mm_nt(qh_s[hh], kc)
            p = jnp.exp(jnp.where(sel, s, NEG_INF) - mx_s[hh])
            ls_s[hh] = ls_s[hh] + p
            acc_s[hh] = acc_s[hh] + jnp.dot(p.astype(BF16), vc, preferred_element_type=F32)
        return carry

    lax.fori_loop(0, nck, p2, 0)
    for hh in range(SA_HEADS):
        denom = jnp.sum(ls_s[hh], axis=-1, keepdims=True)
        o_ref[0, :, hh * SA_HEAD_DIM:(hh + 1) * SA_HEAD_DIM] = (acc_s[hh] / denom).astype(o_ref.dtype)


def _dsa(proj, pos3, qg_t, kg_p, rope_pat, topk):
    bsz, seq, _ = proj.shape
    nc = seq // CHUNK
    qw = SA_HEADS * SA_HEAD_DIM
    iw = IDX_HEADS * IDX_DIM
    tile3 = lambda n: pltpu.VMEM((n, CHUNK, CHUNK), F32)
    return pl.pallas_call(
        functools.partial(_dsa_kernel, topk),
        grid=(bsz, nc),
        in_specs=[pl.BlockSpec((1, CHUNK, qw), lambda b, j: (b, j, C_SAQ // qw)),
                  pl.BlockSpec((1, CHUNK, iw), lambda b, j: (b, j, C_IXQ // iw)),
                  pl.BlockSpec((1, CHUNK, LANES), lambda b, j: (b, j, C_MISC // LANES)),
                  pl.BlockSpec((1, seq, LANES), lambda b, j: (b, 0, C_KV // LANES)),
                  pl.BlockSpec((1, seq, LANES), lambda b, j: (b, 0, C_MISC // LANES)),
                  pl.BlockSpec((1, seq, 1), lambda b, j: (b, 0, 0)),
                  _const_spec((1, qw)), _const_spec((1, LANES)), _const_spec((SUBLANES, LANES))],
        out_specs=pl.BlockSpec((1, CHUNK, qw), lambda b, j: (b, j, 0)),
        out_shape=jax.ShapeDtypeStruct((bsz, seq, qw), BF16),
        scratch_shapes=[pltpu.VMEM((seq, LANES), F32), pltpu.VMEM((seq, LANES), F32),
                        pltpu.VMEM((seq, SA_HEAD_DIM), BF16), pltpu.VMEM((seq, SA_HEAD_DIM), BF16),
                        pltpu.VMEM((seq, IDX_DIM), BF16),
                        pltpu.VMEM((SA_HEADS, CHUNK, SA_HEAD_DIM), BF16),
                        pltpu.VMEM((IDX_HEADS, CHUNK, IDX_DIM), BF16),
                        pltpu.VMEM((SUBLANES, LANES), F32),
                        tile3(nc), tile3(nc),
                        tile3(SA_HEADS), tile3(SA_HEADS),
                        pltpu.VMEM((SA_HEADS, CHUNK, SA_HEAD_DIM), F32)],
        compiler_params=pltpu.CompilerParams(dimension_semantics=("arbitrary", "arbitrary"),
                                             vmem_limit_bytes=VMEM_LIMIT),
        name="dsa",
    )(proj, proj, proj, proj, proj, pos3, qg_t, kg_p, rope_pat)


def _merge_kernel(x_ref, oa_ref, ob_ref, g_ref, wg_ref, bg_ref, wa_ref, wb_ref, wo_ref, o_ref):
    x = x_ref[...]
    hb = _rms(x, g_ref[...]).astype(BF16)
    oa = oa_ref[...]
    ob = ob_ref[...]
    acc = x
    step = 512
    for n in range(0, D_MODEL, step):
        ga = _sigmoid(jnp.dot(hb, wg_ref[:, n:n + step], preferred_element_type=F32)
                      + bg_ref[:, n:n + step])
        gb = _sigmoid(jnp.dot(hb, wg_ref[:, D_MODEL + n:D_MODEL + n + step],
                              preferred_element_type=F32)
                      + bg_ref[:, D_MODEL + n:D_MODEL + n + step])
        ya = jnp.dot(oa, wa_ref[:, n:n + step], preferred_element_type=F32)
        yb = jnp.dot(ob, wb_ref[:, n:n + step], preferred_element_type=F32)
        merged = (ga * ya + gb * yb).astype(BF16)
        acc = acc + jnp.dot(merged, wo_ref[n:n + step, :], preferred_element_type=F32)
    o_ref[...] = acc


def _merge(x2d, oa2d, ob2d, gain, w_gate, b_gate, w_a, w_b, w_o, tm):
    t = x2d.shape[0]
    tok = lambda w: pl.BlockSpec((tm, w), lambda i: (i, 0))
    return pl.pallas_call(
        _merge_kernel,
        grid=(t // tm,),
        in_specs=[tok(D_MODEL), tok(DN_V), tok(SA_HEADS * SA_HEAD_DIM),
                  _const_spec((1, D_MODEL)), _const_spec((D_MODEL, 2 * D_MODEL)),
                  _const_spec((1, 2 * D_MODEL)), _const_spec((DN_V, D_MODEL)),
                  _const_spec((SA_HEADS * SA_HEAD_DIM, D_MODEL)), _const_spec((D_MODEL, D_MODEL))],
        out_specs=tok(D_MODEL),
        out_shape=jax.ShapeDtypeStruct((t, D_MODEL), F32),
        compiler_params=pltpu.CompilerParams(dimension_semantics=("arbitrary",),
                                             vmem_limit_bytes=VMEM_LIMIT),
        name="merge",
    )(x2d, oa2d, ob2d, gain, w_gate, b_gate, w_a, w_b, w_o)


def _ffn_kernel(x_ref, p_ref, g2_ref, wg_ref, wu_ref, wd_ref, g3_ref, wpg_ref, wp_ref, o_ref):
    x = x_ref[...]
    hb = _rms(x, g2_ref[...]).astype(BF16)
    acc = x
    step = 256
    for n in range(0, D_FF, step):
        gate = jnp.dot(hb, wg_ref[:, n:n + step], preferred_element_type=F32)
        up = jnp.dot(hb, wu_ref[:, n:n + step], preferred_element_type=F32)
        acc = acc + jnp.dot((_silu(gate) * up).astype(BF16), wd_ref[n:n + step, :],
                            preferred_element_type=F32)
    h3 = _rms(acc, g3_ref[...]).astype(BF16)
    pg = _sigmoid(jnp.dot(h3, wpg_ref[...], preferred_element_type=F32))
    pe = jnp.dot(p_ref[...].astype(BF16), wp_ref[...], preferred_element_type=F32)
    o_ref[...] = acc + pg * pe


def _ffn(x2d, p2d, g2, w_g, w_u, w_d, g3, w_pg, w_p, tm):
    t = x2d.shape[0]
    tok = lambda w: pl.BlockSpec((tm, w), lambda i: (i, 0))
    return pl.pallas_call(
        _ffn_kernel,
        grid=(t // tm,),
        in_specs=[tok(D_MODEL), tok(PLE_DIM), _const_spec((1, D_MODEL)),
                  _const_spec((D_MODEL, D_FF)), _const_spec((D_MODEL, D_FF)),
                  _const_spec((D_FF, D_MODEL)), _const_spec((1, D_MODEL)),
                  _const_spec((D_MODEL, D_MODEL)), _const_spec((PLE_DIM, D_MODEL))],
        out_specs=tok(D_MODEL),
        out_shape=jax.ShapeDtypeStruct((t, D_MODEL), F32),
        compiler_params=pltpu.CompilerParams(dimension_semantics=("arbitrary",),
                                             vmem_limit_bytes=VMEM_LIMIT),
        name="ffn_ple",
    )(x2d, p2d, g2, w_g, w_u, w_d, g3, w_pg, w_p)


def _regroup_w_in(w):
    sizes = (DN_CONV_CH, DN_V, DN_HEADS, DN_HEADS, SA_HEADS * SA_HEAD_DIM, SA_HEAD_DIM, SA_HEAD_DIM,
             IDX_HEADS * IDX_DIM, IDX_DIM, IDX_HEADS)
    parts, off = [], 0
    for s in sizes:
        parts.append(w[:, off:off + s])
        off += s
    qkv, z, b, a, saq, sak, sav, ixq, ixk, ixw = parts
    cols = [qkv, z, saq, ixq, sak, sav, ixk, b, a, ixw]
    used = sum(c.shape[1] for c in cols)
    cols.append(jnp.zeros((w.shape[0], PROJ_W - used), w.dtype))
    return jnp.concatenate(cols, axis=1)


def _rope_pattern():
    lane = jnp.arange(LANES) % SA_HEAD_DIM
    inv_freq = ROPE_THETA ** (-jnp.arange(0, ROT, 2, dtype=F32) / ROT)
    freq = jnp.where(lane < ROT, inv_freq[lane % HALF], 0.0)
    s1 = jnp.where(lane < HALF, -1.0, 0.0)
    s2 = jnp.where((lane >= HALF) & (lane < ROT), 1.0, 0.0)
    pat = jnp.zeros((SUBLANES, LANES), F32)
    return pat.at[0].set(freq).at[1].set(s1).at[2].set(s2)


def _layer(x, p, positions, attn_norm, w_in, conv_w, a_log, dt_bias, dn_norm, q_norm, k_norm,
           w_branch_a, w_branch_b, w_gate, b_gate, w_o, ffn_norm, w_ffn_gate, w_ffn_up, w_ffn_down,
           ple_norm, w_ple_gate, w_ple):
    bsz, seq, _ = x.shape
    t = bsz * seq
    tm = 512
    topk = min(IDX_TOPK_MAX, seq // 4)
    row = lambda v: v.reshape(1, -1).astype(F32)
    x2d = x.reshape(t, D_MODEL)

    proj = _in_proj(x2d, row(attn_norm), _regroup_w_in(w_in).astype(BF16), tm)
    proj = proj.reshape(bsz, seq, PROJ_W)

    pad4 = jnp.zeros((DN_HEADS,), F32)
    per_head = lambda v: jnp.broadcast_to(jnp.concatenate([pad4, v.astype(F32)])[:, None],
                                          (SUBLANES, LANES))
    o_a = _deltanet(proj, conv_w.astype(F32), per_head(a_log), per_head(dt_bias), row(dn_norm))

    qg_t = jnp.tile(q_norm.astype(F32), SA_HEADS).reshape(1, -1)
    kg_p = jnp.concatenate([k_norm.astype(F32), jnp.ones((LANES - SA_HEAD_DIM,), F32)]).reshape(1, -1)
    pos3 = positions.astype(I32).reshape(bsz, seq, 1)
    o_b = _dsa(proj, pos3, qg_t, kg_p, _rope_pattern(), topk)

    x1 = _merge(x2d, o_a.reshape(t, DN_V), o_b.reshape(t, -1), row(attn_norm),
                w_gate.astype(BF16), row(b_gate), w_branch_a.astype(BF16), w_branch_b.astype(BF16),
                w_o.astype(BF16), tm)
    x2 = _ffn(x1, p.reshape(t, PLE_DIM), row(ffn_norm), w_ffn_gate.astype(BF16),
              w_ffn_up.astype(BF16), w_ffn_down.astype(BF16), row(ple_norm),
              w_ple_gate.astype(BF16), w_ple.astype(BF16), tm)
    return x2.reshape(bsz, seq, D_MODEL)


def kernel(x, p, positions, attn_norm, w_in, conv_w, a_log, dt_bias, dn_norm, q_norm, k_norm,
           w_branch_a, w_branch_b, w_gate, b_gate, w_o, ffn_norm, w_ffn_gate, w_ffn_up, w_ffn_down,
           ple_norm, w_ple_gate, w_ple):
    depth = w_in.shape[0]
    for i in range(depth):
        x = _layer(x, p[i], positions, attn_norm[i], w_in[i], conv_w[i], a_log[i], dt_bias[i],
                   dn_norm[i], q_norm[i], k_norm[i], w_branch_a[i], w_branch_b[i], w_gate[i],
                   b_gate[i], w_o[i], ffn_norm[i], w_ffn_gate[i], w_ffn_up[i], w_ffn_down[i],
                   ple_norm[i], w_ple_gate[i], w_ple[i])
    return x
```

```python
import functools

import jax
import jax.numpy as jnp
from jax import lax
from jax.experimental import pallas as pl
from jax.experimental.pallas import tpu as pltpu

F32 = jnp.float32
BF16 = jnp.bfloat16
I32 = jnp.int32

D_MODEL = 1024
PLE_DIM = 256
RMS_EPS = 1e-6
DN_HEADS = 4
DN_DK = 128
DN_DV = 128
DN_CONV = 4
DN_QK = DN_HEADS * DN_DK
DN_V = DN_HEADS * DN_DV
DN_CONV_CH = 2 * DN_QK + DN_V
SA_HEADS = 8
SA_HEAD_DIM = 64
IDX_HEADS = 4
IDX_DIM = 64
IDX_TOPK_MAX = 256
ROPE_THETA = 500000.0
ROPE_FRACTION = 4
NEG_INF = -1e30
D_FF = 2816

LANES = 128
SUBLANES = 8
CHUNK = 128
DN_GROUP = 8
KPAIR = 2 * CHUNK
BISECT_STEPS = 19

C_QKV = 0
C_Z = DN_CONV_CH
C_SAQ = C_Z + DN_V
C_IXQ = C_SAQ + SA_HEADS * SA_HEAD_DIM
C_KV = C_IXQ + IDX_HEADS * IDX_DIM
C_MISC = C_KV + 2 * SA_HEAD_DIM
PROJ_W = C_MISC + LANES
M_IXK = 0
M_B = IDX_DIM
M_A = M_B + DN_HEADS
M_W = M_A + DN_HEADS

ROT = SA_HEAD_DIM // ROPE_FRACTION
HALF = ROT // 2

VMEM_LIMIT = 56 * 1024 * 1024


def _const_spec(shape):
    nd = len(shape)
    return pl.BlockSpec(shape, lambda *_: (0,) * nd, pipeline_mode=pl.Buffered(1))


def _rms(x, g):
    return x * lax.rsqrt(jnp.mean(x * x, axis=-1, keepdims=True) + RMS_EPS) * g


def _mm(a, b):
    return jnp.dot(a.astype(BF16), b.astype(BF16), preferred_element_type=F32)


def _mm_nt(a, b):
    return lax.dot_general(a.astype(BF16), b.astype(BF16), (((1,), (1,)), ((), ())),
                           preferred_element_type=F32)


def _sigmoid(x):
    return 1.0 / (1.0 + jnp.exp(-x))


def _silu(x):
    return x * _sigmoid(x)


def _softplus(x):
    return jnp.maximum(x, 0.0) + jnp.log1p(jnp.exp(-jnp.abs(x)))


def _in_proj_kernel(x_ref, g_ref, w_ref, o_ref):
    hb = _rms(x_ref[...], g_ref[...]).astype(BF16)
    step = 512
    for n in range(0, PROJ_W, step):
        o_ref[:, n:n + step] = jnp.dot(hb, w_ref[:, n:n + step], preferred_element_type=F32)


def _in_proj(x2d, gain, w_perm, tm):
    t = x2d.shape[0]
    return pl.pallas_call(
        _in_proj_kernel,
        grid=(t // tm,),
        in_specs=[pl.BlockSpec((tm, D_MODEL), lambda i: (i, 0)),
                  _const_spec((1, D_MODEL)),
                  _const_spec((D_MODEL, PROJ_W))],
        out_specs=pl.BlockSpec((tm, PROJ_W), lambda i: (i, 0)),
        out_shape=jax.ShapeDtypeStruct((t, PROJ_W), F32),
        compiler_params=pltpu.CompilerParams(dimension_semantics=("arbitrary",),
                                             vmem_limit_bytes=VMEM_LIMIT),
        name="in_proj",
    )(x2d, gain, w_perm)


def _dn_kernel(q_ref, k_ref, v_ref, z_ref, misc_ref, cwq_ref, cwk_ref, cwv_ref, alog_ref, dtb_ref,
               ng_ref, o_ref,
               pad_s, cq_s, ck_s, cv_s, brow_s, grow_s, u_s, w_s, a_s, qg_s, c_s, n_s, st_s, el_s):
    seq = q_ref.shape[1]
    nc = seq // CHUNK
    h = pl.program_id(1)
    row = lax.broadcasted_iota(I32, (CHUNK, CHUNK), 0)
    col = lax.broadcasted_iota(I32, (CHUNK, CHUNK), 1)

    pad_s[0:SUBLANES, :] = jnp.zeros((SUBLANES, LANES), F32)
    for src, cw_ref, dst, kind in ((q_ref, cwq_ref, cq_s, "q"), (k_ref, cwk_ref, ck_s, "k"),
                                   (v_ref, cwv_ref, cv_s, "v")):
        pad_s[SUBLANES:seq + SUBLANES, :] = src[0]
        for n in range(nc):
            base = n * CHUNK + SUBLANES
            y = cw_ref[DN_CONV - 1:DN_CONV, :] * pad_s[base:base + CHUNK, :]
            for j in range(1, DN_CONV):
                y = y + (cw_ref[DN_CONV - 1 - j:DN_CONV - j, :]
                         * pad_s[base - j:base - j + CHUNK, :])
            y = _silu(y)
            if kind != "v":
                y = y * lax.rsqrt(jnp.sum(y * y, axis=-1, keepdims=True) + 1e-6)
            if kind == "q":
                y = y * (DN_DK ** -0.5)
            dst[n * CHUNK:(n + 1) * CHUNK, :] = y

    upper = jnp.where(row <= col, 1.0, 0.0).astype(F32)
    sub8 = lax.broadcasted_iota(I32, (SUBLANES, LANES), 0)
    for n in range(nc):
        mt = misc_ref[0, n * CHUNK:(n + 1) * CHUNK, :].T
        slab = mt[M_B:M_B + SUBLANES, :]
        beta8 = _sigmoid(slab)
        g8 = -jnp.exp(alog_ref[...]) * _softplus(slab + dtb_ref[...])
        gc8 = jnp.dot(g8, upper, preferred_element_type=F32, precision=lax.Precision.HIGHEST)
        b_row = jnp.sum(jnp.where(sub8 == h, beta8, 0.0), axis=0, keepdims=True)
        g_row = jnp.sum(jnp.where(sub8 == h + DN_HEADS, gc8, 0.0), axis=0, keepdims=True)
        brow_s[n] = jnp.broadcast_to(b_row, (SUBLANES, LANES))
        grow_s[n] = jnp.broadcast_to(g_row, (SUBLANES, LANES))

    eye = jnp.where(row == col, 1.0, 0.0).astype(F32)

    def prep_one(n):
        r0 = pl.multiple_of(n * CHUNK, CHUNK)
        q = cq_s[pl.ds(r0, CHUNK), :]
        k = ck_s[pl.ds(r0, CHUNK), :]
        v = cv_s[pl.ds(r0, CHUNK), :]
        g_r = jnp.broadcast_to(grow_s[n][0:1, :], (CHUNK, CHUNK))
        g_c = g_r.T
        b_c = jnp.broadcast_to(brow_s[n][0:1, :], (CHUNK, CHUNK)).T
        g_last = jnp.broadcast_to(g_c[CHUNK - 1:CHUNK, :], (CHUNK, CHUNK))
        decay = jnp.where(row >= col, jnp.exp(jnp.minimum(g_c - g_r, 0.0)), 0.0)
        kb = k * b_c
        kq = _mm_nt(jnp.concatenate([kb, q], axis=0), k)
        yield
        lmat = jnp.where(row > col, kq[:CHUNK] * decay, 0.0)
        a = jnp.where(row >= col, kq[CHUNK:] * decay, 0.0)
        x = eye - lmat
        m = _mm(lmat, lmat)
        yield
        npow = CHUNK.bit_length() - 2
        for i in range(npow - 1):
            xm = _mm(jnp.concatenate([x, m], axis=0), m)
            yield
            x = x + xm[:CHUNK]
            m = xm[CHUNK:]
        x = x + _mm(x, m)
        yield
        eg = jnp.exp(g_c)
        sol = _mm(x, jnp.concatenate([v * b_c, kb * eg], axis=1))
        yield
        cn = _mm((k * jnp.exp(g_last - g_c)).T, sol)
        yield
        u_s[n] = sol[:, :DN_DV]
        w_s[n] = sol[:, DN_DV:]
        a_s[n] = a
        qg_s[n] = q * eg
        c_s[n] = cn[:, :DN_DV]
        n_s[n] = cn[:, DN_DV:]
        el_s[n] = jnp.exp(g_last)[0:SUBLANES, :]

    def grouped(gen):
        def body(i, carry):
            live = [gen(i * DN_GROUP + j) for j in range(DN_GROUP)]
            while live:
                live = [g for g in live if next(g, True) is None]
            return carry
        lax.fori_loop(0, nc // DN_GROUP, body, 0)

    grouped(prep_one)

    def chain(n, state):
        st_s[n] = state
        el = jnp.broadcast_to(el_s[n][0:1, :], (CHUNK, CHUNK))
        return state * el + c_s[n] - _mm(n_s[n], state)

    lax.fori_loop(0, nc, chain, jnp.zeros((DN_DK, DN_DV), F32))

    def emit_one(n):
        r0 = pl.multiple_of(n * CHUNK, CHUNK)
        ws = _mm(jnp.concatenate([w_s[n], qg_s[n]], axis=0), st_s[n])
        yield
        o = ws[CHUNK:] + _mm(a_s[n], u_s[n] - ws[:CHUNK])
        yield
        z = z_ref[0, pl.ds(r0, CHUNK), :]
        o_ref[0, pl.ds(r0, CHUNK), :] = (_rms(o, ng_ref[...]) * _silu(z)).astype(o_ref.dtype)

    grouped(emit_one)


def _deltanet(proj, conv_w, alog8, dtb8, norm_g):
    bsz, seq, _ = proj.shape
    nc = seq // CHUNK
    qb, kb, vb, zb = (C_QKV // LANES, (C_QKV + DN_QK) // LANES, (C_QKV + 2 * DN_QK) // LANES,
                      C_Z // LANES)
    col = lambda off: pl.BlockSpec((1, seq, LANES), lambda b, h, off=off: (b, 0, off + h))
    cw = lambda off: pl.BlockSpec((DN_CONV, LANES), lambda b, h, off=off: (0, off + h))
    mat = lambda: pltpu.VMEM((nc, CHUNK, CHUNK), F32)
    return pl.pallas_call(
        _dn_kernel,
        grid=(bsz, DN_HEADS),
        in_specs=[col(qb), col(kb), col(vb), col(zb),
                  pl.BlockSpec((1, seq, LANES), lambda b, h: (b, 0, C_MISC // LANES)),
                  cw(qb), cw(kb), cw(vb),
                  _const_spec((SUBLANES, LANES)), _const_spec((SUBLANES, LANES)),
                  _const_spec((1, DN_DV))],
        out_specs=pl.BlockSpec((1, seq, LANES), lambda b, h: (b, 0, h)),
        out_shape=jax.ShapeDtypeStruct((bsz, seq, DN_V), BF16),
        scratch_shapes=[pltpu.VMEM((seq + SUBLANES, LANES), F32),
                        pltpu.VMEM((seq, LANES), F32), pltpu.VMEM((seq, LANES), F32),
                        pltpu.VMEM((seq, LANES), F32),
                        pltpu.VMEM((nc, SUBLANES, LANES), F32), pltpu.VMEM((nc, SUBLANES, LANES), F32),
                        mat(), mat(), mat(), mat(), mat(), mat(), mat(),
                        pltpu.VMEM((nc, SUBLANES, LANES), F32)],
        compiler_params=pltpu.CompilerParams(dimension_semantics=("arbitrary", "arbitrary"),
                                             vmem_limit_bytes=VMEM_LIMIT),
        name="deltanet",
    )(proj, proj, proj, proj, proj, conv_w, conv_w, conv_w, alog8, dtb8, norm_g)


def _even_odd(n):
    return list(range(0, n, 2)) + list(range(1, n, 2))


def _rope(x, c, s, s1, s2):
    w = x.shape[-1]
    return x * c + pltpu.roll(x, w - HALF, 1) * (s * s1) + pltpu.roll(x, HALF, 1) * (s * s2)


def _dsa_kernel(topk, q_ref, qi_ref, misct_ref, kv_ref, miscf_ref, pos_ref, qg_ref, kg_ref, rp_ref,
                hsum_ref, o_ref,
                cos_s, sin_s, ka_s, kb_s, vt_s, kia_s, kib_s, q2_s, qi2_s, sc_s, bias_s, lg_s, p_s,
                acc_s):
    seq = kv_ref.shape[1]
    nc = seq // CHUNK
    qb = pl.program_id(1)
    npair = (qb + 2) // 2
    r0 = pl.multiple_of(qb * CHUNK, CHUNK)
    col = lax.broadcasted_iota(I32, (CHUNK, CHUNK), 1)
    freq = rp_ref[0:1, :]
    s1 = rp_ref[1:2, :]
    s2 = rp_ref[2:3, :]

    @pl.when(qb == 0)
    def _():
        for n in range(nc):
            rows = slice(n * CHUNK, (n + 1) * CHUNK)
            ang = pos_ref[0, rows, :].astype(F32) * freq
            c = jnp.cos(ang)
            s = jnp.sin(ang)
            cos_s[rows, :] = c
            sin_s[rows, :] = s
            ck = jnp.where(col < ROT, c, 1.0)
            sk = jnp.where(col < ROT, s, 0.0)
            kv = kv_ref[0, rows, :]
            ss = jnp.sum(jnp.where(col < SA_HEAD_DIM, kv * kv, 0.0), axis=-1, keepdims=True)
            r = lax.rsqrt(ss * (1.0 / SA_HEAD_DIM) + RMS_EPS)
            y = jnp.where(col < SA_HEAD_DIM, kv * r * kg_ref[...], kv)
            y = _rope(y, ck, sk, s1, s2)
            ka = jnp.where(col < SA_HEAD_DIM, y, 0.0)
            ka_s[rows, :] = ka.astype(BF16)
            kb_s[rows, :] = pltpu.roll(ka, SA_HEAD_DIM, 1).astype(BF16)
            half = (n % 2) * CHUNK
            vt_s[n // 2, :, half:half + CHUNK] = y.T[SA_HEAD_DIM:, :].astype(BF16)
            yi = _rope(miscf_ref[0, rows, :], ck, sk, s1, s2)
            kia = jnp.where(col < IDX_DIM, yi, 0.0)
            kia_s[rows, :] = kia.astype(BF16)
            kib_s[rows, :] = pltpu.roll(kia, IDX_DIM, 1).astype(BF16)

    c1 = cos_s[pl.ds(r0, CHUNK), :]
    sn1 = sin_s[pl.ds(r0, CHUNK), :]
    rep = lambda a, k: jnp.concatenate([a] * k, axis=1)
    nq = SA_HEADS * SA_HEAD_DIM // LANES
    xq = q_ref[0]
    yq = _rope(xq * qg_ref[...], rep(c1, nq), rep(sn1, nq), rep(s1, nq), rep(s2, nq))
    ss = jnp.dot((xq * xq).astype(BF16), hsum_ref[...], preferred_element_type=F32)
    r = lax.rsqrt(ss * (1.0 / SA_HEAD_DIM) + RMS_EPS)
    qn = (yq * (r * (SA_HEAD_DIM ** -0.5))).astype(BF16)
    for pr in range(nq):
        q2_s[pr * CHUNK:(pr + 1) * CHUNK, :] = qn[:, pr * LANES:(pr + 1) * LANES]
    ni = IDX_HEADS * IDX_DIM // LANES
    yi = _rope(qi_ref[0], rep(c1, ni), rep(sn1, ni), rep(s1, ni), rep(s2, ni)).astype(BF16)
    for pr in range(ni):
        qi2_s[pr * CHUNK:(pr + 1) * CHUNK, :] = yi[:, pr * LANES:(pr + 1) * LANES]
    mt = misct_ref[0].T
    wscale = (IDX_HEADS ** -0.5) * (IDX_DIM ** -0.5)
    wrow = jnp.concatenate([mt[M_W + hh:M_W + hh + 1, :] for hh in _even_odd(IDX_HEADS)],
                           axis=1) * wscale

    krow = lax.broadcasted_iota(I32, (KPAIR, CHUNK), 0)
    qpos = r0 + lax.broadcasted_iota(I32, (KPAIR, CHUNK), 1)
    big = -NEG_INF

    def fold8(x, op):
        parts = [x[i:i + SUBLANES] for i in range(0, x.shape[0], SUBLANES)]
        while len(parts) > 1:
            parts = [op(parts[i], parts[i + 1]) for i in range(0, len(parts), 2)]
        return parts[0]

    def idx_body(c, carry):
        k0 = pl.multiple_of(c * KPAIR, KPAIR)
        qi2 = qi2_s[...]
        lt = jnp.concatenate([_mm_nt(kia_s[pl.ds(k0, KPAIR), :], qi2),
                              _mm_nt(kib_s[pl.ds(k0, KPAIR), :], qi2)], axis=1)
        wl = jnp.maximum(lt, 0.0) * wrow
        acc = wl[:, 0:CHUNK]
        for hh in range(1, IDX_HEADS):
            acc = acc + wl[:, hh * CHUNK:(hh + 1) * CHUNK]
        causal = k0 + krow <= qpos
        sc = jnp.where(causal, acc, NEG_INF)
        sc_s[c] = sc
        lo8, hi8 = carry
        lo8 = jnp.minimum(lo8, fold8(jnp.where(causal, acc, big), jnp.minimum))
        hi8 = jnp.maximum(hi8, fold8(sc, jnp.maximum))
        return lo8, hi8

    lo8, hi8 = lax.fori_loop(0, npair, idx_body, (jnp.full((SUBLANES, CHUNK), big, F32),
                                                  jnp.full((SUBLANES, CHUNK), NEG_INF, F32)))
    lo = jnp.min(lo8, axis=0, keepdims=True)
    hi = jnp.max(hi8, axis=0, keepdims=True)

    def count(pred):
        def body(c, acc):
            return acc + fold8(jnp.where(pred(sc_s[c]), 1.0, 0.0), jnp.add)
        acc = lax.fori_loop(0, npair, body, jnp.zeros((SUBLANES, CHUNK), F32))
        return jnp.sum(acc, axis=0, keepdims=True)

    def max_where(pred):
        def body(c, acc):
            s = sc_s[c]
            return jnp.maximum(acc, fold8(jnp.where(pred(s), s, NEG_INF), jnp.maximum))
        acc = lax.fori_loop(0, npair, body, jnp.full((SUBLANES, CHUNK), NEG_INF, F32))
        return jnp.max(acc, axis=0, keepdims=True)

    kf = jnp.minimum(float(topk), (qpos[0:1, :] + 1).astype(F32))

    def bisect(_, bounds):
        lo, hi = bounds
        mid = lo + 0.5 * (hi - lo)
        above = count(lambda s: s > mid) >= kf
        return jnp.where(above, mid, lo), jnp.where(above, hi, mid)

    lo, hi = lax.fori_loop(0, BISECT_STEPS, bisect, (lo, hi))

    thr = max_where(lambda s: s <= hi)
    cge = count(lambda s: s >= thr)

    def short(cge):
        return jnp.max(jnp.where(cge < kf, 1.0, 0.0)) > 0.5

    def peel(state):
        thr, cge, _ = state
        nxt = max_where(lambda s: s < thr)
        cnx = count(lambda s: s >= nxt)
        step = cge < kf
        thr = jnp.where(step, nxt, thr)
        cge = jnp.where(step, cnx, cge)
        return thr, cge, short(cge)

    thr, cge, _ = lax.while_loop(lambda st: st[2], peel, (thr, cge, short(cge)))
    need = kf - count(lambda s: s > thr)

    lower = jnp.where(lax.broadcasted_iota(I32, (KPAIR, KPAIR), 0)
                      >= lax.broadcasted_iota(I32, (KPAIR, KPAIR), 1), 1.0, 0.0).astype(BF16)

    tile_heads = lambda b: jnp.concatenate([b] * SA_HEADS, axis=1)

    def sel_body(c, carry):
        off, m8 = carry
        k0 = pl.multiple_of(c * KPAIR, KPAIR)
        q2 = q2_s[...]
        lg = jnp.concatenate([_mm_nt(ka_s[pl.ds(k0, KPAIR), :], q2),
                              _mm_nt(kb_s[pl.ds(k0, KPAIR), :], q2)], axis=1)
        lg_s[c] = lg
        s = sc_s[c]
        eqf = jnp.where(s == thr, 1.0, 0.0)
        pref = jnp.dot(lower, eqf.astype(BF16), preferred_element_type=F32)
        tie = jnp.where(pref + off <= need, eqf, 0.0)
        keep = jnp.where(k0 + krow <= qpos, jnp.where(s > thr, 1.0, tie), 0.0)
        bias = jnp.where(keep > 0.5, 0.0, NEG_INF)
        bias_s[c] = bias
        m8 = jnp.maximum(m8, fold8(lg + tile_heads(bias), jnp.maximum))
        return off + pref[KPAIR - 1:KPAIR, :], m8

    _, m8 = lax.fori_loop(0, npair, sel_body,
                          (jnp.zeros((1, CHUNK), F32),
                           jnp.full((SUBLANES, SA_HEADS * CHUNK), NEG_INF, F32)))
    mrow = jnp.max(m8, axis=0, keepdims=True)

    def probs(c):
        p = jnp.exp(lg_s[c] + tile_heads(bias_s[c]) - mrow)
        p_s[c % 2] = p.astype(BF16)
        return fold8(p, jnp.add)

    def pv(c):
        return jnp.dot(vt_s[c], p_s[c % 2], preferred_element_type=F32)

    def p2(c, l8):
        acc_s[...] += pv(c - 1)
        return l8 + probs(c)

    acc_s[...] = jnp.zeros(acc_s.shape, F32)
    l8 = lax.fori_loop(1, npair, p2, probs(0))
    out_t = (acc_s[...] + pv(npair - 1)) / jnp.sum(l8, axis=0, keepdims=True)
    order = _even_odd(SA_HEADS)
    for pr in range(SA_HEADS // 2):
        ev, od = order.index(2 * pr), order.index(2 * pr + 1)
        two = jnp.concatenate([out_t[:, ev * CHUNK:(ev + 1) * CHUNK],
                               out_t[:, od * CHUNK:(od + 1) * CHUNK]], axis=0)
        o_ref[0, :, pr * LANES:(pr + 1) * LANES] = two.T.astype(o_ref.dtype)


def _dsa(proj, pos3, qg_t, kg_p, rope_pat, topk):
    bsz, seq, _ = proj.shape
    nc = seq // CHUNK
    qw = SA_HEADS * SA_HEAD_DIM
    iw = IDX_HEADS * IDX_DIM
    assert nc % 2 == 0
    pairs = lambda: pltpu.VMEM((nc // 2, KPAIR, CHUNK), F32)
    head_of = jnp.arange(qw) // SA_HEAD_DIM
    head_sum = (head_of[:, None] == head_of[None, :]).astype(BF16)
    return pl.pallas_call(
        functools.partial(_dsa_kernel, topk),
        grid=(bsz, nc),
        in_specs=[pl.BlockSpec((1, CHUNK, qw), lambda b, j: (b, j, C_SAQ // qw)),
                  pl.BlockSpec((1, CHUNK, iw), lambda b, j: (b, j, C_IXQ // iw)),
                  pl.BlockSpec((1, CHUNK, LANES), lambda b, j: (b, j, C_MISC // LANES)),
                  pl.BlockSpec((1, seq, LANES), lambda b, j: (b, 0, C_KV // LANES)),
                  pl.BlockSpec((1, seq, LANES), lambda b, j: (b, 0, C_MISC // LANES)),
                  pl.BlockSpec((1, seq, 1), lambda b, j: (b, 0, 0)),
                  _const_spec((1, qw)), _const_spec((1, LANES)), _const_spec((SUBLANES, LANES)),
                  _const_spec((qw, qw))],
        out_specs=pl.BlockSpec((1, CHUNK, qw), lambda b, j: (b, j, 0)),
        out_shape=jax.ShapeDtypeStruct((bsz, seq, qw), BF16),
        scratch_shapes=[pltpu.VMEM((seq, LANES), F32), pltpu.VMEM((seq, LANES), F32),
                        pltpu.VMEM((seq, LANES), BF16), pltpu.VMEM((seq, LANES), BF16),
                        pltpu.VMEM((nc // 2, SA_HEAD_DIM, KPAIR), BF16),
                        pltpu.VMEM((seq, LANES), BF16), pltpu.VMEM((seq, LANES), BF16),
                        pltpu.VMEM((qw // LANES * CHUNK, LANES), BF16),
                        pltpu.VMEM((iw // LANES * CHUNK, LANES), BF16),
                        pairs(), pairs(),
                        pltpu.VMEM((nc // 2, KPAIR, SA_HEADS * CHUNK), F32),
                        pltpu.VMEM((2, KPAIR, SA_HEADS * CHUNK), BF16),
                        pltpu.VMEM((SA_HEAD_DIM, SA_HEADS * CHUNK), F32)],
        compiler_params=pltpu.CompilerParams(dimension_semantics=("arbitrary", "arbitrary"),
                                             vmem_limit_bytes=VMEM_LIMIT),
        name="dsa",
    )(proj, proj, proj, proj, proj, pos3, qg_t, kg_p, rope_pat, head_sum)


def _merge_kernel(x_ref, oa_ref, ob_ref, g_ref, wg_ref, bg_ref, wa_ref, wb_ref, wo_ref, o_ref):
    x = x_ref[...]
    hb = _rms(x, g_ref[...]).astype(BF16)
    oa = oa_ref[...]
    ob = ob_ref[...]
    acc = x
    step = 512
    for n in range(0, D_MODEL, step):
        ga = _sigmoid(jnp.dot(hb, wg_ref[:, n:n + step], preferred_element_type=F32)
                      + bg_ref[:, n:n + step])
        gb = _sigmoid(jnp.dot(hb, wg_ref[:, D_MODEL + n:D_MODEL + n + step],
                              preferred_element_type=F32)
                      + bg_ref[:, D_MODEL + n:D_MODEL + n + step])
        ya = jnp.dot(oa, wa_ref[:, n:n + step], preferred_element_type=F32)
        yb = jnp.dot(ob, wb_ref[:, n:n + step], preferred_element_type=F32)
        merged = (ga * ya + gb * yb).astype(BF16)
        acc = acc + jnp.dot(merged, wo_ref[n:n + step, :], preferred_element_type=F32)
    o_ref[...] = acc


def _merge(x2d, oa2d, ob2d, gain, w_gate, b_gate, w_a, w_b, w_o, tm):
    t = x2d.shape[0]
    tok = lambda w: pl.BlockSpec((tm, w), lambda i: (i, 0))
    return pl.pallas_call(
        _merge_kernel,
        grid=(t // tm,),
        in_specs=[tok(D_MODEL), tok(DN_V), tok(SA_HEADS * SA_HEAD_DIM),
                  _const_spec((1, D_MODEL)), _const_spec((D_MODEL, 2 * D_MODEL)),
                  _const_spec((1, 2 * D_MODEL)), _const_spec((DN_V, D_MODEL)),
                  _const_spec((SA_HEADS * SA_HEAD_DIM, D_MODEL)), _const_spec((D_MODEL, D_MODEL))],
        out_specs=tok(D_MODEL),
        out_shape=jax.ShapeDtypeStruct((t, D_MODEL), F32),
        compiler_params=pltpu.CompilerParams(dimension_semantics=("arbitrary",),
                                             vmem_limit_bytes=VMEM_LIMIT),
        name="merge",
    )(x2d, oa2d, ob2d, gain, w_gate, b_gate, w_a, w_b, w_o)


def _ffn_kernel(x_ref, p_ref, g2_ref, wg_ref, wu_ref, wd_ref, g3_ref, wpg_ref, wp_ref, o_ref):
    x = x_ref[...]
    hb = _rms(x, g2_ref[...]).astype(BF16)
    acc = x
    step = 256
    for n in range(0, D_FF, step):
        gate = jnp.dot(hb, wg_ref[:, n:n + step], preferred_element_type=F32)
        up = jnp.dot(hb, wu_ref[:, n:n + step], preferred_element_type=F32)
        acc = acc + jnp.dot((_silu(gate) * up).astype(BF16), wd_ref[n:n + step, :],
                            preferred_element_type=F32)
    h3 = _rms(acc, g3_ref[...]).astype(BF16)
    pg = _sigmoid(jnp.dot(h3, wpg_ref[...], preferred_element_type=F32))
    pe = jnp.dot(p_ref[...].astype(BF16), wp_ref[...], preferred_element_type=F32)
    o_ref[...] = acc + pg * pe


def _ffn(x2d, p2d, g2, w_g, w_u, w_d, g3, w_pg, w_p, tm):
    t = x2d.shape[0]
    tok = lambda w: pl.BlockSpec((tm, w), lambda i: (i, 0))
    return pl.pallas_call(
        _ffn_kernel,
        grid=(t // tm,),
        in_specs=[tok(D_MODEL), tok(PLE_DIM), _const_spec((1, D_MODEL)),
                  _const_spec((D_MODEL, D_FF)), _const_spec((D_MODEL, D_FF)),
                  _const_spec((D_FF, D_MODEL)), _const_spec((1, D_MODEL)),
                  _const_spec((D_MODEL, D_MODEL)), _const_spec((PLE_DIM, D_MODEL))],
        out_specs=tok(D_MODEL),
        out_shape=jax.ShapeDtypeStruct((t, D_MODEL), F32),
        compiler_params=pltpu.CompilerParams(dimension_semantics=("arbitrary",),
                                             vmem_limit_bytes=VMEM_LIMIT),
        name="ffn_ple",
    )(x2d, p2d, g2, w_g, w_u, w_d, g3, w_pg, w_p)


def _regroup_w_in(w):
    sizes = (DN_CONV_CH, DN_V, DN_HEADS, DN_HEADS, SA_HEADS * SA_HEAD_DIM, SA_HEAD_DIM, SA_HEAD_DIM,
             IDX_HEADS * IDX_DIM, IDX_DIM, IDX_HEADS)
    parts, off = [], 0
    for s in sizes:
        parts.append(w[:, off:off + s])
        off += s
    qkv, z, b, a, saq, sak, sav, ixq, ixk, ixw = parts
    cols = [qkv, z, saq, ixq, sak, sav, ixk, b, a, ixw]
    used = sum(c.shape[1] for c in cols)
    cols.append(jnp.zeros((w.shape[0], PROJ_W - used), w.dtype))
    return jnp.concatenate(cols, axis=1)


def _rope_pattern():
    lane = jnp.arange(LANES) % SA_HEAD_DIM
    inv_freq = ROPE_THETA ** (-jnp.arange(0, ROT, 2, dtype=F32) / ROT)
    freq = jnp.where(lane < ROT, inv_freq[lane % HALF], 0.0)
    s1 = jnp.where(lane < HALF, -1.0, 0.0)
    s2 = jnp.where((lane >= HALF) & (lane < ROT), 1.0, 0.0)
    pat = jnp.zeros((SUBLANES, LANES), F32)
    return pat.at[0].set(freq).at[1].set(s1).at[2].set(s2)


def _layer(x, p, positions, attn_norm, w_in, conv_w, a_log, dt_bias, dn_norm, q_norm, k_norm,
           w_branch_a, w_branch_b, w_gate, b_gate, w_o, ffn_norm, w_ffn_gate, w_ffn_up, w_ffn_down,
           ple_norm, w_ple_gate, w_ple):
    bsz, seq, _ = x.shape
    t = bsz * seq
    tm = 512
    topk = min(IDX_TOPK_MAX, seq // 4)
    row = lambda v: v.reshape(1, -1).astype(F32)
    x2d = x.reshape(t, D_MODEL)

    proj = _in_proj(x2d, row(attn_norm), _regroup_w_in(w_in).astype(BF16), tm)
    proj = proj.reshape(bsz, seq, PROJ_W)

    pad4 = jnp.zeros((DN_HEADS,), F32)
    per_head = lambda v: jnp.broadcast_to(jnp.concatenate([pad4, v.astype(F32)])[:, None],
                                          (SUBLANES, LANES))
    o_a = _deltanet(proj, conv_w.astype(F32), per_head(a_log), per_head(dt_bias), row(dn_norm))

    qg_t = jnp.tile(q_norm.astype(F32), SA_HEADS).reshape(1, -1)
    kg_p = jnp.concatenate([k_norm.astype(F32), jnp.ones((LANES - SA_HEAD_DIM,), F32)]).reshape(1, -1)
    pos3 = positions.astype(I32).reshape(bsz, seq, 1)
    o_b = _dsa(proj, pos3, qg_t, kg_p, _rope_pattern(), topk)

    x1 = _merge(x2d, o_a.reshape(t, DN_V), o_b.reshape(t, -1), row(attn_norm),
                w_gate.astype(BF16), row(b_gate), w_branch_a.astype(BF16), w_branch_b.astype(BF16),
                w_o.astype(BF16), tm)
    x2 = _ffn(x1, p.reshape(t, PLE_DIM), row(ffn_norm), w_ffn_gate.astype(BF16),
              w_ffn_up.astype(BF16), w_ffn_down.astype(BF16), row(ple_norm),
              w_ple_gate.astype(BF16), w_ple.astype(BF16), tm)
    return x2.reshape(bsz, seq, D_MODEL)


def kernel(x, p, positions, attn_norm, w_in, conv_w, a_log, dt_bias, dn_norm, q_norm, k_norm,
           w_branch_a, w_branch_b, w_gate, b_gate, w_o, ffn_norm, w_ffn_gate, w_ffn_up, w_ffn_down,
           ple_norm, w_ple_gate, w_ple):
    depth = w_in.shape[0]
    for i in range(depth):
        x = _layer(x, p[i], positions, attn_norm[i], w_in[i], conv_w[i], a_log[i], dt_bias[i],
                   dn_norm[i], q_norm[i], k_norm[i], w_branch_a[i], w_branch_b[i], w_gate[i],
                   b_gate[i], w_o[i], ffn_norm[i], w_ffn_gate[i], w_ffn_up[i], w_ffn_down[i],
                   ple_norm[i], w_ple_gate[i], w_ple[i])
    return x
```

```python
import functools

import jax
import jax.numpy as jnp
from jax import lax
from jax.experimental import pallas as pl
from jax.experimental.pallas import tpu as pltpu

F32 = jnp.float32
BF16 = jnp.bfloat16
I32 = jnp.int32

D_MODEL = 1024
PLE_DIM = 256
RMS_EPS = 1e-6
DN_HEADS = 4
DN_DK = 128
DN_DV = 128
DN_CONV = 4
DN_QK = DN_HEADS * DN_DK
DN_V = DN_HEADS * DN_DV
DN_CONV_CH = 2 * DN_QK + DN_V
SA_HEADS = 8
SA_HEAD_DIM = 64
IDX_HEADS = 4
IDX_DIM = 64
IDX_TOPK_MAX = 256
ROPE_THETA = 500000.0
ROPE_FRACTION = 4
NEG_INF = -1e30
D_FF = 2816

LANES = 128
SUBLANES = 8
CHUNK = 128
DN_GROUP = 8
KPAIR = 2 * CHUNK
QBLK = KPAIR
BISECT_STEPS = 19

C_QKV = 0
C_Z = DN_CONV_CH
C_SAQ = C_Z + DN_V
C_IXQ = C_SAQ + SA_HEADS * SA_HEAD_DIM
C_KV = C_IXQ + IDX_HEADS * IDX_DIM
C_MISC = C_KV + 2 * SA_HEAD_DIM
PROJ_W = C_MISC + LANES
M_IXK = 0
M_B = IDX_DIM
M_A = M_B + DN_HEADS
M_W = M_A + DN_HEADS

ROT = SA_HEAD_DIM // ROPE_FRACTION
HALF = ROT // 2

VMEM_LIMIT = 56 * 1024 * 1024


def _const_spec(shape):
    nd = len(shape)
    return pl.BlockSpec(shape, lambda *_: (0,) * nd, pipeline_mode=pl.Buffered(1))


def _rms(x, g):
    return x * lax.rsqrt(jnp.mean(x * x, axis=-1, keepdims=True) + RMS_EPS) * g


def _mm(a, b):
    return jnp.dot(a.astype(BF16), b.astype(BF16), preferred_element_type=F32)


def _mm_nt(a, b):
    return lax.dot_general(a.astype(BF16), b.astype(BF16), (((1,), (1,)), ((), ())),
                           preferred_element_type=F32)


def _sigmoid(x):
    return 1.0 / (1.0 + jnp.exp(-x))


def _silu(x):
    return x * _sigmoid(x)


def _softplus(x):
    return jnp.maximum(x, 0.0) + jnp.log1p(jnp.exp(-jnp.abs(x)))


def _in_proj_kernel(x_ref, g_ref, w_ref, o_ref):
    hb = _rms(x_ref[...], g_ref[...]).astype(BF16)
    step = 512
    for n in range(0, PROJ_W, step):
        o_ref[:, n:n + step] = jnp.dot(hb, w_ref[:, n:n + step], preferred_element_type=F32)


def _in_proj(x2d, gain, w_perm, tm):
    t = x2d.shape[0]
    return pl.pallas_call(
        _in_proj_kernel,
        grid=(t // tm,),
        in_specs=[pl.BlockSpec((tm, D_MODEL), lambda i: (i, 0)),
                  _const_spec((1, D_MODEL)),
                  _const_spec((D_MODEL, PROJ_W))],
        out_specs=pl.BlockSpec((tm, PROJ_W), lambda i: (i, 0)),
        out_shape=jax.ShapeDtypeStruct((t, PROJ_W), F32),
        compiler_params=pltpu.CompilerParams(dimension_semantics=("arbitrary",),
                                             vmem_limit_bytes=VMEM_LIMIT),
        name="in_proj",
    )(x2d, gain, w_perm)


def _dn_kernel(q_ref, k_ref, v_ref, z_ref, misc_ref, cwq_ref, cwk_ref, cwv_ref, alog_ref, dtb_ref,
               ng_ref, o_ref,
               pad_s, cq_s, ck_s, cv_s, brow_s, grow_s, u_s, w_s, a_s, qg_s, c_s, n_s, st_s, el_s):
    seq = q_ref.shape[1]
    nc = seq // CHUNK
    h = pl.program_id(1)
    row = lax.broadcasted_iota(I32, (CHUNK, CHUNK), 0)
    col = lax.broadcasted_iota(I32, (CHUNK, CHUNK), 1)

    pad_s[0:SUBLANES, :] = jnp.zeros((SUBLANES, LANES), F32)
    for src, cw_ref, dst, kind in ((q_ref, cwq_ref, cq_s, "q"), (k_ref, cwk_ref, ck_s, "k"),
                                   (v_ref, cwv_ref, cv_s, "v")):
        pad_s[SUBLANES:seq + SUBLANES, :] = src[0]
        for n in range(nc):
            base = n * CHUNK + SUBLANES
            y = cw_ref[DN_CONV - 1:DN_CONV, :] * pad_s[base:base + CHUNK, :]
            for j in range(1, DN_CONV):
                y = y + (cw_ref[DN_CONV - 1 - j:DN_CONV - j, :]
                         * pad_s[base - j:base - j + CHUNK, :])
            y = _silu(y)
            if kind != "v":
                y = y * lax.rsqrt(jnp.sum(y * y, axis=-1, keepdims=True) + 1e-6)
            if kind == "q":
                y = y * (DN_DK ** -0.5)
            dst[n * CHUNK:(n + 1) * CHUNK, :] = y

    upper = jnp.where(row <= col, 1.0, 0.0).astype(F32)
    sub8 = lax.broadcasted_iota(I32, (SUBLANES, LANES), 0)
    for n in range(nc):
        mt = misc_ref[0, n * CHUNK:(n + 1) * CHUNK, :].T
        slab = mt[M_B:M_B + SUBLANES, :]
        beta8 = _sigmoid(slab)
        g8 = -jnp.exp(alog_ref[...]) * _softplus(slab + dtb_ref[...])
        gc8 = jnp.dot(g8, upper, preferred_element_type=F32, precision=lax.Precision.HIGHEST)
        b_row = jnp.sum(jnp.where(sub8 == h, beta8, 0.0), axis=0, keepdims=True)
        g_row = jnp.sum(jnp.where(sub8 == h + DN_HEADS, gc8, 0.0), axis=0, keepdims=True)
        brow_s[n] = jnp.broadcast_to(b_row, (SUBLANES, LANES))
        grow_s[n] = jnp.broadcast_to(g_row, (SUBLANES, LANES))

    eye = jnp.where(row == col, 1.0, 0.0).astype(F32)

    def prep_one(n):
        r0 = pl.multiple_of(n * CHUNK, CHUNK)
        q = cq_s[pl.ds(r0, CHUNK), :]
        k = ck_s[pl.ds(r0, CHUNK), :]
        v = cv_s[pl.ds(r0, CHUNK), :]
        g_r = jnp.broadcast_to(grow_s[n][0:1, :], (CHUNK, CHUNK))
        g_c = g_r.T
        b_c = jnp.broadcast_to(brow_s[n][0:1, :], (CHUNK, CHUNK)).T
        g_last = jnp.broadcast_to(g_c[CHUNK - 1:CHUNK, :], (CHUNK, CHUNK))
        decay = jnp.where(row >= col, jnp.exp(jnp.minimum(g_c - g_r, 0.0)), 0.0)
        kb = k * b_c
        kq = _mm_nt(jnp.concatenate([kb, q], axis=0), k)
        yield
        lmat = jnp.where(row > col, kq[:CHUNK] * decay, 0.0)
        a = jnp.where(row >= col, kq[CHUNK:] * decay, 0.0)
        x = eye - lmat
        m = _mm(lmat, lmat)
        yield
        npow = CHUNK.bit_length() - 2
        for i in range(npow - 1):
            xm = _mm(jnp.concatenate([x, m], axis=0), m)
            yield
            x = x + xm[:CHUNK]
            m = xm[CHUNK:]
        x = x + _mm(x, m)
        yield
        eg = jnp.exp(g_c)
        sol = _mm(x, jnp.concatenate([v * b_c, kb * eg], axis=1))
        yield
        cn = _mm((k * jnp.exp(g_last - g_c)).T, sol)
        yield
        u_s[n] = sol[:, :DN_DV]
        w_s[n] = sol[:, DN_DV:]
        a_s[n] = a
        qg_s[n] = q * eg
        c_s[n] = cn[:, :DN_DV]
        n_s[n] = cn[:, DN_DV:]
        el_s[n] = jnp.exp(g_last)[0:SUBLANES, :]

    def grouped(gen):
        def body(i, carry):
            live = [gen(i * DN_GROUP + j) for j in range(DN_GROUP)]
            while live:
                live = [g for g in live if next(g, True) is None]
            return carry
        lax.fori_loop(0, nc // DN_GROUP, body, 0)

    grouped(prep_one)

    def chain(n, state):
        st_s[n] = state
        el = jnp.broadcast_to(el_s[n][0:1, :], (CHUNK, CHUNK))
        return state * el + c_s[n] - _mm(n_s[n], state)

    lax.fori_loop(0, nc, chain, jnp.zeros((DN_DK, DN_DV), F32))

    def emit_one(n):
        r0 = pl.multiple_of(n * CHUNK, CHUNK)
        ws = _mm(jnp.concatenate([w_s[n], qg_s[n]], axis=0), st_s[n])
        yield
        o = ws[CHUNK:] + _mm(a_s[n], u_s[n] - ws[:CHUNK])
        yield
        z = z_ref[0, pl.ds(r0, CHUNK), :]
        o_ref[0, pl.ds(r0, CHUNK), :] = (_rms(o, ng_ref[...]) * _silu(z)).astype(o_ref.dtype)

    grouped(emit_one)


def _deltanet(proj, conv_w, alog8, dtb8, norm_g):
    bsz, seq, _ = proj.shape
    nc = seq // CHUNK
    qb, kb, vb, zb = (C_QKV // LANES, (C_QKV + DN_QK) // LANES, (C_QKV + 2 * DN_QK) // LANES,
                      C_Z // LANES)
    col = lambda off: pl.BlockSpec((1, seq, LANES), lambda b, h, off=off: (b, 0, off + h))
    cw = lambda off: pl.BlockSpec((DN_CONV, LANES), lambda b, h, off=off: (0, off + h))
    mat = lambda: pltpu.VMEM((nc, CHUNK, CHUNK), F32)
    return pl.pallas_call(
        _dn_kernel,
        grid=(bsz, DN_HEADS),
        in_specs=[col(qb), col(kb), col(vb), col(zb),
                  pl.BlockSpec((1, seq, LANES), lambda b, h: (b, 0, C_MISC // LANES)),
                  cw(qb), cw(kb), cw(vb),
                  _const_spec((SUBLANES, LANES)), _const_spec((SUBLANES, LANES)),
                  _const_spec((1, DN_DV))],
        out_specs=pl.BlockSpec((1, seq, LANES), lambda b, h: (b, 0, h)),
        out_shape=jax.ShapeDtypeStruct((bsz, seq, DN_V), BF16),
        scratch_shapes=[pltpu.VMEM((seq + SUBLANES, LANES), F32),
                        pltpu.VMEM((seq, LANES), F32), pltpu.VMEM((seq, LANES), F32),
                        pltpu.VMEM((seq, LANES), F32),
                        pltpu.VMEM((nc, SUBLANES, LANES), F32), pltpu.VMEM((nc, SUBLANES, LANES), F32),
                        mat(), mat(), mat(), mat(), mat(), mat(), mat(),
                        pltpu.VMEM((nc, SUBLANES, LANES), F32)],
        compiler_params=pltpu.CompilerParams(dimension_semantics=("arbitrary", "arbitrary"),
                                             vmem_limit_bytes=VMEM_LIMIT),
        name="deltanet",
    )(proj, proj, proj, proj, proj, conv_w, conv_w, conv_w, alog8, dtb8, norm_g)


def _even_odd(n):
    return list(range(0, n, 2)) + list(range(1, n, 2))


def _rope(x, c, s, s1, s2):
    w = x.shape[-1]
    return x * c + pltpu.roll(x, w - HALF, 1) * (s * s1) + pltpu.roll(x, HALF, 1) * (s * s2)


def _dsa_kernel(topk, q_ref, qi_ref, misct_ref, kv_ref, miscf_ref, pos_ref, qg_ref, kg_ref, rp_ref,
                hsum_ref, o_ref,
                cos_s, sin_s, ka_s, kb_s, vt_s, kia_s, kib_s, q2_s, qi2_s, sc_s, bias_s, lg_s, p_s,
                acc_s):
    seq = kv_ref.shape[1]
    nc = seq // CHUNK
    qb = pl.program_id(1)
    npair = qb + 1
    r0 = pl.multiple_of(qb * QBLK, QBLK)
    col = lax.broadcasted_iota(I32, (CHUNK, CHUNK), 1)
    freq = rp_ref[0:1, :]
    s1 = rp_ref[1:2, :]
    s2 = rp_ref[2:3, :]

    @pl.when(qb == 0)
    def _():
        for n in range(nc):
            rows = slice(n * CHUNK, (n + 1) * CHUNK)
            ang = pos_ref[0, rows, :].astype(F32) * freq
            c = jnp.cos(ang)
            s = jnp.sin(ang)
            cos_s[rows, :] = c
            sin_s[rows, :] = s
            ck = jnp.where(col < ROT, c, 1.0)
            sk = jnp.where(col < ROT, s, 0.0)
            kv = kv_ref[0, rows, :]
            ss = jnp.sum(jnp.where(col < SA_HEAD_DIM, kv * kv, 0.0), axis=-1, keepdims=True)
            r = lax.rsqrt(ss * (1.0 / SA_HEAD_DIM) + RMS_EPS)
            y = jnp.where(col < SA_HEAD_DIM, kv * r * kg_ref[...], kv)
            y = _rope(y, ck, sk, s1, s2)
            ka = jnp.where(col < SA_HEAD_DIM, y, 0.0)
            ka_s[rows, :] = ka.astype(BF16)
            kb_s[rows, :] = pltpu.roll(ka, SA_HEAD_DIM, 1).astype(BF16)
            half = (n % 2) * CHUNK
            vt_s[n // 2, :, half:half + CHUNK] = y.T[SA_HEAD_DIM:, :].astype(BF16)
            yi = _rope(miscf_ref[0, rows, :], ck, sk, s1, s2)
            kia = jnp.where(col < IDX_DIM, yi, 0.0)
            kia_s[rows, :] = kia.astype(BF16)
            kib_s[rows, :] = pltpu.roll(kia, IDX_DIM, 1).astype(BF16)

    c1 = cos_s[pl.ds(r0, QBLK), :]
    sn1 = sin_s[pl.ds(r0, QBLK), :]
    rep = lambda a, k: jnp.concatenate([a] * k, axis=1)
    nq = SA_HEADS * SA_HEAD_DIM // LANES
    xq = q_ref[0]
    yq = _rope(xq * qg_ref[...], rep(c1, nq), rep(sn1, nq), rep(s1, nq), rep(s2, nq))
    ss = jnp.dot((xq * xq).astype(BF16), hsum_ref[...], preferred_element_type=F32)
    r = lax.rsqrt(ss * (1.0 / SA_HEAD_DIM) + RMS_EPS)
    qn = (yq * (r * (SA_HEAD_DIM ** -0.5))).astype(BF16)
    for pr in range(nq):
        q2_s[pr * QBLK:(pr + 1) * QBLK, :] = qn[:, pr * LANES:(pr + 1) * LANES]
    ni = IDX_HEADS * IDX_DIM // LANES
    yi = _rope(qi_ref[0], rep(c1, ni), rep(sn1, ni), rep(s1, ni), rep(s2, ni)).astype(BF16)
    for pr in range(ni):
        qi2_s[pr * QBLK:(pr + 1) * QBLK, :] = yi[:, pr * LANES:(pr + 1) * LANES]
    mt = misct_ref[0].T
    wscale = (IDX_HEADS ** -0.5) * (IDX_DIM ** -0.5)
    wrow = jnp.concatenate([mt[M_W + hh:M_W + hh + 1, :] for hh in _even_odd(IDX_HEADS)],
                           axis=1) * wscale

    krow = lax.broadcasted_iota(I32, (KPAIR, QBLK), 0)
    qpos = r0 + lax.broadcasted_iota(I32, (KPAIR, QBLK), 1)
    big = -NEG_INF

    def fold8(x, op):
        parts = [x[i:i + SUBLANES] for i in range(0, x.shape[0], SUBLANES)]
        while len(parts) > 1:
            parts = [op(parts[i], parts[i + 1]) for i in range(0, len(parts), 2)]
        return parts[0]

    def idx_body(c, carry):
        k0 = pl.multiple_of(c * KPAIR, KPAIR)
        qi2 = qi2_s[...]
        lt = jnp.concatenate([_mm_nt(kia_s[pl.ds(k0, KPAIR), :], qi2),
                              _mm_nt(kib_s[pl.ds(k0, KPAIR), :], qi2)], axis=1)
        wl = jnp.maximum(lt, 0.0) * wrow
        acc = wl[:, 0:QBLK]
        for hh in range(1, IDX_HEADS):
            acc = acc + wl[:, hh * QBLK:(hh + 1) * QBLK]
        causal = k0 + krow <= qpos
        sc = jnp.where(causal, acc, NEG_INF)
        sc_s[c] = sc
        lo8, hi8 = carry
        lo8 = jnp.minimum(lo8, fold8(jnp.where(causal, acc, big), jnp.minimum))
        hi8 = jnp.maximum(hi8, fold8(sc, jnp.maximum))
        return lo8, hi8

    lo8, hi8 = lax.fori_loop(0, npair, idx_body, (jnp.full((SUBLANES, QBLK), big, F32),
                                                  jnp.full((SUBLANES, QBLK), NEG_INF, F32)))
    lo = jnp.min(lo8, axis=0, keepdims=True)
    hi = jnp.max(hi8, axis=0, keepdims=True)

    def count(pred):
        def body(c, acc):
            return acc + fold8(jnp.where(pred(sc_s[c]), 1.0, 0.0), jnp.add)
        acc = lax.fori_loop(0, npair, body, jnp.zeros((SUBLANES, QBLK), F32))
        return jnp.sum(acc, axis=0, keepdims=True)

    def max_where(pred):
        def body(c, acc):
            s = sc_s[c]
            return jnp.maximum(acc, fold8(jnp.where(pred(s), s, NEG_INF), jnp.maximum))
        acc = lax.fori_loop(0, npair, body, jnp.full((SUBLANES, QBLK), NEG_INF, F32))
        return jnp.max(acc, axis=0, keepdims=True)

    kf = jnp.minimum(float(topk), (qpos[0:1, :] + 1).astype(F32))

    def bisect(_, bounds):
        lo, hi = bounds
        mid = lo + 0.5 * (hi - lo)
        above = count(lambda s: s > mid) >= kf
        return jnp.where(above, mid, lo), jnp.where(above, hi, mid)

    lo, hi = lax.fori_loop(0, BISECT_STEPS, bisect, (lo, hi))

    thr = max_where(lambda s: s <= hi)
    cge = count(lambda s: s >= thr)

    def short(cge):
        return jnp.max(jnp.where(cge < kf, 1.0, 0.0)) > 0.5

    def peel(state):
        thr, cge, _ = state
        nxt = max_where(lambda s: s < thr)
        cnx = count(lambda s: s >= nxt)
        step = cge < kf
        thr = jnp.where(step, nxt, thr)
        cge = jnp.where(step, cnx, cge)
        return thr, cge, short(cge)

    thr, cge, _ = lax.while_loop(lambda st: st[2], peel, (thr, cge, short(cge)))
    need = kf - count(lambda s: s > thr)

    lower = jnp.where(lax.broadcasted_iota(I32, (KPAIR, KPAIR), 0)
                      >= lax.broadcasted_iota(I32, (KPAIR, KPAIR), 1), 1.0, 0.0).astype(BF16)

    tile_heads = lambda b: jnp.concatenate([b] * SA_HEADS, axis=1)

    def sel_body(c, carry):
        off, m8 = carry
        k0 = pl.multiple_of(c * KPAIR, KPAIR)
        q2 = q2_s[...]
        lg = jnp.concatenate([_mm_nt(ka_s[pl.ds(k0, KPAIR), :], q2),
                              _mm_nt(kb_s[pl.ds(k0, KPAIR), :], q2)], axis=1)
        lg_s[c] = lg
        s = sc_s[c]
        eqf = jnp.where(s == thr, 1.0, 0.0)
        pref = jnp.dot(lower, eqf.astype(BF16), preferred_element_type=F32)
        tie = jnp.where(pref + off <= need, eqf, 0.0)
        keep = jnp.where(k0 + krow <= qpos, jnp.where(s > thr, 1.0, tie), 0.0)
        bias = jnp.where(keep > 0.5, 0.0, NEG_INF)
        bias_s[c] = bias
        m8 = jnp.maximum(m8, fold8(lg + tile_heads(bias), jnp.maximum))
        return off + pref[KPAIR - 1:KPAIR, :], m8

    _, m8 = lax.fori_loop(0, npair, sel_body,
                          (jnp.zeros((1, QBLK), F32),
                           jnp.full((SUBLANES, SA_HEADS * QBLK), NEG_INF, F32)))
    mrow = jnp.max(m8, axis=0, keepdims=True)

    def probs(c):
        p = jnp.exp(lg_s[c] + tile_heads(bias_s[c]) - mrow)
        p_s[c % 2] = p.astype(BF16)
        return fold8(p, jnp.add)

    def pv(c):
        return jnp.dot(vt_s[c], p_s[c % 2], preferred_element_type=F32)

    def p2(c, l8):
        acc_s[...] += pv(c - 1)
        return l8 + probs(c)

    acc_s[...] = jnp.zeros(acc_s.shape, F32)
    l8 = lax.fori_loop(1, npair, p2, probs(0))
    out_t = (acc_s[...] + pv(npair - 1)) / jnp.sum(l8, axis=0, keepdims=True)
    order = _even_odd(SA_HEADS)
    for pr in range(SA_HEADS // 2):
        ev, od = order.index(2 * pr), order.index(2 * pr + 1)
        two = jnp.concatenate([out_t[:, ev * QBLK:(ev + 1) * QBLK],
                               out_t[:, od * QBLK:(od + 1) * QBLK]], axis=0)
        o_ref[0, :, pr * LANES:(pr + 1) * LANES] = two.T.astype(o_ref.dtype)


def _dsa(proj, pos3, qg_t, kg_p, rope_pat, topk):
    bsz, seq, _ = proj.shape
    nc = seq // CHUNK
    qw = SA_HEADS * SA_HEAD_DIM
    iw = IDX_HEADS * IDX_DIM
    assert seq % QBLK == 0 and QBLK == KPAIR
    npairs = seq // KPAIR
    pairs = lambda: pltpu.VMEM((npairs, KPAIR, QBLK), F32)
    head_of = jnp.arange(qw) // SA_HEAD_DIM
    head_sum = (head_of[:, None] == head_of[None, :]).astype(BF16)
    return pl.pallas_call(
        functools.partial(_dsa_kernel, topk),
        grid=(bsz, seq // QBLK),
        in_specs=[pl.BlockSpec((1, QBLK, qw), lambda b, j: (b, j, C_SAQ // qw)),
                  pl.BlockSpec((1, QBLK, iw), lambda b, j: (b, j, C_IXQ // iw)),
                  pl.BlockSpec((1, QBLK, LANES), lambda b, j: (b, j, C_MISC // LANES)),
                  pl.BlockSpec((1, seq, LANES), lambda b, j: (b, 0, C_KV // LANES)),
                  pl.BlockSpec((1, seq, LANES), lambda b, j: (b, 0, C_MISC // LANES)),
                  pl.BlockSpec((1, seq, 1), lambda b, j: (b, 0, 0)),
                  _const_spec((1, qw)), _const_spec((1, LANES)), _const_spec((SUBLANES, LANES)),
                  _const_spec((qw, qw))],
        out_specs=pl.BlockSpec((1, QBLK, qw), lambda b, j: (b, j, 0)),
        out_shape=jax.ShapeDtypeStruct((bsz, seq, qw), BF16),
        scratch_shapes=[pltpu.VMEM((seq, LANES), F32), pltpu.VMEM((seq, LANES), F32),
                        pltpu.VMEM((seq, LANES), BF16), pltpu.VMEM((seq, LANES), BF16),
                        pltpu.VMEM((npairs, SA_HEAD_DIM, KPAIR), BF16),
                        pltpu.VMEM((seq, LANES), BF16), pltpu.VMEM((seq, LANES), BF16),
                        pltpu.VMEM((qw // LANES * QBLK, LANES), BF16),
                        pltpu.VMEM((iw // LANES * QBLK, LANES), BF16),
                        pairs(), pairs(),
                        pltpu.VMEM((npairs, KPAIR, SA_HEADS * QBLK), F32),
                        pltpu.VMEM((2, KPAIR, SA_HEADS * QBLK), BF16),
                        pltpu.VMEM((SA_HEAD_DIM, SA_HEADS * QBLK), F32)],
        compiler_params=pltpu.CompilerParams(dimension_semantics=("arbitrary", "arbitrary"),
                                             vmem_limit_bytes=VMEM_LIMIT),
        name="dsa",
    )(proj, proj, proj, proj, proj, pos3, qg_t, kg_p, rope_pat, head_sum)


def _merge_kernel(x_ref, oa_ref, ob_ref, g_ref, wg_ref, bg_ref, wa_ref, wb_ref, wo_ref, o_ref):
    x = x_ref[...]
    hb = _rms(x, g_ref[...]).astype(BF16)
    oa = oa_ref[...]
    ob = ob_ref[...]
    acc = x
    step = 512
    for n in range(0, D_MODEL, step):
        ga = _sigmoid(jnp.dot(hb, wg_ref[:, n:n + step], preferred_element_type=F32)
                      + bg_ref[:, n:n + step])
        gb = _sigmoid(jnp.dot(hb, wg_ref[:, D_MODEL + n:D_MODEL + n + step],
                              preferred_element_type=F32)
                      + bg_ref[:, D_MODEL + n:D_MODEL + n + step])
        ya = jnp.dot(oa, wa_ref[:, n:n + step], preferred_element_type=F32)
        yb = jnp.dot(ob, wb_ref[:, n:n + step], preferred_element_type=F32)
        merged = (ga * ya + gb * yb).astype(BF16)
        acc = acc + jnp.dot(merged, wo_ref[n:n + step, :], preferred_element_type=F32)
    o_ref[...] = acc


def _merge(x2d, oa2d, ob2d, gain, w_gate, b_gate, w_a, w_b, w_o, tm):
    t = x2d.shape[0]
    tok = lambda w: pl.BlockSpec((tm, w), lambda i: (i, 0))
    return pl.pallas_call(
        _merge_kernel,
        grid=(t // tm,),
        in_specs=[tok(D_MODEL), tok(DN_V), tok(SA_HEADS * SA_HEAD_DIM),
                  _const_spec((1, D_MODEL)), _const_spec((D_MODEL, 2 * D_MODEL)),
                  _const_spec((1, 2 * D_MODEL)), _const_spec((DN_V, D_MODEL)),
                  _const_spec((SA_HEADS * SA_HEAD_DIM, D_MODEL)), _const_spec((D_MODEL, D_MODEL))],
        out_specs=tok(D_MODEL),
        out_shape=jax.ShapeDtypeStruct((t, D_MODEL), F32),
        compiler_params=pltpu.CompilerParams(dimension_semantics=("arbitrary",),
                                             vmem_limit_bytes=VMEM_LIMIT),
        name="merge",
    )(x2d, oa2d, ob2d, gain, w_gate, b_gate, w_a, w_b, w_o)


def _ffn_kernel(x_ref, p_ref, g2_ref, wg_ref, wu_ref, wd_ref, g3_ref, wpg_ref, wp_ref, o_ref):
    x = x_ref[...]
    hb = _rms(x, g2_ref[...]).astype(BF16)
    acc = x
    step = 256
    for n in range(0, D_FF, step):
        gate = jnp.dot(hb, wg_ref[:, n:n + step], preferred_element_type=F32)
        up = jnp.dot(hb, wu_ref[:, n:n + step], preferred_element_type=F32)
        acc = acc + jnp.dot((_silu(gate) * up).astype(BF16), wd_ref[n:n + step, :],
                            preferred_element_type=F32)
    h3 = _rms(acc, g3_ref[...]).astype(BF16)
    pg = _sigmoid(jnp.dot(h3, wpg_ref[...], preferred_element_type=F32))
    pe = jnp.dot(p_ref[...].astype(BF16), wp_ref[...], preferred_element_type=F32)
    o_ref[...] = acc + pg * pe


def _ffn(x2d, p2d, g2, w_g, w_u, w_d, g3, w_pg, w_p, tm):
    t = x2d.shape[0]
    tok = lambda w: pl.BlockSpec((tm, w), lambda i: (i, 0))
    return pl.pallas_call(
        _ffn_kernel,
        grid=(t // tm,),
        in_specs=[tok(D_MODEL), tok(PLE_DIM), _const_spec((1, D_MODEL)),
                  _const_spec((D_MODEL, D_FF)), _const_spec((D_MODEL, D_FF)),
                  _const_spec((D_FF, D_MODEL)), _const_spec((1, D_MODEL)),
                  _const_spec((D_MODEL, D_MODEL)), _const_spec((PLE_DIM, D_MODEL))],
        out_specs=tok(D_MODEL),
        out_shape=jax.ShapeDtypeStruct((t, D_MODEL), F32),
        compiler_params=pltpu.CompilerParams(dimension_semantics=("arbitrary",),
                                             vmem_limit_bytes=VMEM_LIMIT),
        name="ffn_ple",
    )(x2d, p2d, g2, w_g, w_u, w_d, g3, w_pg, w_p)


def _regroup_w_in(w):
    sizes = (DN_CONV_CH, DN_V, DN_HEADS, DN_HEADS, SA_HEADS * SA_HEAD_DIM, SA_HEAD_DIM, SA_HEAD_DIM,
             IDX_HEADS * IDX_DIM, IDX_DIM, IDX_HEADS)
    parts, off = [], 0
    for s in sizes:
        parts.append(w[:, off:off + s])
        off += s
    qkv, z, b, a, saq, sak, sav, ixq, ixk, ixw = parts
    cols = [qkv, z, saq, ixq, sak, sav, ixk, b, a, ixw]
    used = sum(c.shape[1] for c in cols)
    cols.append(jnp.zeros((w.shape[0], PROJ_W - used), w.dtype))
    return jnp.concatenate(cols, axis=1)


def _rope_pattern():
    lane = jnp.arange(LANES) % SA_HEAD_DIM
    inv_freq = ROPE_THETA ** (-jnp.arange(0, ROT, 2, dtype=F32) / ROT)
    freq = jnp.where(lane < ROT, inv_freq[lane % HALF], 0.0)
    s1 = jnp.where(lane < HALF, -1.0, 0.0)
    s2 = jnp.where((lane >= HALF) & (lane < ROT), 1.0, 0.0)
    pat = jnp.zeros((SUBLANES, LANES), F32)
    return pat.at[0].set(freq).at[1].set(s1).at[2].set(s2)


def _layer(x, p, positions, attn_norm, w_in, conv_w, a_log, dt_bias, dn_norm, q_norm, k_norm,
           w_branch_a, w_branch_b, w_gate, b_gate, w_o, ffn_norm, w_ffn_gate, w_ffn_up, w_ffn_down,
           ple_norm, w_ple_gate, w_ple):
    bsz, seq, _ = x.shape
    t = bsz * seq
    tm = 512
    topk = min(IDX_TOPK_MAX, seq // 4)
    row = lambda v: v.reshape(1, -1).astype(F32)
    x2d = x.reshape(t, D_MODEL)

    proj = _in_proj(x2d, row(attn_norm), _regroup_w_in(w_in).astype(BF16), tm)
    proj = proj.reshape(bsz, seq, PROJ_W)

    pad4 = jnp.zeros((DN_HEADS,), F32)
    per_head = lambda v: jnp.broadcast_to(jnp.concatenate([pad4, v.astype(F32)])[:, None],
                                          (SUBLANES, LANES))
    o_a = _deltanet(proj, conv_w.astype(F32), per_head(a_log), per_head(dt_bias), row(dn_norm))

    qg_t = jnp.tile(q_norm.astype(F32), SA_HEADS).reshape(1, -1)
    kg_p = jnp.concatenate([k_norm.astype(F32), jnp.ones((LANES - SA_HEAD_DIM,), F32)]).reshape(1, -1)
    pos3 = positions.astype(I32).reshape(bsz, seq, 1)
    o_b = _dsa(proj, pos3, qg_t, kg_p, _rope_pattern(), topk)

    x1 = _merge(x2d, o_a.reshape(t, DN_V), o_b.reshape(t, -1), row(attn_norm),
                w_gate.astype(BF16), row(b_gate), w_branch_a.astype(BF16), w_branch_b.astype(BF16),
                w_o.astype(BF16), tm)
    x2 = _ffn(x1, p.reshape(t, PLE_DIM), row(ffn_norm), w_ffn_gate.astype(BF16),
              w_ffn_up.astype(BF16), w_ffn_down.astype(BF16), row(ple_norm),
              w_ple_gate.astype(BF16), w_ple.astype(BF16), tm)
    return x2.reshape(bsz, seq, D_MODEL)


def kernel(x, p, positions, attn_norm, w_in, conv_w, a_log, dt_bias, dn_norm, q_norm, k_norm,
           w_branch_a, w_branch_b, w_gate, b_gate, w_o, ffn_norm, w_ffn_gate, w_ffn_up, w_ffn_down,
           ple_norm, w_ple_gate, w_ple):
    depth = w_in.shape[0]
    for i in range(depth):
        x = _layer(x, p[i], positions, attn_norm[i], w_in[i], conv_w[i], a_log[i], dt_bias[i],
                   dn_norm[i], q_norm[i], k_norm[i], w_branch_a[i], w_branch_b[i], w_gate[i],
                   b_gate[i], w_o[i], ffn_norm[i], w_ffn_gate[i], w_ffn_up[i], w_ffn_down[i],
                   ple_norm[i], w_ple_gate[i], w_ple[i])
    return x
```

```python
import functools

import jax
import jax.numpy as jnp
from jax import lax
from jax.experimental import pallas as pl
from jax.experimental.pallas import tpu as pltpu

F32 = jnp.float32
BF16 = jnp.bfloat16
I32 = jnp.int32

D_MODEL = 1024
PLE_DIM = 256
RMS_EPS = 1e-6
DN_HEADS = 4
DN_DK = 128
DN_DV = 128
DN_CONV = 4
DN_QK = DN_HEADS * DN_DK
DN_V = DN_HEADS * DN_DV
DN_CONV_CH = 2 * DN_QK + DN_V
SA_HEADS = 8
SA_HEAD_DIM = 64
IDX_HEADS = 4
IDX_DIM = 64
IDX_TOPK_MAX = 256
ROPE_THETA = 500000.0
ROPE_FRACTION = 4
NEG_INF = -1e30
D_FF = 2816

LANES = 128
SUBLANES = 8
CHUNK = 128
DN_GROUP = 8
DN_BASE = 16
KPAIR = 2 * CHUNK
QBLK = KPAIR
BISECT_STEPS = 19

C_QKV = 0
C_Z = DN_CONV_CH
C_SAQ = C_Z + DN_V
C_IXQ = C_SAQ + SA_HEADS * SA_HEAD_DIM
C_KV = C_IXQ + IDX_HEADS * IDX_DIM
C_MISC = C_KV + 2 * SA_HEAD_DIM
PROJ_W = C_MISC + LANES
M_IXK = 0
M_B = IDX_DIM
M_A = M_B + DN_HEADS
M_W = M_A + DN_HEADS

ROT = SA_HEAD_DIM // ROPE_FRACTION
HALF = ROT // 2

VMEM_LIMIT = 56 * 1024 * 1024


def _const_spec(shape):
    nd = len(shape)
    return pl.BlockSpec(shape, lambda *_: (0,) * nd, pipeline_mode=pl.Buffered(1))


def _rms(x, g):
    return x * lax.rsqrt(jnp.mean(x * x, axis=-1, keepdims=True) + RMS_EPS) * g


def _mm(a, b):
    return jnp.dot(a.astype(BF16), b.astype(BF16), preferred_element_type=F32)


def _mm_nt(a, b):
    return lax.dot_general(a.astype(BF16), b.astype(BF16), (((1,), (1,)), ((), ())),
                           preferred_element_type=F32)


def _sigmoid(x):
    return 1.0 / (1.0 + jnp.exp(-x))


def _silu(x):
    return x * _sigmoid(x)


def _softplus(x):
    return jnp.maximum(x, 0.0) + jnp.log1p(jnp.exp(-jnp.abs(x)))


def _in_proj_kernel(x_ref, g_ref, w_ref, o_ref):
    hb = _rms(x_ref[...], g_ref[...]).astype(BF16)
    step = 512
    for n in range(0, PROJ_W, step):
        o_ref[:, n:n + step] = jnp.dot(hb, w_ref[:, n:n + step], preferred_element_type=F32)


def _in_proj(x2d, gain, w_perm, tm):
    t = x2d.shape[0]
    return pl.pallas_call(
        _in_proj_kernel,
        grid=(t // tm,),
        in_specs=[pl.BlockSpec((tm, D_MODEL), lambda i: (i, 0)),
                  _const_spec((1, D_MODEL)),
                  _const_spec((D_MODEL, PROJ_W))],
        out_specs=pl.BlockSpec((tm, PROJ_W), lambda i: (i, 0)),
        out_shape=jax.ShapeDtypeStruct((t, PROJ_W), F32),
        compiler_params=pltpu.CompilerParams(dimension_semantics=("arbitrary",),
                                             vmem_limit_bytes=VMEM_LIMIT),
        name="in_proj",
    )(x2d, gain, w_perm)


def _dn_kernel(q_ref, k_ref, v_ref, z_ref, misc_ref, cwq_ref, cwk_ref, cwv_ref, alog_ref, dtb_ref,
               ng_ref, o_ref,
               pad_s, cq_s, ck_s, cv_s, brow_s, grow_s, u_s, w_s, a_s, qg_s, c_s, n_s, st_s, el_s):
    seq = q_ref.shape[1]
    nc = seq // CHUNK
    h = pl.program_id(1)
    row = lax.broadcasted_iota(I32, (CHUNK, CHUNK), 0)
    col = lax.broadcasted_iota(I32, (CHUNK, CHUNK), 1)

    pad_s[0:SUBLANES, :] = jnp.zeros((SUBLANES, LANES), F32)
    for src, cw_ref, dst, kind in ((q_ref, cwq_ref, cq_s, "q"), (k_ref, cwk_ref, ck_s, "k"),
                                   (v_ref, cwv_ref, cv_s, "v")):
        pad_s[SUBLANES:seq + SUBLANES, :] = src[0]
        for n in range(nc):
            base = n * CHUNK + SUBLANES
            y = cw_ref[DN_CONV - 1:DN_CONV, :] * pad_s[base:base + CHUNK, :]
            for j in range(1, DN_CONV):
                y = y + (cw_ref[DN_CONV - 1 - j:DN_CONV - j, :]
                         * pad_s[base - j:base - j + CHUNK, :])
            y = _silu(y)
            if kind != "v":
                y = y * lax.rsqrt(jnp.sum(y * y, axis=-1, keepdims=True) + 1e-6)
            if kind == "q":
                y = y * (DN_DK ** -0.5)
            dst[n * CHUNK:(n + 1) * CHUNK, :] = y

    upper = jnp.where(row <= col, 1.0, 0.0).astype(F32)
    sub8 = lax.broadcasted_iota(I32, (SUBLANES, LANES), 0)
    for n in range(nc):
        mt = misc_ref[0, n * CHUNK:(n + 1) * CHUNK, :].T
        slab = mt[M_B:M_B + SUBLANES, :]
        beta8 = _sigmoid(slab)
        g8 = -jnp.exp(alog_ref[...]) * _softplus(slab + dtb_ref[...])
        gc8 = jnp.dot(g8, upper, preferred_element_type=F32, precision=lax.Precision.HIGHEST)
        b_row = jnp.sum(jnp.where(sub8 == h, beta8, 0.0), axis=0, keepdims=True)
        g_row = jnp.sum(jnp.where(sub8 == h + DN_HEADS, gc8, 0.0), axis=0, keepdims=True)
        brow_s[n] = jnp.broadcast_to(b_row, (SUBLANES, LANES))
        grow_s[n] = jnp.broadcast_to(g_row, (SUBLANES, LANES))

    eye = jnp.where(row == col, 1.0, 0.0).astype(F32)

    def prep_one(n):
        r0 = pl.multiple_of(n * CHUNK, CHUNK)
        q = cq_s[pl.ds(r0, CHUNK), :]
        k = ck_s[pl.ds(r0, CHUNK), :]
        v = cv_s[pl.ds(r0, CHUNK), :]
        g_r = jnp.broadcast_to(grow_s[n][0:1, :], (CHUNK, CHUNK))
        g_c = g_r.T
        b_c = jnp.broadcast_to(brow_s[n][0:1, :], (CHUNK, CHUNK)).T
        g_last = jnp.broadcast_to(g_c[CHUNK - 1:CHUNK, :], (CHUNK, CHUNK))
        decay = jnp.where(row >= col, jnp.exp(jnp.minimum(g_c - g_r, 0.0)), 0.0)
        kb = k * b_c
        kq = _mm_nt(jnp.concatenate([kb, q], axis=0), k)
        yield
        lmat = jnp.where(row > col, kq[:CHUNK] * decay, 0.0)
        a = jnp.where(row >= col, kq[CHUNK:] * decay, 0.0)
        same = lambda b: (row // b) == (col // b)
        l0 = jnp.where(same(DN_BASE), lmat, 0.0)
        x = eye - l0
        m = _mm(l0, l0)
        yield
        power = 2
        while 2 * power < DN_BASE:
            xm = _mm(jnp.concatenate([x, m], axis=0), m)
            yield
            x = x + xm[:CHUNK]
            m = xm[CHUNK:]
            power *= 2
        x = x + _mm(x, m)
        yield
        blk = DN_BASE
        while blk < CHUNK:
            below = jnp.where(same(2 * blk) & jnp.logical_not(same(blk)), lmat, 0.0)
            y = _mm(below, x)
            yield
            x = x - _mm(x, y)
            yield
            blk *= 2
        eg = jnp.exp(g_c)
        sol = _mm(x, jnp.concatenate([v * b_c, kb * eg], axis=1))
        yield
        cn = _mm((k * jnp.exp(g_last - g_c)).T, sol)
        yield
        u_s[n] = sol[:, :DN_DV]
        w_s[n] = sol[:, DN_DV:]
        a_s[n] = a
        qg_s[n] = q * eg
        c_s[n] = cn[:, :DN_DV]
        n_s[n] = cn[:, DN_DV:]
        el_s[n] = jnp.exp(g_last)[0:SUBLANES, :]

    def grouped(gen):
        def body(i, carry):
            live = [gen(i * DN_GROUP + j) for j in range(DN_GROUP)]
            while live:
                live = [g for g in live if next(g, True) is None]
            return carry
        lax.fori_loop(0, nc // DN_GROUP, body, 0)

    grouped(prep_one)

    def chain(n, state):
        st_s[n] = state
        el = jnp.broadcast_to(el_s[n][0:1, :], (CHUNK, CHUNK))
        return state * el + c_s[n] - _mm(n_s[n], state)

    lax.fori_loop(0, nc, chain, jnp.zeros((DN_DK, DN_DV), F32))

    def emit_one(n):
        r0 = pl.multiple_of(n * CHUNK, CHUNK)
        ws = _mm(jnp.concatenate([w_s[n], qg_s[n]], axis=0), st_s[n])
        yield
        o = ws[CHUNK:] + _mm(a_s[n], u_s[n] - ws[:CHUNK])
        yield
        z = z_ref[0, pl.ds(r0, CHUNK), :]
        o_ref[0, pl.ds(r0, CHUNK), :] = (_rms(o, ng_ref[...]) * _silu(z)).astype(o_ref.dtype)

    grouped(emit_one)


def _deltanet(proj, conv_w, alog8, dtb8, norm_g):
    bsz, seq, _ = proj.shape
    nc = seq // CHUNK
    qb, kb, vb, zb = (C_QKV // LANES, (C_QKV + DN_QK) // LANES, (C_QKV + 2 * DN_QK) // LANES,
                      C_Z // LANES)
    col = lambda off: pl.BlockSpec((1, seq, LANES), lambda b, h, off=off: (b, 0, off + h))
    cw = lambda off: pl.BlockSpec((DN_CONV, LANES), lambda b, h, off=off: (0, off + h))
    mat = lambda: pltpu.VMEM((nc, CHUNK, CHUNK), F32)
    return pl.pallas_call(
        _dn_kernel,
        grid=(bsz, DN_HEADS),
        in_specs=[col(qb), col(kb), col(vb), col(zb),
                  pl.BlockSpec((1, seq, LANES), lambda b, h: (b, 0, C_MISC // LANES)),
                  cw(qb), cw(kb), cw(vb),
                  _const_spec((SUBLANES, LANES)), _const_spec((SUBLANES, LANES)),
                  _const_spec((1, DN_DV))],
        out_specs=pl.BlockSpec((1, seq, LANES), lambda b, h: (b, 0, h)),
        out_shape=jax.ShapeDtypeStruct((bsz, seq, DN_V), BF16),
        scratch_shapes=[pltpu.VMEM((seq + SUBLANES, LANES), F32),
                        pltpu.VMEM((seq, LANES), F32), pltpu.VMEM((seq, LANES), F32),
                        pltpu.VMEM((seq, LANES), F32),
                        pltpu.VMEM((nc, SUBLANES, LANES), F32), pltpu.VMEM((nc, SUBLANES, LANES), F32),
                        mat(), mat(), mat(), mat(), mat(), mat(), mat(),
                        pltpu.VMEM((nc, SUBLANES, LANES), F32)],
        compiler_params=pltpu.CompilerParams(dimension_semantics=("arbitrary", "arbitrary"),
                                             vmem_limit_bytes=VMEM_LIMIT),
        name="deltanet",
    )(proj, proj, proj, proj, proj, conv_w, conv_w, conv_w, alog8, dtb8, norm_g)


def _even_odd(n):
    return list(range(0, n, 2)) + list(range(1, n, 2))


def _rope(x, c, s, s1, s2):
    w = x.shape[-1]
    return x * c + pltpu.roll(x, w - HALF, 1) * (s * s1) + pltpu.roll(x, HALF, 1) * (s * s2)


def _dsa_kernel(topk, q_ref, qi_ref, misct_ref, kv_ref, miscf_ref, pos_ref, qg_ref, kg_ref, rp_ref,
                hsum_ref, o_ref,
                cos_s, sin_s, ka_s, kb_s, vt_s, kia_s, kib_s, q2_s, qi2_s, sc_s, bias_s, lg_s, p_s,
                acc_s):
    seq = kv_ref.shape[1]
    nc = seq // CHUNK
    qb = pl.program_id(1)
    npair = qb + 1
    r0 = pl.multiple_of(qb * QBLK, QBLK)
    col = lax.broadcasted_iota(I32, (CHUNK, CHUNK), 1)
    freq = rp_ref[0:1, :]
    s1 = rp_ref[1:2, :]
    s2 = rp_ref[2:3, :]

    @pl.when(qb == 0)
    def _():
        for n in range(nc):
            rows = slice(n * CHUNK, (n + 1) * CHUNK)
            ang = pos_ref[0, rows, :].astype(F32) * freq
            c = jnp.cos(ang)
            s = jnp.sin(ang)
            cos_s[rows, :] = c
            sin_s[rows, :] = s
            ck = jnp.where(col < ROT, c, 1.0)
            sk = jnp.where(col < ROT, s, 0.0)
            kv = kv_ref[0, rows, :]
            ss = jnp.sum(jnp.where(col < SA_HEAD_DIM, kv * kv, 0.0), axis=-1, keepdims=True)
            r = lax.rsqrt(ss * (1.0 / SA_HEAD_DIM) + RMS_EPS)
            y = jnp.where(col < SA_HEAD_DIM, kv * r * kg_ref[...], kv)
            y = _rope(y, ck, sk, s1, s2)
            ka = jnp.where(col < SA_HEAD_DIM, y, 0.0)
            ka_s[rows, :] = ka.astype(BF16)
            kb_s[rows, :] = pltpu.roll(ka, SA_HEAD_DIM, 1).astype(BF16)
            half = (n % 2) * CHUNK
            vt_s[n // 2, :, half:half + CHUNK] = y.T[SA_HEAD_DIM:, :].astype(BF16)
            yi = _rope(miscf_ref[0, rows, :], ck, sk, s1, s2)
            kia = jnp.where(col < IDX_DIM, yi, 0.0)
            kia_s[rows, :] = kia.astype(BF16)
            kib_s[rows, :] = pltpu.roll(kia, IDX_DIM, 1).astype(BF16)

    c1 = cos_s[pl.ds(r0, QBLK), :]
    sn1 = sin_s[pl.ds(r0, QBLK), :]
    rep = lambda a, k: jnp.concatenate([a] * k, axis=1)
    nq = SA_HEADS * SA_HEAD_DIM // LANES
    xq = q_ref[0]
    yq = _rope(xq * qg_ref[...], rep(c1, nq), rep(sn1, nq), rep(s1, nq), rep(s2, nq))
    ss = jnp.dot((xq * xq).astype(BF16), hsum_ref[...], preferred_element_type=F32)
    r = lax.rsqrt(ss * (1.0 / SA_HEAD_DIM) + RMS_EPS)
    qn = (yq * (r * (SA_HEAD_DIM ** -0.5))).astype(BF16)
    for pr in range(nq):
        q2_s[pr * QBLK:(pr + 1) * QBLK, :] = qn[:, pr * LANES:(pr + 1) * LANES]
    ni = IDX_HEADS * IDX_DIM // LANES
    yi = _rope(qi_ref[0], rep(c1, ni), rep(sn1, ni), rep(s1, ni), rep(s2, ni)).astype(BF16)
    for pr in range(ni):
        qi2_s[pr * QBLK:(pr + 1) * QBLK, :] = yi[:, pr * LANES:(pr + 1) * LANES]
    mt = misct_ref[0].T
    wscale = (IDX_HEADS ** -0.5) * (IDX_DIM ** -0.5)
    wrow = jnp.concatenate([mt[M_W + hh:M_W + hh + 1, :] for hh in _even_odd(IDX_HEADS)],
                           axis=1) * wscale

    krow = lax.broadcasted_iota(I32, (KPAIR, QBLK), 0)
    qpos = r0 + lax.broadcasted_iota(I32, (KPAIR, QBLK), 1)
    big = -NEG_INF

    def fold8(x, op):
        parts = [x[i:i + SUBLANES] for i in range(0, x.shape[0], SUBLANES)]
        while len(parts) > 1:
            parts = [op(parts[i], parts[i + 1]) for i in range(0, len(parts), 2)]
        return parts[0]

    def idx_body(c, carry):
        k0 = pl.multiple_of(c * KPAIR, KPAIR)
        qi2 = qi2_s[...]
        lt = jnp.concatenate([_mm_nt(kia_s[pl.ds(k0, KPAIR), :], qi2),
                              _mm_nt(kib_s[pl.ds(k0, KPAIR), :], qi2)], axis=1)
        wl = jnp.maximum(lt, 0.0) * wrow
        acc = wl[:, 0:QBLK]
        for hh in range(1, IDX_HEADS):
            acc = acc + wl[:, hh * QBLK:(hh + 1) * QBLK]
        causal = k0 + krow <= qpos
        sc = jnp.where(causal, acc, NEG_INF)
        sc_s[c] = sc
        lo8, hi8 = carry
        lo8 = jnp.minimum(lo8, fold8(jnp.where(causal, acc, big), jnp.minimum))
        hi8 = jnp.maximum(hi8, fold8(sc, jnp.maximum))
        return lo8, hi8

    lo8, hi8 = lax.fori_loop(0, npair, idx_body, (jnp.full((SUBLANES, QBLK), big, F32),
                                                  jnp.full((SUBLANES, QBLK), NEG_INF, F32)))
    lo = jnp.min(lo8, axis=0, keepdims=True)
    hi = jnp.max(hi8, axis=0, keepdims=True)

    def count(pred):
        def body(c, acc):
            return acc + fold8(jnp.where(pred(sc_s[c]), 1.0, 0.0), jnp.add)
        acc = lax.fori_loop(0, npair, body, jnp.zeros((SUBLANES, QBLK), F32))
        return jnp.sum(acc, axis=0, keepdims=True)

    def max_where(pred):
        def body(c, acc):
            s = sc_s[c]
            return jnp.maximum(acc, fold8(jnp.where(pred(s), s, NEG_INF), jnp.maximum))
        acc = lax.fori_loop(0, npair, body, jnp.full((SUBLANES, QBLK), NEG_INF, F32))
        return jnp.max(acc, axis=0, keepdims=True)

    kf = jnp.minimum(float(topk), (qpos[0:1, :] + 1).astype(F32))

    def bisect(_, bounds):
        lo, hi = bounds
        mid = lo + 0.5 * (hi - lo)
        above = count(lambda s: s > mid) >= kf
        return jnp.where(above, mid, lo), jnp.where(above, hi, mid)

    lo, hi = lax.fori_loop(0, BISECT_STEPS, bisect, (lo, hi))

    thr = max_where(lambda s: s <= hi)
    cge = count(lambda s: s >= thr)

    def short(cge):
        return jnp.max(jnp.where(cge < kf, 1.0, 0.0)) > 0.5

    def peel(state):
        thr, cge, _ = state
        nxt = max_where(lambda s: s < thr)
        cnx = count(lambda s: s >= nxt)
        step = cge < kf
        thr = jnp.where(step, nxt, thr)
        cge = jnp.where(step, cnx, cge)
        return thr, cge, short(cge)

    thr, cge, _ = lax.while_loop(lambda st: st[2], peel, (thr, cge, short(cge)))
    need = kf - count(lambda s: s > thr)

    lower = jnp.where(lax.broadcasted_iota(I32, (KPAIR, KPAIR), 0)
                      >= lax.broadcasted_iota(I32, (KPAIR, KPAIR), 1), 1.0, 0.0).astype(BF16)

    tile_heads = lambda b: jnp.concatenate([b] * SA_HEADS, axis=1)

    def sel_body(c, carry):
        off, m8 = carry
        k0 = pl.multiple_of(c * KPAIR, KPAIR)
        q2 = q2_s[...]
        lg = jnp.concatenate([_mm_nt(ka_s[pl.ds(k0, KPAIR), :], q2),
                              _mm_nt(kb_s[pl.ds(k0, KPAIR), :], q2)], axis=1)
        lg_s[c] = lg
        s = sc_s[c]
        eqf = jnp.where(s == thr, 1.0, 0.0)
        pref = jnp.dot(lower, eqf.astype(BF16), preferred_element_type=F32)
        tie = jnp.where(pref + off <= need, eqf, 0.0)
        keep = jnp.where(k0 + krow <= qpos, jnp.where(s > thr, 1.0, tie), 0.0)
        bias = jnp.where(keep > 0.5, 0.0, NEG_INF)
        bias_s[c] = bias
        m8 = jnp.maximum(m8, fold8(lg + tile_heads(bias), jnp.maximum))
        return off + pref[KPAIR - 1:KPAIR, :], m8

    _, m8 = lax.fori_loop(0, npair, sel_body,
                          (jnp.zeros((1, QBLK), F32),
                           jnp.full((SUBLANES, SA_HEADS * QBLK), NEG_INF, F32)))
    mrow = jnp.max(m8, axis=0, keepdims=True)

    def probs(c):
        p = jnp.exp(lg_s[c] + tile_heads(bias_s[c]) - mrow)
        p_s[c % 2] = p.astype(BF16)
        return fold8(p, jnp.add)

    def pv(c):
        return jnp.dot(vt_s[c], p_s[c % 2], preferred_element_type=F32)

    def p2(c, l8):
        acc_s[...] += pv(c - 1)
        return l8 + probs(c)

    acc_s[...] = jnp.zeros(acc_s.shape, F32)
    l8 = lax.fori_loop(1, npair, p2, probs(0))
    out_t = (acc_s[...] + pv(npair - 1)) / jnp.sum(l8, axis=0, keepdims=True)
    order = _even_odd(SA_HEADS)
    for pr in range(SA_HEADS // 2):
        ev, od = order.index(2 * pr), order.index(2 * pr + 1)
        two = jnp.concatenate([out_t[:, ev * QBLK:(ev + 1) * QBLK],
                               out_t[:, od * QBLK:(od + 1) * QBLK]], axis=0)
        o_ref[0, :, pr * LANES:(pr + 1) * LANES] = two.T.astype(o_ref.dtype)


def _dsa(proj, pos3, qg_t, kg_p, rope_pat, topk):
    bsz, seq, _ = proj.shape
    nc = seq // CHUNK
    qw = SA_HEADS * SA_HEAD_DIM
    iw = IDX_HEADS * IDX_DIM
    assert seq % QBLK == 0 and QBLK == KPAIR
    npairs = seq // KPAIR
    pairs = lambda: pltpu.VMEM((npairs, KPAIR, QBLK), F32)
    head_of = jnp.arange(qw) // SA_HEAD_DIM
    head_sum = (head_of[:, None] == head_of[None, :]).astype(BF16)
    return pl.pallas_call(
        functools.partial(_dsa_kernel, topk),
        grid=(bsz, seq // QBLK),
        in_specs=[pl.BlockSpec((1, QBLK, qw), lambda b, j: (b, j, C_SAQ // qw)),
                  pl.BlockSpec((1, QBLK, iw), lambda b, j: (b, j, C_IXQ // iw)),
                  pl.BlockSpec((1, QBLK, LANES), lambda b, j: (b, j, C_MISC // LANES)),
                  pl.BlockSpec((1, seq, LANES), lambda b, j: (b, 0, C_KV // LANES)),
                  pl.BlockSpec((1, seq, LANES), lambda b, j: (b, 0, C_MISC // LANES)),
                  pl.BlockSpec((1, seq, 1), lambda b, j: (b, 0, 0)),
                  _const_spec((1, qw)), _const_spec((1, LANES)), _const_spec((SUBLANES, LANES)),
                  _const_spec((qw, qw))],
        out_specs=pl.BlockSpec((1, QBLK, qw), lambda b, j: (b, j, 0)),
        out_shape=jax.ShapeDtypeStruct((bsz, seq, qw), BF16),
        scratch_shapes=[pltpu.VMEM((seq, LANES), F32), pltpu.VMEM((seq, LANES), F32),
                        pltpu.VMEM((seq, LANES), BF16), pltpu.VMEM((seq, LANES), BF16),
                        pltpu.VMEM((npairs, SA_HEAD_DIM, KPAIR), BF16),
                        pltpu.VMEM((seq, LANES), BF16), pltpu.VMEM((seq, LANES), BF16),
                        pltpu.VMEM((qw // LANES * QBLK, LANES), BF16),
                        pltpu.VMEM((iw // LANES * QBLK, LANES), BF16),
                        pairs(), pairs(),
                        pltpu.VMEM((npairs, KPAIR, SA_HEADS * QBLK), F32),
                        pltpu.VMEM((2, KPAIR, SA_HEADS * QBLK), BF16),
                        pltpu.VMEM((SA_HEAD_DIM, SA_HEADS * QBLK), F32)],
        compiler_params=pltpu.CompilerParams(dimension_semantics=("arbitrary", "arbitrary"),
                                             vmem_limit_bytes=VMEM_LIMIT),
        name="dsa",
    )(proj, proj, proj, proj, proj, pos3, qg_t, kg_p, rope_pat, head_sum)


def _merge_kernel(x_ref, oa_ref, ob_ref, g_ref, wg_ref, bg_ref, wa_ref, wb_ref, wo_ref, o_ref):
    x = x_ref[...]
    hb = _rms(x, g_ref[...]).astype(BF16)
    oa = oa_ref[...]
    ob = ob_ref[...]
    acc = x
    step = 512
    for n in range(0, D_MODEL, step):
        ga = _sigmoid(jnp.dot(hb, wg_ref[:, n:n + step], preferred_element_type=F32)
                      + bg_ref[:, n:n + step])
        gb = _sigmoid(jnp.dot(hb, wg_ref[:, D_MODEL + n:D_MODEL + n + step],
                              preferred_element_type=F32)
                      + bg_ref[:, D_MODEL + n:D_MODEL + n + step])
        ya = jnp.dot(oa, wa_ref[:, n:n + step], preferred_element_type=F32)
        yb = jnp.dot(ob, wb_ref[:, n:n + step], preferred_element_type=F32)
        merged = (ga * ya + gb * yb).astype(BF16)
        acc = acc + jnp.dot(merged, wo_ref[n:n + step, :], preferred_element_type=F32)
    o_ref[...] = acc


def _merge(x2d, oa2d, ob2d, gain, w_gate, b_gate, w_a, w_b, w_o, tm):
    t = x2d.shape[0]
    tok = lambda w: pl.BlockSpec((tm, w), lambda i: (i, 0))
    return pl.pallas_call(
        _merge_kernel,
        grid=(t // tm,),
        in_specs=[tok(D_MODEL), tok(DN_V), tok(SA_HEADS * SA_HEAD_DIM),
                  _const_spec((1, D_MODEL)), _const_spec((D_MODEL, 2 * D_MODEL)),
                  _const_spec((1, 2 * D_MODEL)), _const_spec((DN_V, D_MODEL)),
                  _const_spec((SA_HEADS * SA_HEAD_DIM, D_MODEL)), _const_spec((D_MODEL, D_MODEL))],
        out_specs=tok(D_MODEL),
        out_shape=jax.ShapeDtypeStruct((t, D_MODEL), F32),
        compiler_params=pltpu.CompilerParams(dimension_semantics=("arbitrary",),
                                             vmem_limit_bytes=VMEM_LIMIT),
        name="merge",
    )(x2d, oa2d, ob2d, gain, w_gate, b_gate, w_a, w_b, w_o)


def _ffn_kernel(x_ref, p_ref, g2_ref, wg_ref, wu_ref, wd_ref, g3_ref, wpg_ref, wp_ref, o_ref):
    x = x_ref[...]
    hb = _rms(x, g2_ref[...]).astype(BF16)
    acc = x
    step = 256
    for n in range(0, D_FF, step):
        gate = jnp.dot(hb, wg_ref[:, n:n + step], preferred_element_type=F32)
        up = jnp.dot(hb, wu_ref[:, n:n + step], preferred_element_type=F32)
        acc = acc + jnp.dot((_silu(gate) * up).astype(BF16), wd_ref[n:n + step, :],
                            preferred_element_type=F32)
    h3 = _rms(acc, g3_ref[...]).astype(BF16)
    pg = _sigmoid(jnp.dot(h3, wpg_ref[...], preferred_element_type=F32))
    pe = jnp.dot(p_ref[...].astype(BF16), wp_ref[...], preferred_element_type=F32)
    o_ref[...] = acc + pg * pe


def _ffn(x2d, p2d, g2, w_g, w_u, w_d, g3, w_pg, w_p, tm):
    t = x2d.shape[0]
    tok = lambda w: pl.BlockSpec((tm, w), lambda i: (i, 0))
    return pl.pallas_call(
        _ffn_kernel,
        grid=(t // tm,),
        in_specs=[tok(D_MODEL), tok(PLE_DIM), _const_spec((1, D_MODEL)),
                  _const_spec((D_MODEL, D_FF)), _const_spec((D_MODEL, D_FF)),
                  _const_spec((D_FF, D_MODEL)), _const_spec((1, D_MODEL)),
                  _const_spec((D_MODEL, D_MODEL)), _const_spec((PLE_DIM, D_MODEL))],
        out_specs=tok(D_MODEL),
        out_shape=jax.ShapeDtypeStruct((t, D_MODEL), F32),
        compiler_params=pltpu.CompilerParams(dimension_semantics=("arbitrary",),
                                             vmem_limit_bytes=VMEM_LIMIT),
        name="ffn_ple",
    )(x2d, p2d, g2, w_g, w_u, w_d, g3, w_pg, w_p)


def _regroup_w_in(w):
    sizes = (DN_CONV_CH, DN_V, DN_HEADS, DN_HEADS, SA_HEADS * SA_HEAD_DIM, SA_HEAD_DIM, SA_HEAD_DIM,
             IDX_HEADS * IDX_DIM, IDX_DIM, IDX_HEADS)
    parts, off = [], 0
    for s in sizes:
        parts.append(w[:, off:off + s])
        off += s
    qkv, z, b, a, saq, sak, sav, ixq, ixk, ixw = parts
    cols = [qkv, z, saq, ixq, sak, sav, ixk, b, a, ixw]
    used = sum(c.shape[1] for c in cols)
    cols.append(jnp.zeros((w.shape[0], PROJ_W - used), w.dtype))
    return jnp.concatenate(cols, axis=1)


def _rope_pattern():
    lane = jnp.arange(LANES) % SA_HEAD_DIM
    inv_freq = ROPE_THETA ** (-jnp.arange(0, ROT, 2, dtype=F32) / ROT)
    freq = jnp.where(lane < ROT, inv_freq[lane % HALF], 0.0)
    s1 = jnp.where(lane < HALF, -1.0, 0.0)
    s2 = jnp.where((lane >= HALF) & (lane < ROT), 1.0, 0.0)
    pat = jnp.zeros((SUBLANES, LANES), F32)
    return pat.at[0].set(freq).at[1].set(s1).at[2].set(s2)


def _layer(x, p, positions, attn_norm, w_in, conv_w, a_log, dt_bias, dn_norm, q_norm, k_norm,
           w_branch_a, w_branch_b, w_gate, b_gate, w_o, ffn_norm, w_ffn_gate, w_ffn_up, w_ffn_down,
           ple_norm, w_ple_gate, w_ple):
    bsz, seq, _ = x.shape
    t = bsz * seq
    tm = 512
    topk = min(IDX_TOPK_MAX, seq // 4)
    row = lambda v: v.reshape(1, -1).astype(F32)
    x2d = x.reshape(t, D_MODEL)

    proj = _in_proj(x2d, row(attn_norm), _regroup_w_in(w_in).astype(BF16), tm)
    proj = proj.reshape(bsz, seq, PROJ_W)

    pad4 = jnp.zeros((DN_HEADS,), F32)
    per_head = lambda v: jnp.broadcast_to(jnp.concatenate([pad4, v.astype(F32)])[:, None],
                                          (SUBLANES, LANES))
    o_a = _deltanet(proj, conv_w.astype(F32), per_head(a_log), per_head(dt_bias), row(dn_norm))

    qg_t = jnp.tile(q_norm.astype(F32), SA_HEADS).reshape(1, -1)
    kg_p = jnp.concatenate([k_norm.astype(F32), jnp.ones((LANES - SA_HEAD_DIM,), F32)]).reshape(1, -1)
    pos3 = positions.astype(I32).reshape(bsz, seq, 1)
    o_b = _dsa(proj, pos3, qg_t, kg_p, _rope_pattern(), topk)

    x1 = _merge(x2d, o_a.reshape(t, DN_V), o_b.reshape(t, -1), row(attn_norm),
                w_gate.astype(BF16), row(b_gate), w_branch_a.astype(BF16), w_branch_b.astype(BF16),
                w_o.astype(BF16), tm)
    x2 = _ffn(x1, p.reshape(t, PLE_DIM), row(ffn_norm), w_ffn_gate.astype(BF16),
              w_ffn_up.astype(BF16), w_ffn_down.astype(BF16), row(ple_norm),
              w_ple_gate.astype(BF16), w_ple.astype(BF16), tm)
    return x2.reshape(bsz, seq, D_MODEL)


def kernel(x, p, positions, attn_norm, w_in, conv_w, a_log, dt_bias, dn_norm, q_norm, k_norm,
           w_branch_a, w_branch_b, w_gate, b_gate, w_o, ffn_norm, w_ffn_gate, w_ffn_up, w_ffn_down,
           ple_norm, w_ple_gate, w_ple):
    depth = w_in.shape[0]
    for i in range(depth):
        x = _layer(x, p[i], positions, attn_norm[i], w_in[i], conv_w[i], a_log[i], dt_bias[i],
                   dn_norm[i], q_norm[i], k_norm[i], w_branch_a[i], w_branch_b[i], w_gate[i],
                   b_gate[i], w_o[i], ffn_norm[i], w_ffn_gate[i], w_ffn_up[i], w_ffn_down[i],
                   ple_norm[i], w_ple_gate[i], w_ple[i])
    return x
```

```python
import functools

import jax
import jax.numpy as jnp
from jax import lax
from jax.experimental import pallas as pl
from jax.experimental.pallas import tpu as pltpu

F32 = jnp.float32
BF16 = jnp.bfloat16
I32 = jnp.int32

D_MODEL = 1024
PLE_DIM = 256
RMS_EPS = 1e-6
DN_HEADS = 4
DN_DK = 128
DN_DV = 128
DN_CONV = 4
DN_QK = DN_HEADS * DN_DK
DN_V = DN_HEADS * DN_DV
DN_CONV_CH = 2 * DN_QK + DN_V
SA_HEADS = 8
SA_HEAD_DIM = 64
IDX_HEADS = 4
IDX_DIM = 64
IDX_TOPK_MAX = 256
ROPE_THETA = 500000.0
ROPE_FRACTION = 4
NEG_INF = -1e30
D_FF = 2816

LANES = 128
SUBLANES = 8
CHUNK = 128
DN_GROUP = 8
DN_BASE = 16
KPAIR = 2 * CHUNK
QBLK = KPAIR
BISECT_STEPS = 19

C_QKV = 0
C_Z = DN_CONV_CH
C_SAQ = C_Z + DN_V
C_IXQ = C_SAQ + SA_HEADS * SA_HEAD_DIM
C_KV = C_IXQ + IDX_HEADS * IDX_DIM
C_MISC = C_KV + 2 * SA_HEAD_DIM
PROJ_W = C_MISC + LANES
M_IXK = 0
M_B = IDX_DIM
M_A = M_B + DN_HEADS
M_W = M_A + DN_HEADS

ROT = SA_HEAD_DIM // ROPE_FRACTION
HALF = ROT // 2

VMEM_LIMIT = 56 * 1024 * 1024


def _const_spec(shape):
    nd = len(shape)
    return pl.BlockSpec(shape, lambda *_: (0,) * nd, pipeline_mode=pl.Buffered(1))


def _rms(x, g):
    return x * lax.rsqrt(jnp.mean(x * x, axis=-1, keepdims=True) + RMS_EPS) * g


def _mm(a, b):
    return jnp.dot(a.astype(BF16), b.astype(BF16), preferred_element_type=F32)


def _mm_nt(a, b):
    return lax.dot_general(a.astype(BF16), b.astype(BF16), (((1,), (1,)), ((), ())),
                           preferred_element_type=F32)


def _sigmoid(x):
    return 1.0 / (1.0 + jnp.exp(-x))


def _silu(x):
    return x * _sigmoid(x)


def _softplus(x):
    return jnp.maximum(x, 0.0) + jnp.log1p(jnp.exp(-jnp.abs(x)))


def _in_proj_kernel(x_ref, g_ref, w_ref, o_ref):
    hb = _rms(x_ref[...], g_ref[...]).astype(BF16)
    step = 512
    for n in range(0, PROJ_W, step):
        o_ref[:, n:n + step] = jnp.dot(hb, w_ref[:, n:n + step], preferred_element_type=F32)


def _in_proj(x2d, gain, w_perm, tm):
    t = x2d.shape[0]
    return pl.pallas_call(
        _in_proj_kernel,
        grid=(t // tm,),
        in_specs=[pl.BlockSpec((tm, D_MODEL), lambda i: (i, 0)),
                  _const_spec((1, D_MODEL)),
                  _const_spec((D_MODEL, PROJ_W))],
        out_specs=pl.BlockSpec((tm, PROJ_W), lambda i: (i, 0)),
        out_shape=jax.ShapeDtypeStruct((t, PROJ_W), F32),
        compiler_params=pltpu.CompilerParams(dimension_semantics=("arbitrary",),
                                             vmem_limit_bytes=VMEM_LIMIT),
        name="in_proj",
    )(x2d, gain, w_perm)


def _dn_kernel(q_ref, k_ref, v_ref, z_ref, misc_ref, cwq_ref, cwk_ref, cwv_ref, alog_ref, dtb_ref,
               ng_ref, o_ref,
               pad_s, cq_s, ck_s, cv_s, brow_s, grow_s, u_s, w_s, a_s, qg_s, c_s, n_s, st_s, el_s):
    seq = q_ref.shape[1]
    nc = seq // CHUNK
    h = pl.program_id(1)
    row = lax.broadcasted_iota(I32, (CHUNK, CHUNK), 0)
    col = lax.broadcasted_iota(I32, (CHUNK, CHUNK), 1)

    pad_s[0:SUBLANES, :] = jnp.zeros((SUBLANES, LANES), F32)
    for src, cw_ref, dst, kind in ((q_ref, cwq_ref, cq_s, "q"), (k_ref, cwk_ref, ck_s, "k"),
                                   (v_ref, cwv_ref, cv_s, "v")):
        pad_s[SUBLANES:seq + SUBLANES, :] = src[0]
        for n in range(nc):
            base = n * CHUNK + SUBLANES
            y = cw_ref[DN_CONV - 1:DN_CONV, :] * pad_s[base:base + CHUNK, :]
            for j in range(1, DN_CONV):
                y = y + (cw_ref[DN_CONV - 1 - j:DN_CONV - j, :]
                         * pad_s[base - j:base - j + CHUNK, :])
            y = _silu(y)
            if kind != "v":
                y = y * lax.rsqrt(jnp.sum(y * y, axis=-1, keepdims=True) + 1e-6)
            if kind == "q":
                y = y * (DN_DK ** -0.5)
            dst[n * CHUNK:(n + 1) * CHUNK, :] = y

    upper = jnp.where(row <= col, 1.0, 0.0).astype(F32)
    sub8 = lax.broadcasted_iota(I32, (SUBLANES, LANES), 0)
    for n in range(nc):
        mt = misc_ref[0, n * CHUNK:(n + 1) * CHUNK, :].T
        slab = mt[M_B:M_B + SUBLANES, :]
        beta8 = _sigmoid(slab)
        g8 = -jnp.exp(alog_ref[...]) * _softplus(slab + dtb_ref[...])
        gc8 = jnp.dot(g8, upper, preferred_element_type=F32, precision=lax.Precision.HIGHEST)
        b_row = jnp.sum(jnp.where(sub8 == h, beta8, 0.0), axis=0, keepdims=True)
        g_row = jnp.sum(jnp.where(sub8 == h + DN_HEADS, gc8, 0.0), axis=0, keepdims=True)
        brow_s[n] = jnp.broadcast_to(b_row, (SUBLANES, LANES))
        grow_s[n] = jnp.broadcast_to(g_row, (SUBLANES, LANES))

    eye = jnp.where(row == col, 1.0, 0.0).astype(F32)

    def prep_one(n):
        r0 = pl.multiple_of(n * CHUNK, CHUNK)
        q = cq_s[pl.ds(r0, CHUNK), :]
        k = ck_s[pl.ds(r0, CHUNK), :]
        v = cv_s[pl.ds(r0, CHUNK), :]
        g_r = jnp.broadcast_to(grow_s[n][0:1, :], (CHUNK, CHUNK))
        g_c = g_r.T
        b_c = jnp.broadcast_to(brow_s[n][0:1, :], (CHUNK, CHUNK)).T
        g_last = jnp.broadcast_to(g_c[CHUNK - 1:CHUNK, :], (CHUNK, CHUNK))
        decay = jnp.where(row >= col, jnp.exp(jnp.minimum(g_c - g_r, 0.0)), 0.0)
        kb = k * b_c
        kq = _mm_nt(jnp.concatenate([kb, q], axis=0), k)
        yield
        lmat = jnp.where(row > col, kq[:CHUNK] * decay, 0.0)
        a = jnp.where(row >= col, kq[CHUNK:] * decay, 0.0)
        same = lambda b: (row // b) == (col // b)
        l0 = jnp.where(same(DN_BASE), lmat, 0.0)
        x = eye - l0
        m = _mm(l0, l0)
        yield
        power = 2
        while 2 * power < DN_BASE:
            xm = _mm(jnp.concatenate([x, m], axis=0), m)
            yield
            x = x + xm[:CHUNK]
            m = xm[CHUNK:]
            power *= 2
        x = x + _mm(x, m)
        yield
        blk = DN_BASE
        while blk < CHUNK:
            below = jnp.where(same(2 * blk) & jnp.logical_not(same(blk)), lmat, 0.0)
            y = _mm(below, x)
            yield
            x = x - _mm(x, y)
            yield
            blk *= 2
        eg = jnp.exp(g_c)
        sol = _mm(x, jnp.concatenate([v * b_c, kb * eg], axis=1))
        yield
        cn = _mm((k * jnp.exp(g_last - g_c)).T, sol)
        yield
        u_s[n] = sol[:, :DN_DV]
        w_s[n] = sol[:, DN_DV:]
        a_s[n] = a
        qg_s[n] = q * eg
        c_s[n] = cn[:, :DN_DV]
        n_s[n] = cn[:, DN_DV:]
        el_s[n] = jnp.exp(g_last)[0:SUBLANES, :]

    def grouped(gen):
        def body(i, carry):
            live = [gen(i * DN_GROUP + j) for j in range(DN_GROUP)]
            while live:
                live = [g for g in live if next(g, True) is None]
            return carry
        lax.fori_loop(0, nc // DN_GROUP, body, 0)

    grouped(prep_one)

    def chain(n, state):
        st_s[n] = state
        el = jnp.broadcast_to(el_s[n][0:1, :], (CHUNK, CHUNK))
        return state * el + c_s[n] - _mm(n_s[n], state)

    lax.fori_loop(0, nc, chain, jnp.zeros((DN_DK, DN_DV), F32))

    def emit_one(n):
        r0 = pl.multiple_of(n * CHUNK, CHUNK)
        ws = _mm(jnp.concatenate([w_s[n], qg_s[n]], axis=0), st_s[n])
        yield
        o = ws[CHUNK:] + _mm(a_s[n], u_s[n] - ws[:CHUNK])
        yield
        z = z_ref[0, pl.ds(r0, CHUNK), :]
        o_ref[0, pl.ds(r0, CHUNK), :] = (_rms(o, ng_ref[...]) * _silu(z)).astype(o_ref.dtype)

    grouped(emit_one)


def _deltanet(proj, conv_w, alog8, dtb8, norm_g):
    bsz, seq, _ = proj.shape
    nc = seq // CHUNK
    qb, kb, vb, zb = (C_QKV // LANES, (C_QKV + DN_QK) // LANES, (C_QKV + 2 * DN_QK) // LANES,
                      C_Z // LANES)
    col = lambda off: pl.BlockSpec((1, seq, LANES), lambda b, h, off=off: (b, 0, off + h))
    cw = lambda off: pl.BlockSpec((DN_CONV, LANES), lambda b, h, off=off: (0, off + h))
    mat = lambda: pltpu.VMEM((nc, CHUNK, CHUNK), F32)
    return pl.pallas_call(
        _dn_kernel,
        grid=(bsz, DN_HEADS),
        in_specs=[col(qb), col(kb), col(vb), col(zb),
                  pl.BlockSpec((1, seq, LANES), lambda b, h: (b, 0, C_MISC // LANES)),
                  cw(qb), cw(kb), cw(vb),
                  _const_spec((SUBLANES, LANES)), _const_spec((SUBLANES, LANES)),
                  _const_spec((1, DN_DV))],
        out_specs=pl.BlockSpec((1, seq, LANES), lambda b, h: (b, 0, h)),
        out_shape=jax.ShapeDtypeStruct((bsz, seq, DN_V), BF16),
        scratch_shapes=[pltpu.VMEM((seq + SUBLANES, LANES), F32),
                        pltpu.VMEM((seq, LANES), F32), pltpu.VMEM((seq, LANES), F32),
                        pltpu.VMEM((seq, LANES), F32),
                        pltpu.VMEM((nc, SUBLANES, LANES), F32), pltpu.VMEM((nc, SUBLANES, LANES), F32),
                        mat(), mat(), mat(), mat(), mat(), mat(), mat(),
                        pltpu.VMEM((nc, SUBLANES, LANES), F32)],
        compiler_params=pltpu.CompilerParams(dimension_semantics=("arbitrary", "arbitrary"),
                                             vmem_limit_bytes=VMEM_LIMIT),
        name="deltanet",
    )(proj, proj, proj, proj, proj, conv_w, conv_w, conv_w, alog8, dtb8, norm_g)


def _even_odd(n):
    return list(range(0, n, 2)) + list(range(1, n, 2))


def _rope(x, c, s, s1, s2):
    w = x.shape[-1]
    return x * c + pltpu.roll(x, w - HALF, 1) * (s * s1) + pltpu.roll(x, HALF, 1) * (s * s2)


def _dsa_kernel(topk, q_ref, qi_ref, misct_ref, kv_ref, miscf_ref, pos_ref, qg_ref, kg_ref, rp_ref,
                hsum_ref, o_ref,
                cos_s, sin_s, ka_s, kb_s, vt_s, kia_s, kib_s, q2_s, qi2_s, sc_s, bias_s, lg_s, pa_s,
                pb_s, l8_s, acc_s):
    seq = kv_ref.shape[1]
    nc = seq // CHUNK
    qb = pl.program_id(1)
    npair = qb + 1
    r0 = pl.multiple_of(qb * QBLK, QBLK)
    col = lax.broadcasted_iota(I32, (CHUNK, CHUNK), 1)
    freq = rp_ref[0:1, :]
    s1 = rp_ref[1:2, :]
    s2 = rp_ref[2:3, :]

    @pl.when(qb == 0)
    def _():
        for n in range(nc):
            rows = slice(n * CHUNK, (n + 1) * CHUNK)
            ang = pos_ref[0, rows, :].astype(F32) * freq
            c = jnp.cos(ang)
            s = jnp.sin(ang)
            cos_s[rows, :] = c
            sin_s[rows, :] = s
            ck = jnp.where(col < ROT, c, 1.0)
            sk = jnp.where(col < ROT, s, 0.0)
            kv = kv_ref[0, rows, :]
            ss = jnp.sum(jnp.where(col < SA_HEAD_DIM, kv * kv, 0.0), axis=-1, keepdims=True)
            r = lax.rsqrt(ss * (1.0 / SA_HEAD_DIM) + RMS_EPS)
            y = jnp.where(col < SA_HEAD_DIM, kv * r * kg_ref[...], kv)
            y = _rope(y, ck, sk, s1, s2)
            ka = jnp.where(col < SA_HEAD_DIM, y, 0.0)
            ka_s[rows, :] = ka.astype(BF16)
            kb_s[rows, :] = pltpu.roll(ka, SA_HEAD_DIM, 1).astype(BF16)
            half = (n % 2) * CHUNK
            vt_s[n // 2, :, half:half + CHUNK] = y.T[SA_HEAD_DIM:, :].astype(BF16)
            yi = _rope(miscf_ref[0, rows, :], ck, sk, s1, s2)
            kia = jnp.where(col < IDX_DIM, yi, 0.0)
            kia_s[rows, :] = kia.astype(BF16)
            kib_s[rows, :] = pltpu.roll(kia, IDX_DIM, 1).astype(BF16)

    c1 = cos_s[pl.ds(r0, QBLK), :]
    sn1 = sin_s[pl.ds(r0, QBLK), :]
    rep = lambda a, k: jnp.concatenate([a] * k, axis=1)
    nq = SA_HEADS * SA_HEAD_DIM // LANES
    xq = q_ref[0]
    yq = _rope(xq * qg_ref[...], rep(c1, nq), rep(sn1, nq), rep(s1, nq), rep(s2, nq))
    ss = jnp.dot((xq * xq).astype(BF16), hsum_ref[...], preferred_element_type=F32)
    r = lax.rsqrt(ss * (1.0 / SA_HEAD_DIM) + RMS_EPS)
    qn = (yq * (r * (SA_HEAD_DIM ** -0.5))).astype(BF16)
    for pr in range(nq):
        q2_s[pr * QBLK:(pr + 1) * QBLK, :] = qn[:, pr * LANES:(pr + 1) * LANES]
    ni = IDX_HEADS * IDX_DIM // LANES
    yi = _rope(qi_ref[0], rep(c1, ni), rep(sn1, ni), rep(s1, ni), rep(s2, ni)).astype(BF16)
    for pr in range(ni):
        qi2_s[pr * QBLK:(pr + 1) * QBLK, :] = yi[:, pr * LANES:(pr + 1) * LANES]
    mt = misct_ref[0].T
    wscale = (IDX_HEADS ** -0.5) * (IDX_DIM ** -0.5)
    wrow = jnp.concatenate([mt[M_W + hh:M_W + hh + 1, :] for hh in _even_odd(IDX_HEADS)],
                           axis=1) * wscale

    krow = lax.broadcasted_iota(I32, (KPAIR, QBLK), 0)
    qpos = r0 + lax.broadcasted_iota(I32, (KPAIR, QBLK), 1)
    big = -NEG_INF

    def fold8(x, op):
        parts = [x[i:i + SUBLANES] for i in range(0, x.shape[0], SUBLANES)]
        while len(parts) > 1:
            parts = [op(parts[i], parts[i + 1]) for i in range(0, len(parts), 2)]
        return parts[0]

    def idx_body(c, carry):
        k0 = pl.multiple_of(c * KPAIR, KPAIR)
        terms = []
        for keys_s in (kia_s, kib_s):
            keys = keys_s[pl.ds(k0, KPAIR), :]
            for pr in range(IDX_HEADS // 2):
                lt = _mm_nt(keys, qi2_s[pr * QBLK:(pr + 1) * QBLK, :])
                j = len(terms)
                terms.append(jnp.maximum(lt, 0.0) * wrow[:, j * QBLK:(j + 1) * QBLK])
        acc = (terms[0] + terms[1]) + (terms[2] + terms[3])
        causal = k0 + krow <= qpos
        sc = jnp.where(causal, acc, NEG_INF)
        sc_s[c] = sc
        lo8, hi8 = carry
        lo8 = jnp.minimum(lo8, fold8(jnp.where(causal, acc, big), jnp.minimum))
        hi8 = jnp.maximum(hi8, fold8(sc, jnp.maximum))
        return lo8, hi8

    lo8, hi8 = lax.fori_loop(0, npair, idx_body, (jnp.full((SUBLANES, QBLK), big, F32),
                                                  jnp.full((SUBLANES, QBLK), NEG_INF, F32)))
    lo = jnp.min(lo8, axis=0, keepdims=True)
    hi = jnp.max(hi8, axis=0, keepdims=True)

    def count(pred):
        def body(c, acc):
            return acc + fold8(jnp.where(pred(sc_s[c]), 1.0, 0.0), jnp.add)
        acc = lax.fori_loop(0, npair, body, jnp.zeros((SUBLANES, QBLK), F32))
        return jnp.sum(acc, axis=0, keepdims=True)

    def max_where(pred):
        def body(c, acc):
            s = sc_s[c]
            return jnp.maximum(acc, fold8(jnp.where(pred(s), s, NEG_INF), jnp.maximum))
        acc = lax.fori_loop(0, npair, body, jnp.full((SUBLANES, QBLK), NEG_INF, F32))
        return jnp.max(acc, axis=0, keepdims=True)

    kf = jnp.minimum(float(topk), (qpos[0:1, :] + 1).astype(F32))

    def bisect(_, bounds):
        lo, hi = bounds
        mid = lo + 0.5 * (hi - lo)
        above = count(lambda s: s > mid) >= kf
        return jnp.where(above, mid, lo), jnp.where(above, hi, mid)

    lo, hi = lax.fori_loop(0, BISECT_STEPS, bisect, (lo, hi))

    thr = max_where(lambda s: s <= hi)
    cge = count(lambda s: s >= thr)

    def short(cge):
        return jnp.max(jnp.where(cge < kf, 1.0, 0.0)) > 0.5

    def peel(state):
        thr, cge, _ = state
        nxt = max_where(lambda s: s < thr)
        cnx = count(lambda s: s >= nxt)
        step = cge < kf
        thr = jnp.where(step, nxt, thr)
        cge = jnp.where(step, cnx, cge)
        return thr, cge, short(cge)

    thr, cge, _ = lax.while_loop(lambda st: st[2], peel, (thr, cge, short(cge)))
    need = kf - count(lambda s: s > thr)

    lower = jnp.where(lax.broadcasted_iota(I32, (KPAIR, KPAIR), 0)
                      >= lax.broadcasted_iota(I32, (KPAIR, KPAIR), 1), 1.0, 0.0).astype(BF16)

    tile_heads = lambda b: jnp.concatenate([b] * SA_HEADS, axis=1)

    def sel_body(c, carry):
        off, m8 = carry
        k0 = pl.multiple_of(c * KPAIR, KPAIR)
        s = sc_s[c]
        eqf = jnp.where(s == thr, 1.0, 0.0)
        pref = jnp.dot(lower, eqf.astype(BF16), preferred_element_type=F32)
        tie = jnp.where(pref + off <= need, eqf, 0.0)
        keep = jnp.where(k0 + krow <= qpos, jnp.where(s > thr, 1.0, tie), 0.0)
        bias = jnp.where(keep > 0.5, 0.0, NEG_INF)
        bias_s[c] = bias
        tops = []
        for keys_s in (ka_s, kb_s):
            keys = keys_s[pl.ds(k0, KPAIR), :]
            for pr in range(SA_HEADS // 2):
                lg = _mm_nt(keys, q2_s[pr * QBLK:(pr + 1) * QBLK, :])
                lg_s[c, :, len(tops) * QBLK:(len(tops) + 1) * QBLK] = lg
                tops.append(fold8(lg + bias, jnp.maximum))
        m8 = jnp.maximum(m8, jnp.concatenate(tops, axis=1))
        return off + pref[KPAIR - 1:KPAIR, :], m8

    _, m8 = lax.fori_loop(0, npair, sel_body,
                          (jnp.zeros((1, QBLK), F32),
                           jnp.full((SUBLANES, SA_HEADS * QBLK), NEG_INF, F32)))
    mrow = jnp.max(m8, axis=0, keepdims=True)

    def probs(c, dst):
        slab = 2 * SUBLANES
        l8 = l8_s[...]
        for r in range(0, KPAIR, slab):
            p = jnp.exp(lg_s[c, r:r + slab, :] + tile_heads(bias_s[c, r:r + slab, :]) - mrow)
            dst[r:r + slab, :] = p.astype(BF16)
            l8 = l8 + (p[:SUBLANES] + p[SUBLANES:])
        l8_s[...] = l8

    def pv_acc(c, src):
        acc_s[...] += jnp.dot(vt_s[c], src[...], preferred_element_type=F32)

    acc_s[...] = jnp.zeros(acc_s.shape, F32)
    l8_s[...] = jnp.zeros(l8_s.shape, F32)
    probs(0, pa_s)

    def p2(i, carry):
        c = 2 * i + 1
        pv_acc(c - 1, pa_s)
        probs(c, pb_s)

        @pl.when(c + 1 < npair)
        def _():
            pv_acc(c, pb_s)
            probs(c + 1, pa_s)
        return carry

    lax.fori_loop(0, npair // 2, p2, 0)
    last = npair - 1

    @pl.when(last % 2 == 0)
    def _():
        pv_acc(last, pa_s)

    @pl.when(last % 2 == 1)
    def _():
        pv_acc(last, pb_s)

    out_t = acc_s[...] / jnp.sum(l8_s[...], axis=0, keepdims=True)
    order = _even_odd(SA_HEADS)
    for pr in range(SA_HEADS // 2):
        ev, od = order.index(2 * pr), order.index(2 * pr + 1)
        two = jnp.concatenate([out_t[:, ev * QBLK:(ev + 1) * QBLK],
                               out_t[:, od * QBLK:(od + 1) * QBLK]], axis=0)
        o_ref[0, :, pr * LANES:(pr + 1) * LANES] = two.T.astype(o_ref.dtype)


def _dsa(proj, pos3, qg_t, kg_p, rope_pat, topk):
    bsz, seq, _ = proj.shape
    nc = seq // CHUNK
    qw = SA_HEADS * SA_HEAD_DIM
    iw = IDX_HEADS * IDX_DIM
    assert seq % QBLK == 0 and QBLK == KPAIR
    npairs = seq // KPAIR
    pairs = lambda: pltpu.VMEM((npairs, KPAIR, QBLK), F32)
    head_of = jnp.arange(qw) // SA_HEAD_DIM
    head_sum = (head_of[:, None] == head_of[None, :]).astype(BF16)
    return pl.pallas_call(
        functools.partial(_dsa_kernel, topk),
        grid=(bsz, seq // QBLK),
        in_specs=[pl.BlockSpec((1, QBLK, qw), lambda b, j: (b, j, C_SAQ // qw)),
                  pl.BlockSpec((1, QBLK, iw), lambda b, j: (b, j, C_IXQ // iw)),
                  pl.BlockSpec((1, QBLK, LANES), lambda b, j: (b, j, C_MISC // LANES)),
                  pl.BlockSpec((1, seq, LANES), lambda b, j: (b, 0, C_KV // LANES)),
                  pl.BlockSpec((1, seq, LANES), lambda b, j: (b, 0, C_MISC // LANES)),
                  pl.BlockSpec((1, seq, 1), lambda b, j: (b, 0, 0)),
                  _const_spec((1, qw)), _const_spec((1, LANES)), _const_spec((SUBLANES, LANES)),
                  _const_spec((qw, qw))],
        out_specs=pl.BlockSpec((1, QBLK, qw), lambda b, j: (b, j, 0)),
        out_shape=jax.ShapeDtypeStruct((bsz, seq, qw), BF16),
        scratch_shapes=[pltpu.VMEM((seq, LANES), F32), pltpu.VMEM((seq, LANES), F32),
                        pltpu.VMEM((seq, LANES), BF16), pltpu.VMEM((seq, LANES), BF16),
                        pltpu.VMEM((npairs, SA_HEAD_DIM, KPAIR), BF16),
                        pltpu.VMEM((seq, LANES), BF16), pltpu.VMEM((seq, LANES), BF16),
                        pltpu.VMEM((qw // LANES * QBLK, LANES), BF16),
                        pltpu.VMEM((iw // LANES * QBLK, LANES), BF16),
                        pairs(), pairs(),
                        pltpu.VMEM((npairs, KPAIR, SA_HEADS * QBLK), F32),
                        pltpu.VMEM((KPAIR, SA_HEADS * QBLK), BF16),
                        pltpu.VMEM((KPAIR, SA_HEADS * QBLK), BF16),
                        pltpu.VMEM((SUBLANES, SA_HEADS * QBLK), F32),
                        pltpu.VMEM((SA_HEAD_DIM, SA_HEADS * QBLK), F32)],
        compiler_params=pltpu.CompilerParams(dimension_semantics=("arbitrary", "arbitrary"),
                                             vmem_limit_bytes=VMEM_LIMIT),
        name="dsa",
    )(proj, proj, proj, proj, proj, pos3, qg_t, kg_p, rope_pat, head_sum)


def _merge_kernel(x_ref, oa_ref, ob_ref, g_ref, wg_ref, bg_ref, wa_ref, wb_ref, wo_ref, o_ref):
    x = x_ref[...]
    hb = _rms(x, g_ref[...]).astype(BF16)
    oa = oa_ref[...]
    ob = ob_ref[...]
    acc = x
    step = 512
    for n in range(0, D_MODEL, step):
        ga = _sigmoid(jnp.dot(hb, wg_ref[:, n:n + step], preferred_element_type=F32)
                      + bg_ref[:, n:n + step])
        gb = _sigmoid(jnp.dot(hb, wg_ref[:, D_MODEL + n:D_MODEL + n + step],
                              preferred_element_type=F32)
                      + bg_ref[:, D_MODEL + n:D_MODEL + n + step])
        ya = jnp.dot(oa, wa_ref[:, n:n + step], preferred_element_type=F32)
        yb = jnp.dot(ob, wb_ref[:, n:n + step], preferred_element_type=F32)
        merged = (ga * ya + gb * yb).astype(BF16)
        acc = acc + jnp.dot(merged, wo_ref[n:n + step, :], preferred_element_type=F32)
    o_ref[...] = acc


def _merge(x2d, oa2d, ob2d, gain, w_gate, b_gate, w_a, w_b, w_o, tm):
    t = x2d.shape[0]
    tok = lambda w: pl.BlockSpec((tm, w), lambda i: (i, 0))
    return pl.pallas_call(
        _merge_kernel,
        grid=(t // tm,),
        in_specs=[tok(D_MODEL), tok(DN_V), tok(SA_HEADS * SA_HEAD_DIM),
                  _const_spec((1, D_MODEL)), _const_spec((D_MODEL, 2 * D_MODEL)),
                  _const_spec((1, 2 * D_MODEL)), _const_spec((DN_V, D_MODEL)),
                  _const_spec((SA_HEADS * SA_HEAD_DIM, D_MODEL)), _const_spec((D_MODEL, D_MODEL))],
        out_specs=tok(D_MODEL),
        out_shape=jax.ShapeDtypeStruct((t, D_MODEL), F32),
        compiler_params=pltpu.CompilerParams(dimension_semantics=("arbitrary",),
                                             vmem_limit_bytes=VMEM_LIMIT),
        name="merge",
    )(x2d, oa2d, ob2d, gain, w_gate, b_gate, w_a, w_b, w_o)


def _ffn_kernel(x_ref, p_ref, g2_ref, wg_ref, wu_ref, wd_ref, g3_ref, wpg_ref, wp_ref, o_ref):
    x = x_ref[...]
    hb = _rms(x, g2_ref[...]).astype(BF16)
    acc = x
    step = 256
    for n in range(0, D_FF, step):
        gate = jnp.dot(hb, wg_ref[:, n:n + step], preferred_element_type=F32)
        up = jnp.dot(hb, wu_ref[:, n:n + step], preferred_element_type=F32)
        acc = acc + jnp.dot((_silu(gate) * up).astype(BF16), wd_ref[n:n + step, :],
                            preferred_element_type=F32)
    h3 = _rms(acc, g3_ref[...]).astype(BF16)
    pg = _sigmoid(jnp.dot(h3, wpg_ref[...], preferred_element_type=F32))
    pe = jnp.dot(p_ref[...].astype(BF16), wp_ref[...], preferred_element_type=F32)
    o_ref[...] = acc + pg * pe


def _ffn(x2d, p2d, g2, w_g, w_u, w_d, g3, w_pg, w_p, tm):
    t = x2d.shape[0]
    tok = lambda w: pl.BlockSpec((tm, w), lambda i: (i, 0))
    return pl.pallas_call(
        _ffn_kernel,
        grid=(t // tm,),
        in_specs=[tok(D_MODEL), tok(PLE_DIM), _const_spec((1, D_MODEL)),
                  _const_spec((D_MODEL, D_FF)), _const_spec((D_MODEL, D_FF)),
                  _const_spec((D_FF, D_MODEL)), _const_spec((1, D_MODEL)),
                  _const_spec((D_MODEL, D_MODEL)), _const_spec((PLE_DIM, D_MODEL))],
        out_specs=tok(D_MODEL),
        out_shape=jax.ShapeDtypeStruct((t, D_MODEL), F32),
        compiler_params=pltpu.CompilerParams(dimension_semantics=("arbitrary",),
                                             vmem_limit_bytes=VMEM_LIMIT),
        name="ffn_ple",
    )(x2d, p2d, g2, w_g, w_u, w_d, g3, w_pg, w_p)


def _regroup_w_in(w):
    sizes = (DN_CONV_CH, DN_V, DN_HEADS, DN_HEADS, SA_HEADS * SA_HEAD_DIM, SA_HEAD_DIM, SA_HEAD_DIM,
             IDX_HEADS * IDX_DIM, IDX_DIM, IDX_HEADS)
    parts, off = [], 0
    for s in sizes:
        parts.append(w[:, off:off + s])
        off += s
    qkv, z, b, a, saq, sak, sav, ixq, ixk, ixw = parts
    cols = [qkv, z, saq, ixq, sak, sav, ixk, b, a, ixw]
    used = sum(c.shape[1] for c in cols)
    cols.append(jnp.zeros((w.shape[0], PROJ_W - used), w.dtype))
    return jnp.concatenate(cols, axis=1)


def _rope_pattern():
    lane = jnp.arange(LANES) % SA_HEAD_DIM
    inv_freq = ROPE_THETA ** (-jnp.arange(0, ROT, 2, dtype=F32) / ROT)
    freq = jnp.where(lane < ROT, inv_freq[lane % HALF], 0.0)
    s1 = jnp.where(lane < HALF, -1.0, 0.0)
    s2 = jnp.where((lane >= HALF) & (lane < ROT), 1.0, 0.0)
    pat = jnp.zeros((SUBLANES, LANES), F32)
    return pat.at[0].set(freq).at[1].set(s1).at[2].set(s2)


def _layer(x, p, positions, attn_norm, w_in, conv_w, a_log, dt_bias, dn_norm, q_norm, k_norm,
           w_branch_a, w_branch_b, w_gate, b_gate, w_o, ffn_norm, w_ffn_gate, w_ffn_up, w_ffn_down,
           ple_norm, w_ple_gate, w_ple):
    bsz, seq, _ = x.shape
    t = bsz * seq
    tm = 1024
    topk = min(IDX_TOPK_MAX, seq // 4)
    row = lambda v: v.reshape(1, -1).astype(F32)
    x2d = x.reshape(t, D_MODEL)

    proj = _in_proj(x2d, row(attn_norm), _regroup_w_in(w_in).astype(BF16), tm)
    proj = proj.reshape(bsz, seq, PROJ_W)

    pad4 = jnp.zeros((DN_HEADS,), F32)
    per_head = lambda v: jnp.broadcast_to(jnp.concatenate([pad4, v.astype(F32)])[:, None],
                                          (SUBLANES, LANES))
    o_a = _deltanet(proj, conv_w.astype(F32), per_head(a_log), per_head(dt_bias), row(dn_norm))

    qg_t = jnp.tile(q_norm.astype(F32), SA_HEADS).reshape(1, -1)
    kg_p = jnp.concatenate([k_norm.astype(F32), jnp.ones((LANES - SA_HEAD_DIM,), F32)]).reshape(1, -1)
    pos3 = positions.astype(I32).reshape(bsz, seq, 1)
    o_b = _dsa(proj, pos3, qg_t, kg_p, _rope_pattern(), topk)

    x1 = _merge(x2d, o_a.reshape(t, DN_V), o_b.reshape(t, -1), row(attn_norm),
                w_gate.astype(BF16), row(b_gate), w_branch_a.astype(BF16), w_branch_b.astype(BF16),
                w_o.astype(BF16), tm)
    x2 = _ffn(x1, p.reshape(t, PLE_DIM), row(ffn_norm), w_ffn_gate.astype(BF16),
              w_ffn_up.astype(BF16), w_ffn_down.astype(BF16), row(ple_norm),
              w_ple_gate.astype(BF16), w_ple.astype(BF16), tm)
    return x2.reshape(bsz, seq, D_MODEL)


def kernel(x, p, positions, attn_norm, w_in, conv_w, a_log, dt_bias, dn_norm, q_norm, k_norm,
           w_branch_a, w_branch_b, w_gate, b_gate, w_o, ffn_norm, w_ffn_gate, w_ffn_up, w_ffn_down,
           ple_norm, w_ple_gate, w_ple):
    depth = w_in.shape[0]
    for i in range(depth):
        x = _layer(x, p[i], positions, attn_norm[i], w_in[i], conv_w[i], a_log[i], dt_bias[i],
                   dn_norm[i], q_norm[i], k_norm[i], w_branch_a[i], w_branch_b[i], w_gate[i],
                   b_gate[i], w_o[i], ffn_norm[i], w_ffn_gate[i], w_ffn_up[i], w_ffn_down[i],
                   ple_norm[i], w_ple_gate[i], w_ple[i])
    return x
```

```python
import functools

import jax
import jax.numpy as jnp
from jax import lax
from jax.experimental import pallas as pl
from jax.experimental.pallas import tpu as pltpu

F32 = jnp.float32
BF16 = jnp.bfloat16
I32 = jnp.int32

D_MODEL = 1024
PLE_DIM = 256
RMS_EPS = 1e-6
DN_HEADS = 4
DN_DK = 128
DN_DV = 128
DN_CONV = 4
DN_QK = DN_HEADS * DN_DK
DN_V = DN_HEADS * DN_DV
DN_CONV_CH = 2 * DN_QK + DN_V
SA_HEADS = 8
SA_HEAD_DIM = 64
IDX_HEADS = 4
IDX_DIM = 64
IDX_TOPK_MAX = 256
ROPE_THETA = 500000.0
ROPE_FRACTION = 4
NEG_INF = -1e30
D_FF = 2816

LANES = 128
SUBLANES = 8
CHUNK = 128
DN_GROUP = 8
DN_BASE = 16
KPAIR = 2 * CHUNK
QBLK = KPAIR
BISECT_STEPS = 19

C_QKV = 0
C_Z = DN_CONV_CH
C_SAQ = C_Z + DN_V
C_IXQ = C_SAQ + SA_HEADS * SA_HEAD_DIM
C_KV = C_IXQ + IDX_HEADS * IDX_DIM
C_MISC = C_KV + 2 * SA_HEAD_DIM
PROJ_W = C_MISC + LANES
M_IXK = 0
M_B = IDX_DIM
M_A = M_B + DN_HEADS
M_W = M_A + DN_HEADS

ROT = SA_HEAD_DIM // ROPE_FRACTION
HALF = ROT // 2

VMEM_LIMIT = 56 * 1024 * 1024


def _const_spec(shape):
    nd = len(shape)
    return pl.BlockSpec(shape, lambda *_: (0,) * nd, pipeline_mode=pl.Buffered(1))


def _rms(x, g):
    return x * lax.rsqrt(jnp.mean(x * x, axis=-1, keepdims=True) + RMS_EPS) * g


def _mm(a, b):
    return jnp.dot(a.astype(BF16), b.astype(BF16), preferred_element_type=F32)


def _mm_nt(a, b):
    return lax.dot_general(a.astype(BF16), b.astype(BF16), (((1,), (1,)), ((), ())),
                           preferred_element_type=F32)


def _sigmoid(x):
    return 1.0 / (1.0 + jnp.exp(-x))


def _silu(x):
    return x * _sigmoid(x)


def _softplus(x):
    return jnp.maximum(x, 0.0) + jnp.log1p(jnp.exp(-jnp.abs(x)))


def _in_proj_kernel(x_ref, g_ref, w_ref, o_ref):
    hb = _rms(x_ref[...], g_ref[...]).astype(BF16)
    step = 512
    for n in range(0, PROJ_W, step):
        o_ref[:, n:n + step] = jnp.dot(hb, w_ref[:, n:n + step], preferred_element_type=F32)


def _in_proj(x2d, gain, w_perm, tm):
    t = x2d.shape[0]
    return pl.pallas_call(
        _in_proj_kernel,
        grid=(t // tm,),
        in_specs=[pl.BlockSpec((tm, D_MODEL), lambda i: (i, 0)),
                  _const_spec((1, D_MODEL)),
                  _const_spec((D_MODEL, PROJ_W))],
        out_specs=pl.BlockSpec((tm, PROJ_W), lambda i: (i, 0)),
        out_shape=jax.ShapeDtypeStruct((t, PROJ_W), F32),
        compiler_params=pltpu.CompilerParams(dimension_semantics=("arbitrary",),
                                             vmem_limit_bytes=VMEM_LIMIT),
        name="in_proj",
    )(x2d, gain, w_perm)


def _dn_kernel(q_ref, k_ref, v_ref, z_ref, misc_ref, cwq_ref, cwk_ref, cwv_ref, alog_ref, dtb_ref,
               ng_ref, o_ref,
               pad_s, cq_s, ck_s, cv_s, brow_s, grow_s, u_s, w_s, a_s, qg_s, c_s, n_s, st_s, el_s):
    seq = q_ref.shape[1]
    nc = seq // CHUNK
    h = pl.program_id(1)
    row = lax.broadcasted_iota(I32, (CHUNK, CHUNK), 0)
    col = lax.broadcasted_iota(I32, (CHUNK, CHUNK), 1)

    pad_s[0:SUBLANES, :] = jnp.zeros((SUBLANES, LANES), F32)
    for src, cw_ref, dst, kind in ((q_ref, cwq_ref, cq_s, "q"), (k_ref, cwk_ref, ck_s, "k"),
                                   (v_ref, cwv_ref, cv_s, "v")):
        pad_s[SUBLANES:seq + SUBLANES, :] = src[0]
        for n in range(nc):
            base = n * CHUNK + SUBLANES
            y = cw_ref[DN_CONV - 1:DN_CONV, :] * pad_s[base:base + CHUNK, :]
            for j in range(1, DN_CONV):
                y = y + (cw_ref[DN_CONV - 1 - j:DN_CONV - j, :]
                         * pad_s[base - j:base - j + CHUNK, :])
            y = _silu(y)
            if kind != "v":
                y = y * lax.rsqrt(jnp.sum(y * y, axis=-1, keepdims=True) + 1e-6)
            if kind == "q":
                y = y * (DN_DK ** -0.5)
            dst[n * CHUNK:(n + 1) * CHUNK, :] = y

    upper = jnp.where(row <= col, 1.0, 0.0).astype(F32)
    sub8 = lax.broadcasted_iota(I32, (SUBLANES, LANES), 0)
    for n in range(nc):
        mt = misc_ref[0, n * CHUNK:(n + 1) * CHUNK, :].T
        slab = mt[M_B:M_B + SUBLANES, :]
        beta8 = _sigmoid(slab)
        g8 = -jnp.exp(alog_ref[...]) * _softplus(slab + dtb_ref[...])
        gc8 = jnp.dot(g8, upper, preferred_element_type=F32, precision=lax.Precision.HIGHEST)
        b_row = jnp.sum(jnp.where(sub8 == h, beta8, 0.0), axis=0, keepdims=True)
        g_row = jnp.sum(jnp.where(sub8 == h + DN_HEADS, gc8, 0.0), axis=0, keepdims=True)
        brow_s[n] = jnp.broadcast_to(b_row, (SUBLANES, LANES))
        grow_s[n] = jnp.broadcast_to(g_row, (SUBLANES, LANES))

    eye = jnp.where(row == col, 1.0, 0.0).astype(F32)

    def prep_one(n):
        r0 = pl.multiple_of(n * CHUNK, CHUNK)
        q = cq_s[pl.ds(r0, CHUNK), :]
        k = ck_s[pl.ds(r0, CHUNK), :]
        v = cv_s[pl.ds(r0, CHUNK), :]
        g_r = jnp.broadcast_to(grow_s[n][0:1, :], (CHUNK, CHUNK))
        g_c = g_r.T
        b_c = jnp.broadcast_to(brow_s[n][0:1, :], (CHUNK, CHUNK)).T
        g_last = jnp.broadcast_to(g_c[CHUNK - 1:CHUNK, :], (CHUNK, CHUNK))
        decay = jnp.where(row >= col, jnp.exp(jnp.minimum(g_c - g_r, 0.0)), 0.0)
        kb = k * b_c
        kq = _mm_nt(jnp.concatenate([kb, q], axis=0), k)
        yield
        lmat = jnp.where(row > col, kq[:CHUNK] * decay, 0.0)
        a = jnp.where(row >= col, kq[CHUNK:] * decay, 0.0)
        same = lambda b: (row // b) == (col // b)
        l0 = jnp.where(same(DN_BASE), lmat, 0.0)
        x = eye - l0
        m = _mm(l0, l0)
        yield
        power = 2
        while 2 * power < DN_BASE:
            xm = _mm(jnp.concatenate([x, m], axis=0), m)
            yield
            x = x + xm[:CHUNK]
            m = xm[CHUNK:]
            power *= 2
        x = x + _mm(x, m)
        yield
        blk = DN_BASE
        while blk < CHUNK:
            below = jnp.where(same(2 * blk) & jnp.logical_not(same(blk)), lmat, 0.0)
            y = _mm(below, x)
            yield
            x = x - _mm(x, y)
            yield
            blk *= 2
        eg = jnp.exp(g_c)
        sol = _mm(x, jnp.concatenate([v * b_c, kb * eg], axis=1))
        yield
        cn = _mm((k * jnp.exp(g_last - g_c)).T, sol)
        yield
        u_s[n] = sol[:, :DN_DV]
        w_s[n] = sol[:, DN_DV:]
        a_s[n] = a
        qg_s[n] = q * eg
        c_s[n] = cn[:, :DN_DV]
        n_s[n] = cn[:, DN_DV:]
        el_s[n] = jnp.exp(g_last)[0:SUBLANES, :]

    def grouped(gen):
        def body(i, carry):
            live = [gen(i * DN_GROUP + j) for j in range(DN_GROUP)]
            while live:
                live = [g for g in live if next(g, True) is None]
            return carry
        lax.fori_loop(0, nc // DN_GROUP, body, 0)

    grouped(prep_one)

    def chain(n, state):
        st_s[n] = state
        el = jnp.broadcast_to(el_s[n][0:1, :], (CHUNK, CHUNK))
        return state * el + c_s[n] - _mm(n_s[n], state)

    lax.fori_loop(0, nc, chain, jnp.zeros((DN_DK, DN_DV), F32))

    def emit_one(n):
        r0 = pl.multiple_of(n * CHUNK, CHUNK)
        ws = _mm(jnp.concatenate([w_s[n], qg_s[n]], axis=0), st_s[n])
        yield
        o = ws[CHUNK:] + _mm(a_s[n], u_s[n] - ws[:CHUNK])
        yield
        z = z_ref[0, pl.ds(r0, CHUNK), :]
        o_ref[0, pl.ds(r0, CHUNK), :] = (_rms(o, ng_ref[...]) * _silu(z)).astype(o_ref.dtype)

    grouped(emit_one)


def _deltanet(proj, conv_w, alog8, dtb8, norm_g):
    bsz, seq, _ = proj.shape
    nc = seq // CHUNK
    qb, kb, vb, zb = (C_QKV // LANES, (C_QKV + DN_QK) // LANES, (C_QKV + 2 * DN_QK) // LANES,
                      C_Z // LANES)
    col = lambda off: pl.BlockSpec((1, seq, LANES), lambda b, h, off=off: (b, 0, off + h))
    cw = lambda off: pl.BlockSpec((DN_CONV, LANES), lambda b, h, off=off: (0, off + h))
    mat = lambda: pltpu.VMEM((nc, CHUNK, CHUNK), F32)
    return pl.pallas_call(
        _dn_kernel,
        grid=(bsz, DN_HEADS),
        in_specs=[col(qb), col(kb), col(vb), col(zb),
                  pl.BlockSpec((1, seq, LANES), lambda b, h: (b, 0, C_MISC // LANES)),
                  cw(qb), cw(kb), cw(vb),
                  _const_spec((SUBLANES, LANES)), _const_spec((SUBLANES, LANES)),
                  _const_spec((1, DN_DV))],
        out_specs=pl.BlockSpec((1, seq, LANES), lambda b, h: (b, 0, h)),
        out_shape=jax.ShapeDtypeStruct((bsz, seq, DN_V), BF16),
        scratch_shapes=[pltpu.VMEM((seq + SUBLANES, LANES), F32),
                        pltpu.VMEM((seq, LANES), F32), pltpu.VMEM((seq, LANES), F32),
                        pltpu.VMEM((seq, LANES), F32),
                        pltpu.VMEM((nc, SUBLANES, LANES), F32), pltpu.VMEM((nc, SUBLANES, LANES), F32),
                        mat(), mat(), mat(), mat(), mat(), mat(), mat(),
                        pltpu.VMEM((nc, SUBLANES, LANES), F32)],
        compiler_params=pltpu.CompilerParams(dimension_semantics=("arbitrary", "arbitrary"),
                                             vmem_limit_bytes=VMEM_LIMIT),
        name="deltanet",
    )(proj, proj, proj, proj, proj, conv_w, conv_w, conv_w, alog8, dtb8, norm_g)


def _even_odd(n):
    return list(range(0, n, 2)) + list(range(1, n, 2))


def _rope(x, c, s, s1, s2):
    w = x.shape[-1]
    return x * c + pltpu.roll(x, w - HALF, 1) * (s * s1) + pltpu.roll(x, HALF, 1) * (s * s2)


def _dsa_kernel(topk, q_ref, qi_ref, misct_ref, kv_ref, miscf_ref, pos_ref, qg_ref, kg_ref, rp_ref,
                hsum_ref, o_ref,
                cos_s, sin_s, ka_s, kb_s, vt_s, kia_s, kib_s, q2_s, qi2_s, sc_s, bias_s, lg_s, pa_s,
                pb_s, l8_s, acc_s):
    seq = kv_ref.shape[1]
    nc = seq // CHUNK
    qb = pl.program_id(1)
    npair = qb + 1
    r0 = pl.multiple_of(qb * QBLK, QBLK)
    col = lax.broadcasted_iota(I32, (CHUNK, CHUNK), 1)
    s1 = rp_ref[1:2, :]
    s2 = rp_ref[2:3, :]

    @pl.when(qb == 0)
    def _():
        ang = pos_ref[0].astype(F32) * rp_ref[0:1, :]
        packed = (jnp.cos(ang), jnp.sin(ang))
        lane = lax.broadcasted_iota(I32, ang.shape, 1)
        second = (lane >= SA_HEAD_DIM) & (lane < SA_HEAD_DIM + ROT)
        per_row = LANES // ROT
        for j in range(per_row):
            for tab, dst, rest in zip(packed, (cos_s, sin_s), (1.0, 0.0)):
                mine = tab if j == 0 else pltpu.roll(tab, LANES - j * ROT, 1)
                full = jnp.where(lane < ROT, mine,
                                 jnp.where(second, pltpu.roll(mine, SA_HEAD_DIM, 1), rest))
                dst[pl.ds(j, seq // per_row, stride=per_row), :] = full
        for n in range(nc):
            rows = slice(n * CHUNK, (n + 1) * CHUNK)
            c = cos_s[rows, :]
            s = sin_s[rows, :]
            ck = jnp.where(col < ROT, c, 1.0)
            sk = jnp.where(col < ROT, s, 0.0)
            kv = kv_ref[0, rows, :]
            ss = jnp.dot((kv * kv).astype(BF16), hsum_ref[0:LANES, 0:LANES],
                         preferred_element_type=F32)
            r = lax.rsqrt(ss * (1.0 / SA_HEAD_DIM) + RMS_EPS)
            y = jnp.where(col < SA_HEAD_DIM, kv * r * kg_ref[...], kv)
            y = _rope(y, ck, sk, s1, s2)
            ka = jnp.where(col < SA_HEAD_DIM, y, 0.0)
            ka_s[rows, :] = ka.astype(BF16)
            kb_s[rows, :] = pltpu.roll(ka, SA_HEAD_DIM, 1).astype(BF16)
            half = (n % 2) * CHUNK
            vt_s[n // 2, :, half:half + CHUNK] = y.T[SA_HEAD_DIM:, :].astype(BF16)
            yi = _rope(miscf_ref[0, rows, :], ck, sk, s1, s2)
            kia = jnp.where(col < IDX_DIM, yi, 0.0)
            kia_s[rows, :] = kia.astype(BF16)
            kib_s[rows, :] = pltpu.roll(kia, IDX_DIM, 1).astype(BF16)

    c1 = cos_s[pl.ds(r0, QBLK), :]
    sn1 = sin_s[pl.ds(r0, QBLK), :]
    rep = lambda a, k: jnp.concatenate([a] * k, axis=1)
    nq = SA_HEADS * SA_HEAD_DIM // LANES
    xq = q_ref[0]
    yq = _rope(xq * qg_ref[...], rep(c1, nq), rep(sn1, nq), rep(s1, nq), rep(s2, nq))
    ss = jnp.dot((xq * xq).astype(BF16), hsum_ref[...], preferred_element_type=F32)
    r = lax.rsqrt(ss * (1.0 / SA_HEAD_DIM) + RMS_EPS)
    qn = (yq * (r * (SA_HEAD_DIM ** -0.5))).astype(BF16)
    for pr in range(nq):
        q2_s[pr * QBLK:(pr + 1) * QBLK, :] = qn[:, pr * LANES:(pr + 1) * LANES]
    ni = IDX_HEADS * IDX_DIM // LANES
    yi = _rope(qi_ref[0], rep(c1, ni), rep(sn1, ni), rep(s1, ni), rep(s2, ni)).astype(BF16)
    for pr in range(ni):
        qi2_s[pr * QBLK:(pr + 1) * QBLK, :] = yi[:, pr * LANES:(pr + 1) * LANES]
    mt = misct_ref[0].T
    wscale = (IDX_HEADS ** -0.5) * (IDX_DIM ** -0.5)
    wrow = jnp.concatenate([mt[M_W + hh:M_W + hh + 1, :] for hh in _even_odd(IDX_HEADS)],
                           axis=1) * wscale

    krow = lax.broadcasted_iota(I32, (KPAIR, QBLK), 0)
    qpos = r0 + lax.broadcasted_iota(I32, (KPAIR, QBLK), 1)
    big = -NEG_INF

    def fold8(x, op):
        parts = [x[i:i + SUBLANES] for i in range(0, x.shape[0], SUBLANES)]
        while len(parts) > 1:
            parts = [op(parts[i], parts[i + 1]) for i in range(0, len(parts), 2)]
        return parts[0]

    def idx_body(c, carry):
        k0 = pl.multiple_of(c * KPAIR, KPAIR)
        terms = []
        for keys_s in (kia_s, kib_s):
            keys = keys_s[pl.ds(k0, KPAIR), :]
            for pr in range(IDX_HEADS // 2):
                lt = _mm_nt(keys, qi2_s[pr * QBLK:(pr + 1) * QBLK, :])
                j = len(terms)
                terms.append(jnp.maximum(lt, 0.0) * wrow[:, j * QBLK:(j + 1) * QBLK])
        acc = (terms[0] + terms[1]) + (terms[2] + terms[3])
        causal = k0 + krow <= qpos
        sc = jnp.where(causal, acc, NEG_INF)
        sc_s[c] = sc
        lo8, hi8 = carry
        lo8 = jnp.minimum(lo8, fold8(jnp.where(causal, acc, big), jnp.minimum))
        hi8 = jnp.maximum(hi8, fold8(sc, jnp.maximum))
        return lo8, hi8

    lo8, hi8 = lax.fori_loop(0, npair, idx_body, (jnp.full((SUBLANES, QBLK), big, F32),
                                                  jnp.full((SUBLANES, QBLK), NEG_INF, F32)))
    lo = jnp.min(lo8, axis=0, keepdims=True)
    hi = jnp.max(hi8, axis=0, keepdims=True)

    def count(pred):
        def body(c, acc):
            return acc + fold8(jnp.where(pred(sc_s[c]), 1.0, 0.0), jnp.add)
        acc = lax.fori_loop(0, npair, body, jnp.zeros((SUBLANES, QBLK), F32))
        return jnp.sum(acc, axis=0, keepdims=True)

    def max_where(pred):
        def body(c, acc):
            s = sc_s[c]
            return jnp.maximum(acc, fold8(jnp.where(pred(s), s, NEG_INF), jnp.maximum))
        acc = lax.fori_loop(0, npair, body, jnp.full((SUBLANES, QBLK), NEG_INF, F32))
        return jnp.max(acc, axis=0, keepdims=True)

    kf = jnp.minimum(float(topk), (qpos[0:1, :] + 1).astype(F32))

    def bisect(_, bounds):
        lo, hi = bounds
        mid = lo + 0.5 * (hi - lo)
        above = count(lambda s: s > mid) >= kf
        return jnp.where(above, mid, lo), jnp.where(above, hi, mid)

    lo, hi = lax.fori_loop(0, BISECT_STEPS, bisect, (lo, hi))

    thr = max_where(lambda s: s <= hi)
    cge = count(lambda s: s >= thr)

    def short(cge):
        return jnp.max(jnp.where(cge < kf, 1.0, 0.0)) > 0.5

    def peel(state):
        thr, cge, _ = state
        nxt = max_where(lambda s: s < thr)
        cnx = count(lambda s: s >= nxt)
        step = cge < kf
        thr = jnp.where(step, nxt, thr)
        cge = jnp.where(step, cnx, cge)
        return thr, cge, short(cge)

    thr, cge, _ = lax.while_loop(lambda st: st[2], peel, (thr, cge, short(cge)))
    need = kf - count(lambda s: s > thr)

    lower = jnp.where(lax.broadcasted_iota(I32, (KPAIR, KPAIR), 0)
                      >= lax.broadcasted_iota(I32, (KPAIR, KPAIR), 1), 1.0, 0.0).astype(BF16)

    tile_heads = lambda b: jnp.concatenate([b] * SA_HEADS, axis=1)

    def sel_body(c, carry):
        off, m8 = carry
        k0 = pl.multiple_of(c * KPAIR, KPAIR)
        s = sc_s[c]
        eqf = jnp.where(s == thr, 1.0, 0.0)
        pref = jnp.dot(lower, eqf.astype(BF16), preferred_element_type=F32)
        tie = jnp.where(pref + off <= need, eqf, 0.0)
        keep = jnp.where(k0 + krow <= qpos, jnp.where(s > thr, 1.0, tie), 0.0)
        bias = jnp.where(keep > 0.5, 0.0, NEG_INF)
        bias_s[c] = bias
        tops = []
        for keys_s in (ka_s, kb_s):
            keys = keys_s[pl.ds(k0, KPAIR), :]
            for pr in range(SA_HEADS // 2):
                lg = _mm_nt(keys, q2_s[pr * QBLK:(pr + 1) * QBLK, :])
                lg_s[c, :, len(tops) * QBLK:(len(tops) + 1) * QBLK] = lg
                tops.append(fold8(lg + bias, jnp.maximum))
        m8 = jnp.maximum(m8, jnp.concatenate(tops, axis=1))
        return off + pref[KPAIR - 1:KPAIR, :], m8

    _, m8 = lax.fori_loop(0, npair, sel_body,
                          (jnp.zeros((1, QBLK), F32),
                           jnp.full((SUBLANES, SA_HEADS * QBLK), NEG_INF, F32)))
    mrow = jnp.max(m8, axis=0, keepdims=True)

    def probs(c, dst):
        slab = 2 * SUBLANES
        l8 = l8_s[...]
        for r in range(0, KPAIR, slab):
            p = jnp.exp(lg_s[c, r:r + slab, :] + tile_heads(bias_s[c, r:r + slab, :]) - mrow)
            dst[r:r + slab, :] = p.astype(BF16)
            l8 = l8 + (p[:SUBLANES] + p[SUBLANES:])
        l8_s[...] = l8

    def pv_acc(c, src):
        acc_s[...] += jnp.dot(vt_s[c], src[...], preferred_element_type=F32)

    acc_s[...] = jnp.zeros(acc_s.shape, F32)
    l8_s[...] = jnp.zeros(l8_s.shape, F32)
    probs(0, pa_s)

    def p2(i, carry):
        c = 2 * i + 1
        pv_acc(c - 1, pa_s)
        probs(c, pb_s)

        @pl.when(c + 1 < npair)
        def _():
            pv_acc(c, pb_s)
            probs(c + 1, pa_s)
        return carry

    lax.fori_loop(0, npair // 2, p2, 0)
    last = npair - 1

    @pl.when(last % 2 == 0)
    def _():
        pv_acc(last, pa_s)

    @pl.when(last % 2 == 1)
    def _():
        pv_acc(last, pb_s)

    out_t = acc_s[...] / jnp.sum(l8_s[...], axis=0, keepdims=True)
    order = _even_odd(SA_HEADS)
    for pr in range(SA_HEADS // 2):
        ev, od = order.index(2 * pr), order.index(2 * pr + 1)
        two = jnp.concatenate([out_t[:, ev * QBLK:(ev + 1) * QBLK],
                               out_t[:, od * QBLK:(od + 1) * QBLK]], axis=0)
        o_ref[0, :, pr * LANES:(pr + 1) * LANES] = two.T.astype(o_ref.dtype)


def _dsa(proj, pos3, qg_t, kg_p, rope_pat, topk):
    bsz, seq, _ = proj.shape
    nc = seq // CHUNK
    qw = SA_HEADS * SA_HEAD_DIM
    iw = IDX_HEADS * IDX_DIM
    assert seq % QBLK == 0 and QBLK == KPAIR
    npairs = seq // KPAIR
    pairs = lambda: pltpu.VMEM((npairs, KPAIR, QBLK), F32)
    head_of = jnp.arange(qw) // SA_HEAD_DIM
    head_sum = (head_of[:, None] == head_of[None, :]).astype(BF16)
    return pl.pallas_call(
        functools.partial(_dsa_kernel, topk),
        grid=(bsz, seq // QBLK),
        in_specs=[pl.BlockSpec((1, QBLK, qw), lambda b, j: (b, j, C_SAQ // qw)),
                  pl.BlockSpec((1, QBLK, iw), lambda b, j: (b, j, C_IXQ // iw)),
                  pl.BlockSpec((1, QBLK, LANES), lambda b, j: (b, j, C_MISC // LANES)),
                  pl.BlockSpec((1, seq, LANES), lambda b, j: (b, 0, C_KV // LANES)),
                  pl.BlockSpec((1, seq, LANES), lambda b, j: (b, 0, C_MISC // LANES)),
                  pl.BlockSpec((1, seq * ROT // LANES, LANES), lambda b, j: (b, 0, 0)),
                  _const_spec((1, qw)), _const_spec((1, LANES)), _const_spec((SUBLANES, LANES)),
                  _const_spec((qw, qw))],
        out_specs=pl.BlockSpec((1, QBLK, qw), lambda b, j: (b, j, 0)),
        out_shape=jax.ShapeDtypeStruct((bsz, seq, qw), BF16),
        scratch_shapes=[pltpu.VMEM((seq, LANES), F32), pltpu.VMEM((seq, LANES), F32),
                        pltpu.VMEM((seq, LANES), BF16), pltpu.VMEM((seq, LANES), BF16),
                        pltpu.VMEM((npairs, SA_HEAD_DIM, KPAIR), BF16),
                        pltpu.VMEM((seq, LANES), BF16), pltpu.VMEM((seq, LANES), BF16),
                        pltpu.VMEM((qw // LANES * QBLK, LANES), BF16),
                        pltpu.VMEM((iw // LANES * QBLK, LANES), BF16),
                        pairs(), pairs(),
                        pltpu.VMEM((npairs, KPAIR, SA_HEADS * QBLK), F32),
                        pltpu.VMEM((KPAIR, SA_HEADS * QBLK), BF16),
                        pltpu.VMEM((KPAIR, SA_HEADS * QBLK), BF16),
                        pltpu.VMEM((SUBLANES, SA_HEADS * QBLK), F32),
                        pltpu.VMEM((SA_HEAD_DIM, SA_HEADS * QBLK), F32)],
        compiler_params=pltpu.CompilerParams(dimension_semantics=("arbitrary", "arbitrary"),
                                             vmem_limit_bytes=VMEM_LIMIT),
        name="dsa",
    )(proj, proj, proj, proj, proj, pos3, qg_t, kg_p, rope_pat, head_sum)


def _merge_kernel(x_ref, oa_ref, ob_ref, g_ref, wg_ref, bg_ref, wa_ref, wb_ref, wo_ref, o_ref):
    x = x_ref[...]
    hb = _rms(x, g_ref[...]).astype(BF16)
    oa = oa_ref[...]
    ob = ob_ref[...]
    acc = x
    step = 512
    for n in range(0, D_MODEL, step):
        ga = _sigmoid(jnp.dot(hb, wg_ref[:, n:n + step], preferred_element_type=F32)
                      + bg_ref[:, n:n + step])
        gb = _sigmoid(jnp.dot(hb, wg_ref[:, D_MODEL + n:D_MODEL + n + step],
                              preferred_element_type=F32)
                      + bg_ref[:, D_MODEL + n:D_MODEL + n + step])
        ya = jnp.dot(oa, wa_ref[:, n:n + step], preferred_element_type=F32)
        yb = jnp.dot(ob, wb_ref[:, n:n + step], preferred_element_type=F32)
        merged = (ga * ya + gb * yb).astype(BF16)
        acc = acc + jnp.dot(merged, wo_ref[n:n + step, :], preferred_element_type=F32)
    o_ref[...] = acc


def _merge(x2d, oa2d, ob2d, gain, w_gate, b_gate, w_a, w_b, w_o, tm):
    t = x2d.shape[0]
    tok = lambda w: pl.BlockSpec((tm, w), lambda i: (i, 0))
    return pl.pallas_call(
        _merge_kernel,
        grid=(t // tm,),
        in_specs=[tok(D_MODEL), tok(DN_V), tok(SA_HEADS * SA_HEAD_DIM),
                  _const_spec((1, D_MODEL)), _const_spec((D_MODEL, 2 * D_MODEL)),
                  _const_spec((1, 2 * D_MODEL)), _const_spec((DN_V, D_MODEL)),
                  _const_spec((SA_HEADS * SA_HEAD_DIM, D_MODEL)), _const_spec((D_MODEL, D_MODEL))],
        out_specs=tok(D_MODEL),
        out_shape=jax.ShapeDtypeStruct((t, D_MODEL), F32),
        compiler_params=pltpu.CompilerParams(dimension_semantics=("arbitrary",),
                                             vmem_limit_bytes=VMEM_LIMIT),
        name="merge",
    )(x2d, oa2d, ob2d, gain, w_gate, b_gate, w_a, w_b, w_o)


def _ffn_kernel(x_ref, p_ref, g2_ref, wg_ref, wu_ref, wd_ref, g3_ref, wpg_ref, wp_ref, o_ref):
    x = x_ref[...]
    hb = _rms(x, g2_ref[...]).astype(BF16)
    acc = x
    step = 256
    for n in range(0, D_FF, step):
        gate = jnp.dot(hb, wg_ref[:, n:n + step], preferred_element_type=F32)
        up = jnp.dot(hb, wu_ref[:, n:n + step], preferred_element_type=F32)
        acc = acc + jnp.dot((_silu(gate) * up).astype(BF16), wd_ref[n:n + step, :],
                            preferred_element_type=F32)
    h3 = _rms(acc, g3_ref[...]).astype(BF16)
    pg = _sigmoid(jnp.dot(h3, wpg_ref[...], preferred_element_type=F32))
    pe = jnp.dot(p_ref[...].astype(BF16), wp_ref[...], preferred_element_type=F32)
    o_ref[...] = acc + pg * pe


def _ffn(x2d, p2d, g2, w_g, w_u, w_d, g3, w_pg, w_p, tm):
    t = x2d.shape[0]
    tok = lambda w: pl.BlockSpec((tm, w), lambda i: (i, 0))
    return pl.pallas_call(
        _ffn_kernel,
        grid=(t // tm,),
        in_specs=[tok(D_MODEL), tok(PLE_DIM), _const_spec((1, D_MODEL)),
                  _const_spec((D_MODEL, D_FF)), _const_spec((D_MODEL, D_FF)),
                  _const_spec((D_FF, D_MODEL)), _const_spec((1, D_MODEL)),
                  _const_spec((D_MODEL, D_MODEL)), _const_spec((PLE_DIM, D_MODEL))],
        out_specs=tok(D_MODEL),
        out_shape=jax.ShapeDtypeStruct((t, D_MODEL), F32),
        compiler_params=pltpu.CompilerParams(dimension_semantics=("arbitrary",),
                                             vmem_limit_bytes=VMEM_LIMIT),
        name="ffn_ple",
    )(x2d, p2d, g2, w_g, w_u, w_d, g3, w_pg, w_p)


def _regroup_w_in(w):
    sizes = (DN_CONV_CH, DN_V, DN_HEADS, DN_HEADS, SA_HEADS * SA_HEAD_DIM, SA_HEAD_DIM, SA_HEAD_DIM,
             IDX_HEADS * IDX_DIM, IDX_DIM, IDX_HEADS)
    parts, off = [], 0
    for s in sizes:
        parts.append(w[:, off:off + s])
        off += s
    qkv, z, b, a, saq, sak, sav, ixq, ixk, ixw = parts
    cols = [qkv, z, saq, ixq, sak, sav, ixk, b, a, ixw]
    used = sum(c.shape[1] for c in cols)
    cols.append(jnp.zeros((w.shape[0], PROJ_W - used), w.dtype))
    return jnp.concatenate(cols, axis=1)


def _rope_pattern():
    lane = jnp.arange(LANES) % SA_HEAD_DIM
    inv_freq = ROPE_THETA ** (-jnp.arange(0, ROT, 2, dtype=F32) / ROT)
    freq = inv_freq[jnp.arange(LANES) % HALF]
    s1 = jnp.where(lane < HALF, -1.0, 0.0)
    s2 = jnp.where((lane >= HALF) & (lane < ROT), 1.0, 0.0)
    pat = jnp.zeros((SUBLANES, LANES), F32)
    return pat.at[0].set(freq).at[1].set(s1).at[2].set(s2)


def _layer(x, p, positions, attn_norm, w_in, conv_w, a_log, dt_bias, dn_norm, q_norm, k_norm,
           w_branch_a, w_branch_b, w_gate, b_gate, w_o, ffn_norm, w_ffn_gate, w_ffn_up, w_ffn_down,
           ple_norm, w_ple_gate, w_ple):
    bsz, seq, _ = x.shape
    t = bsz * seq
    tm = 1024
    topk = min(IDX_TOPK_MAX, seq // 4)
    row = lambda v: v.reshape(1, -1).astype(F32)
    x2d = x.reshape(t, D_MODEL)

    proj = _in_proj(x2d, row(attn_norm), _regroup_w_in(w_in).astype(BF16), tm)
    proj = proj.reshape(bsz, seq, PROJ_W)

    pad4 = jnp.zeros((DN_HEADS,), F32)
    per_head = lambda v: jnp.broadcast_to(jnp.concatenate([pad4, v.astype(F32)])[:, None],
                                          (SUBLANES, LANES))
    o_a = _deltanet(proj, conv_w.astype(F32), per_head(a_log), per_head(dt_bias), row(dn_norm))

    qg_t = jnp.tile(q_norm.astype(F32), SA_HEADS).reshape(1, -1)
    kg_p = jnp.concatenate([k_norm.astype(F32), jnp.ones((LANES - SA_HEAD_DIM,), F32)]).reshape(1, -1)
    per_row = LANES // ROT
    pos_packed = jnp.repeat(positions.astype(I32).reshape(bsz, seq // per_row, per_row), ROT, axis=2)
    o_b = _dsa(proj, pos_packed, qg_t, kg_p, _rope_pattern(), topk)

    x1 = _merge(x2d, o_a.reshape(t, DN_V), o_b.reshape(t, -1), row(attn_norm),
                w_gate.astype(BF16), row(b_gate), w_branch_a.astype(BF16), w_branch_b.astype(BF16),
                w_o.astype(BF16), tm)
    x2 = _ffn(x1, p.reshape(t, PLE_DIM), row(ffn_norm), w_ffn_gate.astype(BF16),
              w_ffn_up.astype(BF16), w_ffn_down.astype(BF16), row(ple_norm),
              w_ple_gate.astype(BF16), w_ple.astype(BF16), tm)
    return x2.reshape(bsz, seq, D_MODEL)


def kernel(x, p, positions, attn_norm, w_in, conv_w, a_log, dt_bias, dn_norm, q_norm, k_norm,
           w_branch_a, w_branch_b, w_gate, b_gate, w_o, ffn_norm, w_ffn_gate, w_ffn_up, w_ffn_down,
           ple_norm, w_ple_gate, w_ple):
    depth = w_in.shape[0]
    for i in range(depth):
        x = _layer(x, p[i], positions, attn_norm[i], w_in[i], conv_w[i], a_log[i], dt_bias[i],
                   dn_norm[i], q_norm[i], k_norm[i], w_branch_a[i], w_branch_b[i], w_gate[i],
                   b_gate[i], w_o[i], ffn_norm[i], w_ffn_gate[i], w_ffn_up[i], w_ffn_down[i],
                   ple_norm[i], w_ple_gate[i], w_ple[i])
    return x
```

```python
import functools

import jax
import jax.numpy as jnp
from jax import lax
from jax.experimental import pallas as pl
from jax.experimental.pallas import tpu as pltpu

F32 = jnp.float32
BF16 = jnp.bfloat16
I32 = jnp.int32

D_MODEL = 1024
PLE_DIM = 256
RMS_EPS = 1e-6
DN_HEADS = 4
DN_DK = 128
DN_DV = 128
DN_CONV = 4
DN_QK = DN_HEADS * DN_DK
DN_V = DN_HEADS * DN_DV
DN_CONV_CH = 2 * DN_QK + DN_V
SA_HEADS = 8
SA_HEAD_DIM = 64
IDX_HEADS = 4
IDX_DIM = 64
IDX_TOPK_MAX = 256
ROPE_THETA = 500000.0
ROPE_FRACTION = 4
NEG_INF = -1e30
D_FF = 2816
LOG2E = 1.4426950408889634

LANES = 128
SUBLANES = 8
CHUNK = 128
DN_GROUP = 16
DN_BASE = 16
KPAIR = 2 * CHUNK
QBLK = KPAIR
BISECT_STEPS = 19

C_QKV = 0
C_Z = DN_CONV_CH
C_SAQ = C_Z + DN_V
C_IXQ = C_SAQ + SA_HEADS * SA_HEAD_DIM
C_KV = C_IXQ + IDX_HEADS * IDX_DIM
C_MISC = C_KV + 2 * SA_HEAD_DIM
PROJ_W = C_MISC + LANES
M_IXK = 0
M_B = IDX_DIM
M_A = M_B + DN_HEADS
M_W = M_A + DN_HEADS

ROT = SA_HEAD_DIM // ROPE_FRACTION
HALF = ROT // 2

VMEM_LIMIT = 56 * 1024 * 1024


def _const_spec(shape):
    nd = len(shape)
    return pl.BlockSpec(shape, lambda *_: (0,) * nd, pipeline_mode=pl.Buffered(1))


def _rms(x, g):
    return x * lax.rsqrt(jnp.mean(x * x, axis=-1, keepdims=True) + RMS_EPS) * g


def _mm(a, b):
    return jnp.dot(a.astype(BF16), b.astype(BF16), preferred_element_type=F32)


def _mm_nt(a, b):
    return lax.dot_general(a.astype(BF16), b.astype(BF16), (((1,), (1,)), ((), ())),
                           preferred_element_type=F32)


def _sigmoid(x):
    return 1.0 / (1.0 + jnp.exp(-x))


def _silu(x):
    return x * _sigmoid(x)


def _softplus(x):
    return jnp.maximum(x, 0.0) + jnp.log1p(jnp.exp(-jnp.abs(x)))


def _in_proj_kernel(x_ref, g_ref, w_ref, o_ref):
    hb = _rms(x_ref[...], g_ref[...]).astype(BF16)
    step = 512
    for n in range(0, PROJ_W, step):
        o_ref[:, n:n + step] = jnp.dot(hb, w_ref[:, n:n + step], preferred_element_type=F32)


def _in_proj(x2d, gain, w_perm, tm):
    t = x2d.shape[0]
    return pl.pallas_call(
        _in_proj_kernel,
        grid=(t // tm,),
        in_specs=[pl.BlockSpec((tm, D_MODEL), lambda i: (i, 0)),
                  _const_spec((1, D_MODEL)),
                  _const_spec((D_MODEL, PROJ_W))],
        out_specs=pl.BlockSpec((tm, PROJ_W), lambda i: (i, 0)),
        out_shape=jax.ShapeDtypeStruct((t, PROJ_W), F32),
        compiler_params=pltpu.CompilerParams(dimension_semantics=("arbitrary",),
                                             vmem_limit_bytes=VMEM_LIMIT),
        name="in_proj",
    )(x2d, gain, w_perm)


def _dn_kernel(q_ref, k_ref, v_ref, z_ref, misc_ref, cwq_ref, cwk_ref, cwv_ref, alog_ref, dtb_ref,
               ng_ref, o_ref,
               pad_s, cq_s, ck_s, cv_s, brow_s, grow_s, u_s, w_s, a_s, qg_s, c_s, n_s, st_s, el_s):
    seq = q_ref.shape[1]
    nc = seq // CHUNK
    h = pl.program_id(1)
    row = lax.broadcasted_iota(I32, (CHUNK, CHUNK), 0)
    col = lax.broadcasted_iota(I32, (CHUNK, CHUNK), 1)

    pad_s[0:SUBLANES, :] = jnp.zeros((SUBLANES, LANES), F32)
    for src, cw_ref, dst, kind in ((q_ref, cwq_ref, cq_s, "q"), (k_ref, cwk_ref, ck_s, "k"),
                                   (v_ref, cwv_ref, cv_s, "v")):
        pad_s[SUBLANES:seq + SUBLANES, :] = src[0]
        for n in range(nc):
            base = n * CHUNK + SUBLANES
            y = cw_ref[DN_CONV - 1:DN_CONV, :] * pad_s[base:base + CHUNK, :]
            for j in range(1, DN_CONV):
                y = y + (cw_ref[DN_CONV - 1 - j:DN_CONV - j, :]
                         * pad_s[base - j:base - j + CHUNK, :])
            y = _silu(y)
            if kind != "v":
                y = y * lax.rsqrt(jnp.sum(y * y, axis=-1, keepdims=True) + 1e-6)
            if kind == "q":
                y = y * (DN_DK ** -0.5)
            dst[n * CHUNK:(n + 1) * CHUNK, :] = y

    upper = jnp.where(row <= col, 1.0, 0.0).astype(F32)
    sub8 = lax.broadcasted_iota(I32, (SUBLANES, LANES), 0)
    for n in range(nc):
        mt = misc_ref[0, n * CHUNK:(n + 1) * CHUNK, :].T
        slab = mt[M_B:M_B + SUBLANES, :]
        beta8 = _sigmoid(slab)
        g8 = -jnp.exp(alog_ref[...]) * _softplus(slab + dtb_ref[...])
        gc8 = jnp.dot(g8, upper, preferred_element_type=F32, precision=lax.Precision.HIGHEST)
        b_row = jnp.sum(jnp.where(sub8 == h, beta8, 0.0), axis=0, keepdims=True)
        g_row = jnp.sum(jnp.where(sub8 == h + DN_HEADS, gc8, 0.0), axis=0, keepdims=True)
        brow_s[n] = jnp.broadcast_to(b_row, (SUBLANES, LANES))
        grow_s[n] = jnp.broadcast_to(g_row, (SUBLANES, LANES))

    eye = jnp.where(row == col, 1.0, 0.0).astype(F32)

    def prep_one(n):
        r0 = pl.multiple_of(n * CHUNK, CHUNK)
        q = cq_s[pl.ds(r0, CHUNK), :]
        k = ck_s[pl.ds(r0, CHUNK), :]
        v = cv_s[pl.ds(r0, CHUNK), :]
        g_r = jnp.broadcast_to(grow_s[n][0:1, :], (CHUNK, CHUNK))
        g_c = g_r.T
        b_c = jnp.broadcast_to(brow_s[n][0:1, :], (CHUNK, CHUNK)).T
        g_last = jnp.broadcast_to(g_c[CHUNK - 1:CHUNK, :], (CHUNK, CHUNK))
        decay = jnp.where(row >= col, jnp.exp(jnp.minimum(g_c - g_r, 0.0)), 0.0)
        kb = k * b_c
        kq = _mm_nt(jnp.concatenate([kb, q], axis=0), k)
        yield
        lmat = jnp.where(row > col, kq[:CHUNK] * decay, 0.0)
        a = jnp.where(row >= col, kq[CHUNK:] * decay, 0.0)
        same = lambda b: (row // b) == (col // b)
        l0 = jnp.where(same(DN_BASE), lmat, 0.0)
        x = eye - l0
        m = _mm(l0, l0)
        yield
        power = 2
        while 2 * power < DN_BASE:
            xm = _mm(jnp.concatenate([x, m], axis=0), m)
            yield
            x = x + xm[:CHUNK]
            m = xm[CHUNK:]
            power *= 2
        x = x + _mm(x, m)
        yield
        blk = DN_BASE
        while blk < CHUNK:
            below = jnp.where(same(2 * blk) & jnp.logical_not(same(blk)), lmat, 0.0)
            y = _mm(below, x)
            yield
            x = x - _mm(x, y)
            yield
            blk *= 2
        eg = jnp.exp(g_c)
        sol = _mm(x, jnp.concatenate([v * b_c, kb * eg], axis=1))
        yield
        cn = _mm((k * jnp.exp(g_last - g_c)).T, sol)
        yield
        u_s[n] = sol[:, :DN_DV]
        w_s[n] = sol[:, DN_DV:]
        a_s[n] = a
        qg_s[n] = q * eg
        c_s[n] = cn[:, :DN_DV]
        n_s[n] = cn[:, DN_DV:]
        el_s[n] = jnp.exp(g_last)[0:SUBLANES, :]

    def grouped(gen):
        def body(i, carry):
            live = [gen(i * DN_GROUP + j) for j in range(DN_GROUP)]
            while live:
                live = [g for g in live if next(g, True) is None]
            return carry
        lax.fori_loop(0, nc // DN_GROUP, body, 0)

    grouped(prep_one)

    def chain(n, state):
        st_s[n] = state
        el = jnp.broadcast_to(el_s[n][0:1, :], (CHUNK, CHUNK))
        return state * el + c_s[n] - _mm(n_s[n], state)

    lax.fori_loop(0, nc, chain, jnp.zeros((DN_DK, DN_DV), F32))

    def emit_one(n):
        r0 = pl.multiple_of(n * CHUNK, CHUNK)
        ws = _mm(jnp.concatenate([w_s[n], qg_s[n]], axis=0), st_s[n])
        yield
        o = ws[CHUNK:] + _mm(a_s[n], u_s[n] - ws[:CHUNK])
        yield
        z = z_ref[0, pl.ds(r0, CHUNK), :]
        o_ref[0, pl.ds(r0, CHUNK), :] = (_rms(o, ng_ref[...]) * _silu(z)).astype(o_ref.dtype)

    grouped(emit_one)


def _deltanet(proj, conv_w, alog8, dtb8, norm_g):
    bsz, seq, _ = proj.shape
    nc = seq // CHUNK
    qb, kb, vb, zb = (C_QKV // LANES, (C_QKV + DN_QK) // LANES, (C_QKV + 2 * DN_QK) // LANES,
                      C_Z // LANES)
    col = lambda off: pl.BlockSpec((1, seq, LANES), lambda b, h, off=off: (b, 0, off + h))
    cw = lambda off: pl.BlockSpec((DN_CONV, LANES), lambda b, h, off=off: (0, off + h))
    mat = lambda: pltpu.VMEM((nc, CHUNK, CHUNK), F32)
    return pl.pallas_call(
        _dn_kernel,
        grid=(bsz, DN_HEADS),
        in_specs=[col(qb), col(kb), col(vb), col(zb),
                  pl.BlockSpec((1, seq, LANES), lambda b, h: (b, 0, C_MISC // LANES)),
                  cw(qb), cw(kb), cw(vb),
                  _const_spec((SUBLANES, LANES)), _const_spec((SUBLANES, LANES)),
                  _const_spec((1, DN_DV))],
        out_specs=pl.BlockSpec((1, seq, LANES), lambda b, h: (b, 0, h)),
        out_shape=jax.ShapeDtypeStruct((bsz, seq, DN_V), BF16),
        scratch_shapes=[pltpu.VMEM((seq + SUBLANES, LANES), F32),
                        pltpu.VMEM((seq, LANES), F32), pltpu.VMEM((seq, LANES), F32),
                        pltpu.VMEM((seq, LANES), F32),
                        pltpu.VMEM((nc, SUBLANES, LANES), F32), pltpu.VMEM((nc, SUBLANES, LANES), F32),
                        mat(), mat(), mat(), mat(), mat(), mat(), mat(),
                        pltpu.VMEM((nc, SUBLANES, LANES), F32)],
        compiler_params=pltpu.CompilerParams(dimension_semantics=("arbitrary", "arbitrary"),
                                             vmem_limit_bytes=VMEM_LIMIT),
        name="deltanet",
    )(proj, proj, proj, proj, proj, conv_w, conv_w, conv_w, alog8, dtb8, norm_g)


def _even_odd(n):
    return list(range(0, n, 2)) + list(range(1, n, 2))


def _rope(x, c, s, s1, s2):
    w = x.shape[-1]
    return x * c + pltpu.roll(x, w - HALF, 1) * (s * s1) + pltpu.roll(x, HALF, 1) * (s * s2)


def _dsa_kernel(topk, q_ref, qi_ref, misct_ref, kv_ref, miscf_ref, pos_ref, qg_ref, kg_ref, rp_ref,
                hsum_ref, o_ref,
                cos_s, sin_s, ka_s, kb_s, vt_s, kia_s, kib_s, q2_s, qi2_s, sc_s, bias_s, lg_s, pa_s,
                pb_s, l8_s, acc_s):
    seq = kv_ref.shape[1]
    nc = seq // CHUNK
    qb = pl.program_id(1)
    npair = qb + 1
    r0 = pl.multiple_of(qb * QBLK, QBLK)
    col = lax.broadcasted_iota(I32, (CHUNK, CHUNK), 1)
    s1 = rp_ref[1:2, :]
    s2 = rp_ref[2:3, :]

    @pl.when(qb == 0)
    def _():
        ang = pos_ref[0].astype(F32) * rp_ref[0:1, :]
        packed = (jnp.cos(ang), jnp.sin(ang))
        lane = lax.broadcasted_iota(I32, ang.shape, 1)
        second = (lane >= SA_HEAD_DIM) & (lane < SA_HEAD_DIM + ROT)
        per_row = LANES // ROT
        for j in range(per_row):
            for tab, dst, rest in zip(packed, (cos_s, sin_s), (1.0, 0.0)):
                mine = tab if j == 0 else pltpu.roll(tab, LANES - j * ROT, 1)
                full = jnp.where(lane < ROT, mine,
                                 jnp.where(second, pltpu.roll(mine, SA_HEAD_DIM, 1), rest))
                dst[pl.ds(j, seq // per_row, stride=per_row), :] = full
        for n in range(nc):
            rows = slice(n * CHUNK, (n + 1) * CHUNK)
            c = cos_s[rows, :]
            s = sin_s[rows, :]
            ck = jnp.where(col < ROT, c, 1.0)
            sk = jnp.where(col < ROT, s, 0.0)
            kv = kv_ref[0, rows, :]
            ss = jnp.dot((kv * kv).astype(BF16), hsum_ref[0:LANES, 0:LANES],
                         preferred_element_type=F32)
            r = lax.rsqrt(ss * (1.0 / SA_HEAD_DIM) + RMS_EPS)
            y = jnp.where(col < SA_HEAD_DIM, kv * r * kg_ref[...], kv)
            y = _rope(y, ck, sk, s1, s2)
            ka = jnp.where(col < SA_HEAD_DIM, y, 0.0)
            ka_s[rows, :] = ka.astype(BF16)
            kb_s[rows, :] = pltpu.roll(ka, SA_HEAD_DIM, 1).astype(BF16)
            half = (n % 2) * CHUNK
            vt_s[n // 2, :, half:half + CHUNK] = y.T[SA_HEAD_DIM:, :].astype(BF16)
            yi = _rope(miscf_ref[0, rows, :], ck, sk, s1, s2)
            kia = jnp.where(col < IDX_DIM, yi, 0.0)
            kia_s[rows, :] = kia.astype(BF16)
            kib_s[rows, :] = pltpu.roll(kia, IDX_DIM, 1).astype(BF16)

    c1 = cos_s[pl.ds(r0, QBLK), :]
    sn1 = sin_s[pl.ds(r0, QBLK), :]
    rep = lambda a, k: jnp.concatenate([a] * k, axis=1)
    nq = SA_HEADS * SA_HEAD_DIM // LANES
    xq = q_ref[0]
    yq = _rope(xq * qg_ref[...], rep(c1, nq), rep(sn1, nq), rep(s1, nq), rep(s2, nq))
    ss = jnp.dot((xq * xq).astype(BF16), hsum_ref[...], preferred_element_type=F32)
    r = lax.rsqrt(ss * (1.0 / SA_HEAD_DIM) + RMS_EPS)
    qn = (yq * (r * (SA_HEAD_DIM ** -0.5 * LOG2E))).astype(BF16)
    for pr in range(nq):
        q2_s[pr * QBLK:(pr + 1) * QBLK, :] = qn[:, pr * LANES:(pr + 1) * LANES]
    ni = IDX_HEADS * IDX_DIM // LANES
    yi = _rope(qi_ref[0], rep(c1, ni), rep(sn1, ni), rep(s1, ni), rep(s2, ni)).astype(BF16)
    for pr in range(ni):
        qi2_s[pr * QBLK:(pr + 1) * QBLK, :] = yi[:, pr * LANES:(pr + 1) * LANES]
    mt = misct_ref[0].T
    wscale = (IDX_HEADS ** -0.5) * (IDX_DIM ** -0.5)
    wrow = jnp.concatenate([mt[M_W + hh:M_W + hh + 1, :] for hh in _even_odd(IDX_HEADS)],
                           axis=1) * wscale

    krow = lax.broadcasted_iota(I32, (KPAIR, QBLK), 0)
    qpos = r0 + lax.broadcasted_iota(I32, (KPAIR, QBLK), 1)
    big = -NEG_INF

    def fold8(x, op):
        parts = [x[i:i + SUBLANES] for i in range(0, x.shape[0], SUBLANES)]
        while len(parts) > 1:
            parts = [op(parts[i], parts[i + 1]) for i in range(0, len(parts), 2)]
        return parts[0]

    def idx_body(c, carry):
        k0 = pl.multiple_of(c * KPAIR, KPAIR)
        terms = []
        for keys_s in (kia_s, kib_s):
            keys = keys_s[pl.ds(k0, KPAIR), :]
            for pr in range(IDX_HEADS // 2):
                lt = _mm_nt(keys, qi2_s[pr * QBLK:(pr + 1) * QBLK, :])
                j = len(terms)
                terms.append(jnp.maximum(lt, 0.0) * wrow[:, j * QBLK:(j + 1) * QBLK])
        acc = (terms[0] + terms[1]) + (terms[2] + terms[3])
        causal = k0 + krow <= qpos
        sc = jnp.where(causal, acc, NEG_INF)
        sc_s[c] = sc
        lo8, hi8 = carry
        lo8 = jnp.minimum(lo8, fold8(jnp.where(causal, acc, big), jnp.minimum))
        hi8 = jnp.maximum(hi8, fold8(sc, jnp.maximum))
        return lo8, hi8

    lo8, hi8 = lax.fori_loop(0, npair, idx_body, (jnp.full((SUBLANES, QBLK), big, F32),
                                                  jnp.full((SUBLANES, QBLK), NEG_INF, F32)))
    lo = jnp.min(lo8, axis=0, keepdims=True)
    hi = jnp.max(hi8, axis=0, keepdims=True)

    def count(pred):
        def body(c, acc):
            return acc + fold8(jnp.where(pred(sc_s[c]), 1.0, 0.0), jnp.add)
        acc = lax.fori_loop(0, npair, body, jnp.zeros((SUBLANES, QBLK), F32))
        return jnp.sum(acc, axis=0, keepdims=True)

    def max_where(pred):
        def body(c, acc):
            s = sc_s[c]
            return jnp.maximum(acc, fold8(jnp.where(pred(s), s, NEG_INF), jnp.maximum))
        acc = lax.fori_loop(0, npair, body, jnp.full((SUBLANES, QBLK), NEG_INF, F32))
        return jnp.max(acc, axis=0, keepdims=True)

    kf = jnp.minimum(float(topk), (qpos[0:1, :] + 1).astype(F32))

    def bisect(_, bounds):
        lo, hi = bounds
        mid = lo + 0.5 * (hi - lo)
        above = count(lambda s: s > mid) >= kf
        return jnp.where(above, mid, lo), jnp.where(above, hi, mid)

    lo, hi = lax.fori_loop(0, BISECT_STEPS, bisect, (lo, hi))

    thr = max_where(lambda s: s <= hi)
    cge = count(lambda s: s >= thr)

    def short(cge):
        return jnp.max(jnp.where(cge < kf, 1.0, 0.0)) > 0.5

    def peel(state):
        thr, cge, _ = state
        nxt = max_where(lambda s: s < thr)
        cnx = count(lambda s: s >= nxt)
        step = cge < kf
        thr = jnp.where(step, nxt, thr)
        cge = jnp.where(step, cnx, cge)
        return thr, cge, short(cge)

    thr, cge, _ = lax.while_loop(lambda st: st[2], peel, (thr, cge, short(cge)))
    need = kf - count(lambda s: s > thr)

    lower = jnp.where(lax.broadcasted_iota(I32, (KPAIR, KPAIR), 0)
                      >= lax.broadcasted_iota(I32, (KPAIR, KPAIR), 1), 1.0, 0.0).astype(BF16)

    tile_heads = lambda b: jnp.concatenate([b] * SA_HEADS, axis=1)

    def sel_body(c, carry):
        off, m8 = carry
        k0 = pl.multiple_of(c * KPAIR, KPAIR)
        s = sc_s[c]
        eqf = jnp.where(s == thr, 1.0, 0.0)
        pref = jnp.dot(lower, eqf.astype(BF16), preferred_element_type=F32)
        tie = jnp.where(pref + off <= need, eqf, 0.0)
        keep = jnp.where(k0 + krow <= qpos, jnp.where(s > thr, 1.0, tie), 0.0)
        bias = jnp.where(keep > 0.5, 0.0, NEG_INF)
        bias_s[c] = bias
        tops = []
        for keys_s in (ka_s, kb_s):
            keys = keys_s[pl.ds(k0, KPAIR), :]
            for pr in range(SA_HEADS // 2):
                lg = _mm_nt(keys, q2_s[pr * QBLK:(pr + 1) * QBLK, :])
                lg_s[c, :, len(tops) * QBLK:(len(tops) + 1) * QBLK] = lg
                tops.append(fold8(lg + bias, jnp.maximum))
        m8 = jnp.maximum(m8, jnp.concatenate(tops, axis=1))
        return off + pref[KPAIR - 1:KPAIR, :], m8

    _, m8 = lax.fori_loop(0, npair, sel_body,
                          (jnp.zeros((1, QBLK), F32),
                           jnp.full((SUBLANES, SA_HEADS * QBLK), NEG_INF, F32)))
    mrow = jnp.max(m8, axis=0, keepdims=True)

    def probs(c, dst):
        slab = 2 * SUBLANES
        l8 = l8_s[...]
        for r in range(0, KPAIR, slab):
            p = jnp.exp2(lg_s[c, r:r + slab, :] + tile_heads(bias_s[c, r:r + slab, :]) - mrow)
            dst[r:r + slab, :] = p.astype(BF16)
            l8 = l8 + (p[:SUBLANES] + p[SUBLANES:])
        l8_s[...] = l8

    def pv_acc(c, src):
        acc_s[...] += jnp.dot(vt_s[c], src[...], preferred_element_type=F32)

    acc_s[...] = jnp.zeros(acc_s.shape, F32)
    l8_s[...] = jnp.zeros(l8_s.shape, F32)
    probs(0, pa_s)

    def p2(i, carry):
        c = 2 * i + 1
        pv_acc(c - 1, pa_s)
        probs(c, pb_s)

        @pl.when(c + 1 < npair)
        def _():
            pv_acc(c, pb_s)
            probs(c + 1, pa_s)
        return carry

    lax.fori_loop(0, npair // 2, p2, 0)
    last = npair - 1

    @pl.when(last % 2 == 0)
    def _():
        pv_acc(last, pa_s)

    @pl.when(last % 2 == 1)
    def _():
        pv_acc(last, pb_s)

    out_t = acc_s[...] / jnp.sum(l8_s[...], axis=0, keepdims=True)
    order = _even_odd(SA_HEADS)
    for pr in range(SA_HEADS // 2):
        ev, od = order.index(2 * pr), order.index(2 * pr + 1)
        two = jnp.concatenate([out_t[:, ev * QBLK:(ev + 1) * QBLK],
                               out_t[:, od * QBLK:(od + 1) * QBLK]], axis=0)
        o_ref[0, :, pr * LANES:(pr + 1) * LANES] = two.T.astype(o_ref.dtype)


def _dsa(proj, pos3, qg_t, kg_p, rope_pat, topk):
    bsz, seq, _ = proj.shape
    nc = seq // CHUNK
    qw = SA_HEADS * SA_HEAD_DIM
    iw = IDX_HEADS * IDX_DIM
    assert seq % QBLK == 0 and QBLK == KPAIR
    npairs = seq // KPAIR
    pairs = lambda: pltpu.VMEM((npairs, KPAIR, QBLK), F32)
    head_of = jnp.arange(qw) // SA_HEAD_DIM
    head_sum = (head_of[:, None] == head_of[None, :]).astype(BF16)
    return pl.pallas_call(
        functools.partial(_dsa_kernel, topk),
        grid=(bsz, seq // QBLK),
        in_specs=[pl.BlockSpec((1, QBLK, qw), lambda b, j: (b, j, C_SAQ // qw)),
                  pl.BlockSpec((1, QBLK, iw), lambda b, j: (b, j, C_IXQ // iw)),
                  pl.BlockSpec((1, QBLK, LANES), lambda b, j: (b, j, C_MISC // LANES)),
                  pl.BlockSpec((1, seq, LANES), lambda b, j: (b, 0, C_KV // LANES)),
                  pl.BlockSpec((1, seq, LANES), lambda b, j: (b, 0, C_MISC // LANES)),
                  pl.BlockSpec((1, seq * ROT // LANES, LANES), lambda b, j: (b, 0, 0)),
                  _const_spec((1, qw)), _const_spec((1, LANES)), _const_spec((SUBLANES, LANES)),
                  _const_spec((qw, qw))],
        out_specs=pl.BlockSpec((1, QBLK, qw), lambda b, j: (b, j, 0)),
        out_shape=jax.ShapeDtypeStruct((bsz, seq, qw), BF16),
        scratch_shapes=[pltpu.VMEM((seq, LANES), F32), pltpu.VMEM((seq, LANES), F32),
                        pltpu.VMEM((seq, LANES), BF16), pltpu.VMEM((seq, LANES), BF16),
                        pltpu.VMEM((npairs, SA_HEAD_DIM, KPAIR), BF16),
                        pltpu.VMEM((seq, LANES), BF16), pltpu.VMEM((seq, LANES), BF16),
                        pltpu.VMEM((qw // LANES * QBLK, LANES), BF16),
                        pltpu.VMEM((iw // LANES * QBLK, LANES), BF16),
                        pairs(), pairs(),
                        pltpu.VMEM((npairs, KPAIR, SA_HEADS * QBLK), F32),
                        pltpu.VMEM((KPAIR, SA_HEADS * QBLK), BF16),
                        pltpu.VMEM((KPAIR, SA_HEADS * QBLK), BF16),
                        pltpu.VMEM((SUBLANES, SA_HEADS * QBLK), F32),
                        pltpu.VMEM((SA_HEAD_DIM, SA_HEADS * QBLK), F32)],
        compiler_params=pltpu.CompilerParams(dimension_semantics=("arbitrary", "arbitrary"),
                                             vmem_limit_bytes=VMEM_LIMIT),
        name="dsa",
    )(proj, proj, proj, proj, proj, pos3, qg_t, kg_p, rope_pat, head_sum)


def _merge_kernel(x_ref, oa_ref, ob_ref, g_ref, wg_ref, bg_ref, wa_ref, wb_ref, wo_ref, o_ref):
    x = x_ref[...]
    hb = _rms(x, g_ref[...]).astype(BF16)
    oa = oa_ref[...]
    ob = ob_ref[...]
    acc = x
    step = 512
    for n in range(0, D_MODEL, step):
        ga = _sigmoid(jnp.dot(hb, wg_ref[:, n:n + step], preferred_element_type=F32)
                      + bg_ref[:, n:n + step])
        gb = _sigmoid(jnp.dot(hb, wg_ref[:, D_MODEL + n:D_MODEL + n + step],
                              preferred_element_type=F32)
                      + bg_ref[:, D_MODEL + n:D_MODEL + n + step])
        ya = jnp.dot(oa, wa_ref[:, n:n + step], preferred_element_type=F32)
        yb = jnp.dot(ob, wb_ref[:, n:n + step], preferred_element_type=F32)
        merged = (ga * ya + gb * yb).astype(BF16)
        acc = acc + jnp.dot(merged, wo_ref[n:n + step, :], preferred_element_type=F32)
    o_ref[...] = acc


def _merge(x2d, oa2d, ob2d, gain, w_gate, b_gate, w_a, w_b, w_o, tm):
    t = x2d.shape[0]
    tok = lambda w: pl.BlockSpec((tm, w), lambda i: (i, 0))
    return pl.pallas_call(
        _merge_kernel,
        grid=(t // tm,),
        in_specs=[tok(D_MODEL), tok(DN_V), tok(SA_HEADS * SA_HEAD_DIM),
                  _const_spec((1, D_MODEL)), _const_spec((D_MODEL, 2 * D_MODEL)),
                  _const_spec((1, 2 * D_MODEL)), _const_spec((DN_V, D_MODEL)),
                  _const_spec((SA_HEADS * SA_HEAD_DIM, D_MODEL)), _const_spec((D_MODEL, D_MODEL))],
        out_specs=tok(D_MODEL),
        out_shape=jax.ShapeDtypeStruct((t, D_MODEL), F32),
        compiler_params=pltpu.CompilerParams(dimension_semantics=("arbitrary",),
                                             vmem_limit_bytes=VMEM_LIMIT),
        name="merge",
    )(x2d, oa2d, ob2d, gain, w_gate, b_gate, w_a, w_b, w_o)


def _ffn_kernel(x_ref, p_ref, g2_ref, wg_ref, wu_ref, wd_ref, g3_ref, wpg_ref, wp_ref, o_ref):
    x = x_ref[...]
    hb = _rms(x, g2_ref[...]).astype(BF16)
    acc = x
    step = 256
    for n in range(0, D_FF, step):
        gate = jnp.dot(hb, wg_ref[:, n:n + step], preferred_element_type=F32)
        up = jnp.dot(hb, wu_ref[:, n:n + step], preferred_element_type=F32)
        acc = acc + jnp.dot((_silu(gate) * up).astype(BF16), wd_ref[n:n + step, :],
                            preferred_element_type=F32)
    h3 = _rms(acc, g3_ref[...]).astype(BF16)
    pg = _sigmoid(jnp.dot(h3, wpg_ref[...], preferred_element_type=F32))
    pe = jnp.dot(p_ref[...].astype(BF16), wp_ref[...], preferred_element_type=F32)
    o_ref[...] = acc + pg * pe


def _ffn(x2d, p2d, g2, w_g, w_u, w_d, g3, w_pg, w_p, tm):
    t = x2d.shape[0]
    tok = lambda w: pl.BlockSpec((tm, w), lambda i: (i, 0))
    return pl.pallas_call(
        _ffn_kernel,
        grid=(t // tm,),
        in_specs=[tok(D_MODEL), tok(PLE_DIM), _const_spec((1, D_MODEL)),
                  _const_spec((D_MODEL, D_FF)), _const_spec((D_MODEL, D_FF)),
                  _const_spec((D_FF, D_MODEL)), _const_spec((1, D_MODEL)),
                  _const_spec((D_MODEL, D_MODEL)), _const_spec((PLE_DIM, D_MODEL))],
        out_specs=tok(D_MODEL),
        out_shape=jax.ShapeDtypeStruct((t, D_MODEL), F32),
        compiler_params=pltpu.CompilerParams(dimension_semantics=("arbitrary",),
                                             vmem_limit_bytes=VMEM_LIMIT),
        name="ffn_ple",
    )(x2d, p2d, g2, w_g, w_u, w_d, g3, w_pg, w_p)


def _regroup_w_in(w):
    sizes = (DN_CONV_CH, DN_V, DN_HEADS, DN_HEADS, SA_HEADS * SA_HEAD_DIM, SA_HEAD_DIM, SA_HEAD_DIM,
             IDX_HEADS * IDX_DIM, IDX_DIM, IDX_HEADS)
    parts, off = [], 0
    for s in sizes:
        parts.append(w[:, off:off + s])
        off += s
    qkv, z, b, a, saq, sak, sav, ixq, ixk, ixw = parts
    cols = [qkv, z, saq, ixq, sak, sav, ixk, b, a, ixw]
    used = sum(c.shape[1] for c in cols)
    cols.append(jnp.zeros((w.shape[0], PROJ_W - used), w.dtype))
    return jnp.concatenate(cols, axis=1)


def _rope_pattern():
    lane = jnp.arange(LANES) % SA_HEAD_DIM
    inv_freq = ROPE_THETA ** (-jnp.arange(0, ROT, 2, dtype=F32) / ROT)
    freq = inv_freq[jnp.arange(LANES) % HALF]
    s1 = jnp.where(lane < HALF, -1.0, 0.0)
    s2 = jnp.where((lane >= HALF) & (lane < ROT), 1.0, 0.0)
    pat = jnp.zeros((SUBLANES, LANES), F32)
    return pat.at[0].set(freq).at[1].set(s1).at[2].set(s2)


def _layer(x, p, positions, attn_norm, w_in, conv_w, a_log, dt_bias, dn_norm, q_norm, k_norm,
           w_branch_a, w_branch_b, w_gate, b_gate, w_o, ffn_norm, w_ffn_gate, w_ffn_up, w_ffn_down,
           ple_norm, w_ple_gate, w_ple):
    bsz, seq, _ = x.shape
    t = bsz * seq
    tm = 1024
    topk = min(IDX_TOPK_MAX, seq // 4)
    row = lambda v: v.reshape(1, -1).astype(F32)
    x2d = x.reshape(t, D_MODEL)

    proj = _in_proj(x2d, row(attn_norm), _regroup_w_in(w_in).astype(BF16), tm)
    proj = proj.reshape(bsz, seq, PROJ_W)

    pad4 = jnp.zeros((DN_HEADS,), F32)
    per_head = lambda v: jnp.broadcast_to(jnp.concatenate([pad4, v.astype(F32)])[:, None],
                                          (SUBLANES, LANES))
    o_a = _deltanet(proj, conv_w.astype(F32), per_head(a_log), per_head(dt_bias), row(dn_norm))

    qg_t = jnp.tile(q_norm.astype(F32), SA_HEADS).reshape(1, -1)
    kg_p = jnp.concatenate([k_norm.astype(F32), jnp.ones((LANES - SA_HEAD_DIM,), F32)]).reshape(1, -1)
    per_row = LANES // ROT
    pos_packed = jnp.repeat(positions.astype(I32).reshape(bsz, seq // per_row, per_row), ROT, axis=2)
    o_b = _dsa(proj, pos_packed, qg_t, kg_p, _rope_pattern(), topk)

    x1 = _merge(x2d, o_a.reshape(t, DN_V), o_b.reshape(t, -1), row(attn_norm),
                w_gate.astype(BF16), row(b_gate), w_branch_a.astype(BF16), w_branch_b.astype(BF16),
                w_o.astype(BF16), tm)
    x2 = _ffn(x1, p.reshape(t, PLE_DIM), row(ffn_norm), w_ffn_gate.astype(BF16),
              w_ffn_up.astype(BF16), w_ffn_down.astype(BF16), row(ple_norm),
              w_ple_gate.astype(BF16), w_ple.astype(BF16), tm)
    return x2.reshape(bsz, seq, D_MODEL)


def kernel(x, p, positions, attn_norm, w_in, conv_w, a_log, dt_bias, dn_norm, q_norm, k_norm,
           w_branch_a, w_branch_b, w_gate, b_gate, w_o, ffn_norm, w_ffn_gate, w_ffn_up, w_ffn_down,
           ple_norm, w_ple_gate, w_ple):
    depth = w_in.shape[0]
    for i in range(depth):
        x = _layer(x, p[i], positions, attn_norm[i], w_in[i], conv_w[i], a_log[i], dt_bias[i],
                   dn_norm[i], q_norm[i], k_norm[i], w_branch_a[i], w_branch_b[i], w_gate[i],
                   b_gate[i], w_o[i], ffn_norm[i], w_ffn_gate[i], w_ffn_up[i], w_ffn_down[i],
                   ple_norm[i], w_ple_gate[i], w_ple[i])
    return x
```

```python
import functools

import jax
import jax.numpy as jnp
from jax import lax
from jax.experimental import pallas as pl
from jax.experimental.pallas import tpu as pltpu

F32 = jnp.float32
BF16 = jnp.bfloat16
I32 = jnp.int32

D_MODEL = 1024
PLE_DIM = 256
RMS_EPS = 1e-6
DN_HEADS = 4
DN_DK = 128
DN_DV = 128
DN_CONV = 4
DN_QK = DN_HEADS * DN_DK
DN_V = DN_HEADS * DN_DV
DN_CONV_CH = 2 * DN_QK + DN_V
SA_HEADS = 8
SA_HEAD_DIM = 64
IDX_HEADS = 4
IDX_DIM = 64
IDX_TOPK_MAX = 256
ROPE_THETA = 500000.0
ROPE_FRACTION = 4
NEG_INF = -1e30
D_FF = 2816
LOG2E = 1.4426950408889634

LANES = 128
SUBLANES = 8
CHUNK = 128
DN_BASE = 16
KPAIR = 2 * CHUNK
QBLK = KPAIR
BISECT_STEPS = 19

C_QKV = 0
C_Z = DN_CONV_CH
C_SAQ = C_Z + DN_V
C_IXQ = C_SAQ + SA_HEADS * SA_HEAD_DIM
C_KV = C_IXQ + IDX_HEADS * IDX_DIM
C_MISC = C_KV + 2 * SA_HEAD_DIM
PROJ_W = C_MISC + LANES
M_IXK = 0
M_B = IDX_DIM
M_A = M_B + DN_HEADS
M_W = M_A + DN_HEADS

ROT = SA_HEAD_DIM // ROPE_FRACTION
HALF = ROT // 2

VMEM_LIMIT = 56 * 1024 * 1024


def _const_spec(shape):
    nd = len(shape)
    return pl.BlockSpec(shape, lambda *_: (0,) * nd, pipeline_mode=pl.Buffered(1))


def _rms(x, g):
    return x * lax.rsqrt(jnp.mean(x * x, axis=-1, keepdims=True) + RMS_EPS) * g


def _mm(a, b):
    return jnp.dot(a.astype(BF16), b.astype(BF16), preferred_element_type=F32)


def _mm_nt(a, b):
    return lax.dot_general(a.astype(BF16), b.astype(BF16), (((1,), (1,)), ((), ())),
                           preferred_element_type=F32)


def _sigmoid(x):
    return 1.0 / (1.0 + jnp.exp(-x))


def _silu(x):
    return x * _sigmoid(x)


def _softplus(x):
    return jnp.maximum(x, 0.0) + jnp.log1p(jnp.exp(-jnp.abs(x)))


def _in_proj_kernel(x_ref, g_ref, w_ref, o_ref):
    hb = _rms(x_ref[...], g_ref[...]).astype(BF16)
    step = 512
    for n in range(0, PROJ_W, step):
        o_ref[:, n:n + step] = jnp.dot(hb, w_ref[:, n:n + step], preferred_element_type=F32)


def _in_proj(x2d, gain, w_perm, tm):
    t = x2d.shape[0]
    return pl.pallas_call(
        _in_proj_kernel,
        grid=(t // tm,),
        in_specs=[pl.BlockSpec((tm, D_MODEL), lambda i: (i, 0)),
                  _const_spec((1, D_MODEL)),
                  _const_spec((D_MODEL, PROJ_W))],
        out_specs=pl.BlockSpec((tm, PROJ_W), lambda i: (i, 0)),
        out_shape=jax.ShapeDtypeStruct((t, PROJ_W), F32),
        compiler_params=pltpu.CompilerParams(dimension_semantics=("arbitrary",),
                                             vmem_limit_bytes=VMEM_LIMIT),
        name="in_proj",
    )(x2d, gain, w_perm)


def _dn_kernel(q_ref, k_ref, v_ref, z_ref, misc_ref, cwq_ref, cwk_ref, cwv_ref, alog_ref, dtb_ref,
               ng_ref, o_ref,
               pad_s, cq_s, ck_s, cv_s, brow_s, grow_s, u_s, w_s, a_s, qg_s, c_s, n_s, el_s):
    seq = q_ref.shape[1]
    nc = seq // CHUNK
    h = pl.program_id(1)
    row = lax.broadcasted_iota(I32, (CHUNK, CHUNK), 0)
    col = lax.broadcasted_iota(I32, (CHUNK, CHUNK), 1)

    def conv(chunks):
        for n in chunks:
            for src, cw_ref, dst, kind in ((q_ref, cwq_ref, cq_s, "q"), (k_ref, cwk_ref, ck_s, "k"),
                                           (v_ref, cwv_ref, cv_s, "v")):
                if n == 0:
                    pad_s[0:SUBLANES, :] = jnp.zeros((SUBLANES, LANES), F32)
                    pad_s[SUBLANES:SUBLANES + CHUNK, :] = src[0, 0:CHUNK, :]
                    tap = lambda j: pad_s[SUBLANES - j:SUBLANES - j + CHUNK, :]
                else:
                    tap = lambda j, src=src: src[0, n * CHUNK - j:(n + 1) * CHUNK - j, :]
                y = cw_ref[DN_CONV - 1:DN_CONV, :] * tap(0)
                for j in range(1, DN_CONV):
                    y = y + cw_ref[DN_CONV - 1 - j:DN_CONV - j, :] * tap(j)
                y = _silu(y)
                if kind != "v":
                    y = y * lax.rsqrt(jnp.sum(y * y, axis=-1, keepdims=True) + 1e-6)
                if kind == "q":
                    y = y * (DN_DK ** -0.5)
                dst[n * CHUNK:(n + 1) * CHUNK, :] = y
            yield

    upper = jnp.where(row <= col, 1.0, 0.0).astype(F32)
    sub8 = lax.broadcasted_iota(I32, (SUBLANES, LANES), 0)
    for n in range(nc):
        mt = misc_ref[0, n * CHUNK:(n + 1) * CHUNK, :].T
        slab = mt[M_B:M_B + SUBLANES, :]
        beta8 = _sigmoid(slab)
        g8 = -jnp.exp(alog_ref[...]) * _softplus(slab + dtb_ref[...])
        gc8 = jnp.dot(g8, upper, preferred_element_type=F32, precision=lax.Precision.HIGHEST)
        b_row = jnp.sum(jnp.where(sub8 == h, beta8, 0.0), axis=0, keepdims=True)
        g_row = jnp.sum(jnp.where(sub8 == h + DN_HEADS, gc8, 0.0), axis=0, keepdims=True)
        brow_s[n] = jnp.broadcast_to(b_row, (SUBLANES, LANES))
        grow_s[n] = jnp.broadcast_to(g_row, (SUBLANES, LANES))

    eye = jnp.where(row == col, 1.0, 0.0).astype(F32)

    def prep_one(n):
        rows = slice(n * CHUNK, (n + 1) * CHUNK)
        q = cq_s[rows, :]
        k = ck_s[rows, :]
        v = cv_s[rows, :]
        g_r = jnp.broadcast_to(grow_s[n][0:1, :], (CHUNK, CHUNK))
        g_c = g_r.T
        b_c = jnp.broadcast_to(brow_s[n][0:1, :], (CHUNK, CHUNK)).T
        g_last = jnp.broadcast_to(g_c[CHUNK - 1:CHUNK, :], (CHUNK, CHUNK))
        decay = jnp.where(row >= col, jnp.exp(jnp.minimum(g_c - g_r, 0.0)), 0.0)
        kb = k * b_c
        kq = _mm_nt(jnp.concatenate([kb, q], axis=0), k)
        yield
        lmat = jnp.where(row > col, kq[:CHUNK] * decay, 0.0)
        a = jnp.where(row >= col, kq[CHUNK:] * decay, 0.0)
        same = lambda b: (row // b) == (col // b)
        l0 = jnp.where(same(DN_BASE), lmat, 0.0)
        x = eye - l0
        m = _mm(l0, l0)
        yield
        power = 2
        while 2 * power < DN_BASE:
            xm = _mm(jnp.concatenate([x, m], axis=0), m)
            yield
            x = x + xm[:CHUNK]
            m = xm[CHUNK:]
            power *= 2
        x = x + _mm(x, m)
        yield
        blk = DN_BASE
        while blk < CHUNK:
            below = jnp.where(same(2 * blk) & jnp.logical_not(same(blk)), lmat, 0.0)
            y = _mm(below, x)
            yield
            x = x - _mm(x, y)
            yield
            blk *= 2
        eg = jnp.exp(g_c)
        sol = _mm(x, jnp.concatenate([v * b_c, kb * eg], axis=1))
        yield
        cn = _mm((k * jnp.exp(g_last - g_c)).T, sol)
        yield
        u_s[n] = sol[:, :DN_DV]
        w_s[n] = sol[:, DN_DV:]
        a_s[n] = a
        qg_s[n] = q * eg
        c_s[n] = cn[:, :DN_DV]
        n_s[n] = cn[:, DN_DV:]
        el_s[n] = jnp.exp(g_last)[0:SUBLANES, :]

    carried = [jnp.zeros((DN_DK, DN_DV), F32)]

    def recur(chunks):
        for n in chunks:
            state = carried[0]
            el = jnp.broadcast_to(el_s[n][0:1, :], (CHUNK, CHUNK))
            nws = _mm(jnp.concatenate([n_s[n], w_s[n], qg_s[n]], axis=0), state)
            yield
            carried[0] = state * el + c_s[n] - nws[:CHUNK]
            o = nws[2 * CHUNK:] + _mm(a_s[n], u_s[n] - nws[CHUNK:2 * CHUNK])
            yield
            rows = slice(n * CHUNK, (n + 1) * CHUNK)
            o_ref[0, rows, :] = (_rms(o, ng_ref[...]) * _silu(z_ref[0, rows, :])).astype(o_ref.dtype)

    def weave(*gens):
        live = list(gens)
        while live:
            live = [g for g in live if next(g, True) is None]

    half = nc // 2
    weave(conv(range(half)))
    weave(*[prep_one(n) for n in range(half)], conv(range(half, nc)))
    weave(*[prep_one(n) for n in range(half, nc)], recur(range(half)))
    weave(recur(range(half, nc)))


def _deltanet(proj, conv_w, alog8, dtb8, norm_g):
    bsz, seq, _ = proj.shape
    nc = seq // CHUNK
    assert seq % (2 * CHUNK) == 0
    qb, kb, vb, zb = (C_QKV // LANES, (C_QKV + DN_QK) // LANES, (C_QKV + 2 * DN_QK) // LANES,
                      C_Z // LANES)
    col = lambda off: pl.BlockSpec((1, seq, LANES), lambda b, h, off=off: (b, 0, off + h))
    cw = lambda off: pl.BlockSpec((DN_CONV, LANES), lambda b, h, off=off: (0, off + h))
    mat = lambda: pltpu.VMEM((nc, CHUNK, CHUNK), F32)
    return pl.pallas_call(
        _dn_kernel,
        grid=(bsz, DN_HEADS),
        in_specs=[col(qb), col(kb), col(vb), col(zb),
                  pl.BlockSpec((1, seq, LANES), lambda b, h: (b, 0, C_MISC // LANES)),
                  cw(qb), cw(kb), cw(vb),
                  _const_spec((SUBLANES, LANES)), _const_spec((SUBLANES, LANES)),
                  _const_spec((1, DN_DV))],
        out_specs=pl.BlockSpec((1, seq, LANES), lambda b, h: (b, 0, h)),
        out_shape=jax.ShapeDtypeStruct((bsz, seq, DN_V), BF16),
        scratch_shapes=[pltpu.VMEM((SUBLANES + CHUNK, LANES), F32),
                        pltpu.VMEM((seq, LANES), F32), pltpu.VMEM((seq, LANES), F32),
                        pltpu.VMEM((seq, LANES), F32),
                        pltpu.VMEM((nc, SUBLANES, LANES), F32), pltpu.VMEM((nc, SUBLANES, LANES), F32),
                        mat(), mat(), mat(), mat(), mat(), mat(),
                        pltpu.VMEM((nc, SUBLANES, LANES), F32)],
        compiler_params=pltpu.CompilerParams(dimension_semantics=("arbitrary", "arbitrary"),
                                             vmem_limit_bytes=VMEM_LIMIT),
        name="deltanet",
    )(proj, proj, proj, proj, proj, conv_w, conv_w, conv_w, alog8, dtb8, norm_g)


def _even_odd(n):
    return list(range(0, n, 2)) + list(range(1, n, 2))


def _rope(x, c, s, s1, s2):
    w = x.shape[-1]
    return x * c + pltpu.roll(x, w - HALF, 1) * (s * s1) + pltpu.roll(x, HALF, 1) * (s * s2)


def _dsa_kernel(topk, q_ref, qi_ref, misct_ref, kv_ref, miscf_ref, pos_ref, qg_ref, kg_ref, rp_ref,
                hsum_ref, o_ref,
                cos_s, sin_s, ka_s, kb_s, vt_s, kia_s, kib_s, q2_s, qi2_s, sc_s, bias_s, lg_s, pa_s,
                pb_s, l8_s, acc_s):
    seq = kv_ref.shape[1]
    nc = seq // CHUNK
    qb = pl.program_id(1)
    npair = qb + 1
    r0 = pl.multiple_of(qb * QBLK, QBLK)
    col = lax.broadcasted_iota(I32, (CHUNK, CHUNK), 1)
    s1 = rp_ref[1:2, :]
    s2 = rp_ref[2:3, :]

    @pl.when(qb == 0)
    def _():
        ang = pos_ref[0].astype(F32) * rp_ref[0:1, :]
        packed = (jnp.cos(ang), jnp.sin(ang))
        lane = lax.broadcasted_iota(I32, ang.shape, 1)
        second = (lane >= SA_HEAD_DIM) & (lane < SA_HEAD_DIM + ROT)
        per_row = LANES // ROT
        for j in range(per_row):
            for tab, dst, rest in zip(packed, (cos_s, sin_s), (1.0, 0.0)):
                mine = tab if j == 0 else pltpu.roll(tab, LANES - j * ROT, 1)
                full = jnp.where(lane < ROT, mine,
                                 jnp.where(second, pltpu.roll(mine, SA_HEAD_DIM, 1), rest))
                dst[pl.ds(j, seq // per_row, stride=per_row), :] = full
        for n in range(nc):
            rows = slice(n * CHUNK, (n + 1) * CHUNK)
            c = cos_s[rows, :]
            s = sin_s[rows, :]
            ck = jnp.where(col < ROT, c, 1.0)
            sk = jnp.where(col < ROT, s, 0.0)
            kv = kv_ref[0, rows, :]
            ss = jnp.dot((kv * kv).astype(BF16), hsum_ref[0:LANES, 0:LANES],
                         preferred_element_type=F32)
            r = lax.rsqrt(ss * (1.0 / SA_HEAD_DIM) + RMS_EPS)
            y = jnp.where(col < SA_HEAD_DIM, kv * r * kg_ref[...], kv)
            y = _rope(y, ck, sk, s1, s2)
            ka = jnp.where(col < SA_HEAD_DIM, y, 0.0)
            ka_s[rows, :] = ka.astype(BF16)
            kb_s[rows, :] = pltpu.roll(ka, SA_HEAD_DIM, 1).astype(BF16)
            half = (n % 2) * CHUNK
            vt_s[n // 2, :, half:half + CHUNK] = y.T[SA_HEAD_DIM:, :].astype(BF16)
            yi = _rope(miscf_ref[0, rows, :], ck, sk, s1, s2)
            kia = jnp.where(col < IDX_DIM, yi, 0.0)
            kia_s[rows, :] = kia.astype(BF16)
            kib_s[rows, :] = pltpu.roll(kia, IDX_DIM, 1).astype(BF16)

    c1 = cos_s[pl.ds(r0, QBLK), :]
    sn1 = sin_s[pl.ds(r0, QBLK), :]
    rep = lambda a, k: jnp.concatenate([a] * k, axis=1)
    nq = SA_HEADS * SA_HEAD_DIM // LANES
    xq = q_ref[0]
    yq = _rope(xq * qg_ref[...], rep(c1, nq), rep(sn1, nq), rep(s1, nq), rep(s2, nq))
    ss = jnp.dot((xq * xq).astype(BF16), hsum_ref[...], preferred_element_type=F32)
    r = lax.rsqrt(ss * (1.0 / SA_HEAD_DIM) + RMS_EPS)
    qn = (yq * (r * (SA_HEAD_DIM ** -0.5 * LOG2E))).astype(BF16)
    for pr in range(nq):
        q2_s[pr * QBLK:(pr + 1) * QBLK, :] = qn[:, pr * LANES:(pr + 1) * LANES]
    ni = IDX_HEADS * IDX_DIM // LANES
    yi = _rope(qi_ref[0], rep(c1, ni), rep(sn1, ni), rep(s1, ni), rep(s2, ni)).astype(BF16)
    for pr in range(ni):
        qi2_s[pr * QBLK:(pr + 1) * QBLK, :] = yi[:, pr * LANES:(pr + 1) * LANES]
    mt = misct_ref[0].T
    wscale = (IDX_HEADS ** -0.5) * (IDX_DIM ** -0.5)
    wrow = jnp.concatenate([mt[M_W + hh:M_W + hh + 1, :] for hh in _even_odd(IDX_HEADS)],
                           axis=1) * wscale

    krow = lax.broadcasted_iota(I32, (KPAIR, QBLK), 0)
    qpos = r0 + lax.broadcasted_iota(I32, (KPAIR, QBLK), 1)
    big = -NEG_INF

    def fold8(x, op):
        parts = [x[i:i + SUBLANES] for i in range(0, x.shape[0], SUBLANES)]
        while len(parts) > 1:
            parts = [op(parts[i], parts[i + 1]) for i in range(0, len(parts), 2)]
        return parts[0]

    def idx_body(c, carry):
        k0 = pl.multiple_of(c * KPAIR, KPAIR)
        terms = []
        for keys_s in (kia_s, kib_s):
            keys = keys_s[pl.ds(k0, KPAIR), :]
            for pr in range(IDX_HEADS // 2):
                lt = _mm_nt(keys, qi2_s[pr * QBLK:(pr + 1) * QBLK, :])
                j = len(terms)
                terms.append(jnp.maximum(lt, 0.0) * wrow[:, j * QBLK:(j + 1) * QBLK])
        acc = (terms[0] + terms[1]) + (terms[2] + terms[3])
        causal = k0 + krow <= qpos
        sc = jnp.where(causal, acc, NEG_INF)
        sc_s[c] = sc
        lo8, hi8 = carry
        lo8 = jnp.minimum(lo8, fold8(jnp.where(causal, acc, big), jnp.minimum))
        hi8 = jnp.maximum(hi8, fold8(sc, jnp.maximum))
        return lo8, hi8

    lo8, hi8 = lax.fori_loop(0, npair, idx_body, (jnp.full((SUBLANES, QBLK), big, F32),
                                                  jnp.full((SUBLANES, QBLK), NEG_INF, F32)))
    lo = jnp.min(lo8, axis=0, keepdims=True)
    hi = jnp.max(hi8, axis=0, keepdims=True)

    def count(pred):
        def body(c, acc):
            return acc + fold8(jnp.where(pred(sc_s[c]), 1.0, 0.0), jnp.add)
        acc = lax.fori_loop(0, npair, body, jnp.zeros((SUBLANES, QBLK), F32))
        return jnp.sum(acc, axis=0, keepdims=True)

    def max_where(pred):
        def body(c, acc):
            s = sc_s[c]
            return jnp.maximum(acc, fold8(jnp.where(pred(s), s, NEG_INF), jnp.maximum))
        acc = lax.fori_loop(0, npair, body, jnp.full((SUBLANES, QBLK), NEG_INF, F32))
        return jnp.max(acc, axis=0, keepdims=True)

    kf = jnp.minimum(float(topk), (qpos[0:1, :] + 1).astype(F32))

    def bisect(_, bounds):
        lo, hi = bounds
        mid = lo + 0.5 * (hi - lo)
        above = count(lambda s: s > mid) >= kf
        return jnp.where(above, mid, lo), jnp.where(above, hi, mid)

    lo, hi = lax.fori_loop(0, BISECT_STEPS, bisect, (lo, hi))

    thr = max_where(lambda s: s <= hi)
    cge = count(lambda s: s >= thr)

    def short(cge):
        return jnp.max(jnp.where(cge < kf, 1.0, 0.0)) > 0.5

    def peel(state):
        thr, cge, _ = state
        nxt = max_where(lambda s: s < thr)
        cnx = count(lambda s: s >= nxt)
        step = cge < kf
        thr = jnp.where(step, nxt, thr)
        cge = jnp.where(step, cnx, cge)
        return thr, cge, short(cge)

    thr, cge, _ = lax.while_loop(lambda st: st[2], peel, (thr, cge, short(cge)))
    need = kf - count(lambda s: s > thr)

    lower = jnp.where(lax.broadcasted_iota(I32, (KPAIR, KPAIR), 0)
                      >= lax.broadcasted_iota(I32, (KPAIR, KPAIR), 1), 1.0, 0.0).astype(BF16)

    tile_heads = lambda b: jnp.concatenate([b] * SA_HEADS, axis=1)

    def sel_body(c, carry):
        off, m8 = carry
        k0 = pl.multiple_of(c * KPAIR, KPAIR)
        s = sc_s[c]
        eqf = jnp.where(s == thr, 1.0, 0.0)
        pref = jnp.dot(lower, eqf.astype(BF16), preferred_element_type=F32)
        tie = jnp.where(pref + off <= need, eqf, 0.0)
        keep = jnp.where(k0 + krow <= qpos, jnp.where(s > thr, 1.0, tie), 0.0)
        bias = jnp.where(keep > 0.5, 0.0, NEG_INF)
        bias_s[c] = bias
        tops = []
        for keys_s in (ka_s, kb_s):
            keys = keys_s[pl.ds(k0, KPAIR), :]
            for pr in range(SA_HEADS // 2):
                lg = _mm_nt(keys, q2_s[pr * QBLK:(pr + 1) * QBLK, :])
                lg_s[c, :, len(tops) * QBLK:(len(tops) + 1) * QBLK] = lg
                tops.append(fold8(lg + bias, jnp.maximum))
        m8 = jnp.maximum(m8, jnp.concatenate(tops, axis=1))
        return off + pref[KPAIR - 1:KPAIR, :], m8

    _, m8 = lax.fori_loop(0, npair, sel_body,
                          (jnp.zeros((1, QBLK), F32),
                           jnp.full((SUBLANES, SA_HEADS * QBLK), NEG_INF, F32)))
    mrow = jnp.max(m8, axis=0, keepdims=True)

    def probs(c, dst):
        slab = 2 * SUBLANES
        l8 = l8_s[...]
        for r in range(0, KPAIR, slab):
            p = jnp.exp2(lg_s[c, r:r + slab, :] + tile_heads(bias_s[c, r:r + slab, :]) - mrow)
            dst[r:r + slab, :] = p.astype(BF16)
            l8 = l8 + (p[:SUBLANES] + p[SUBLANES:])
        l8_s[...] = l8

    def pv_acc(c, src):
        acc_s[...] += jnp.dot(vt_s[c], src[...], preferred_element_type=F32)

    acc_s[...] = jnp.zeros(acc_s.shape, F32)
    l8_s[...] = jnp.zeros(l8_s.shape, F32)
    probs(0, pa_s)

    def p2(i, carry):
        c = 2 * i + 1
        pv_acc(c - 1, pa_s)
        probs(c, pb_s)

        @pl.when(c + 1 < npair)
        def _():
            pv_acc(c, pb_s)
            probs(c + 1, pa_s)
        return carry

    lax.fori_loop(0, npair // 2, p2, 0)
    last = npair - 1

    @pl.when(last % 2 == 0)
    def _():
        pv_acc(last, pa_s)

    @pl.when(last % 2 == 1)
    def _():
        pv_acc(last, pb_s)

    out_t = acc_s[...] / jnp.sum(l8_s[...], axis=0, keepdims=True)
    order = _even_odd(SA_HEADS)
    for pr in range(SA_HEADS // 2):
        ev, od = order.index(2 * pr), order.index(2 * pr + 1)
        two = jnp.concatenate([out_t[:, ev * QBLK:(ev + 1) * QBLK],
                               out_t[:, od * QBLK:(od + 1) * QBLK]], axis=0)
        o_ref[0, :, pr * LANES:(pr + 1) * LANES] = two.T.astype(o_ref.dtype)


def _dsa(proj, pos3, qg_t, kg_p, rope_pat, topk):
    bsz, seq, _ = proj.shape
    nc = seq // CHUNK
    qw = SA_HEADS * SA_HEAD_DIM
    iw = IDX_HEADS * IDX_DIM
    assert seq % QBLK == 0 and QBLK == KPAIR
    npairs = seq // KPAIR
    pairs = lambda: pltpu.VMEM((npairs, KPAIR, QBLK), F32)
    head_of = jnp.arange(qw) // SA_HEAD_DIM
    head_sum = (head_of[:, None] == head_of[None, :]).astype(BF16)
    return pl.pallas_call(
        functools.partial(_dsa_kernel, topk),
        grid=(bsz, seq // QBLK),
        in_specs=[pl.BlockSpec((1, QBLK, qw), lambda b, j: (b, j, C_SAQ // qw)),
                  pl.BlockSpec((1, QBLK, iw), lambda b, j: (b, j, C_IXQ // iw)),
                  pl.BlockSpec((1, QBLK, LANES), lambda b, j: (b, j, C_MISC // LANES)),
                  pl.BlockSpec((1, seq, LANES), lambda b, j: (b, 0, C_KV // LANES)),
                  pl.BlockSpec((1, seq, LANES), lambda b, j: (b, 0, C_MISC // LANES)),
                  pl.BlockSpec((1, seq * ROT // LANES, LANES), lambda b, j: (b, 0, 0)),
                  _const_spec((1, qw)), _const_spec((1, LANES)), _const_spec((SUBLANES, LANES)),
                  _const_spec((qw, qw))],
        out_specs=pl.BlockSpec((1, QBLK, qw), lambda b, j: (b, j, 0)),
        out_shape=jax.ShapeDtypeStruct((bsz, seq, qw), BF16),
        scratch_shapes=[pltpu.VMEM((seq, LANES), F32), pltpu.VMEM((seq, LANES), F32),
                        pltpu.VMEM((seq, LANES), BF16), pltpu.VMEM((seq, LANES), BF16),
                        pltpu.VMEM((npairs, SA_HEAD_DIM, KPAIR), BF16),
                        pltpu.VMEM((seq, LANES), BF16), pltpu.VMEM((seq, LANES), BF16),
                        pltpu.VMEM((qw // LANES * QBLK, LANES), BF16),
                        pltpu.VMEM((iw // LANES * QBLK, LANES), BF16),
                        pairs(), pairs(),
                        pltpu.VMEM((npairs, KPAIR, SA_HEADS * QBLK), F32),
                        pltpu.VMEM((KPAIR, SA_HEADS * QBLK), BF16),
                        pltpu.VMEM((KPAIR, SA_HEADS * QBLK), BF16),
                        pltpu.VMEM((SUBLANES, SA_HEADS * QBLK), F32),
                        pltpu.VMEM((SA_HEAD_DIM, SA_HEADS * QBLK), F32)],
        compiler_params=pltpu.CompilerParams(dimension_semantics=("arbitrary", "arbitrary"),
                                             vmem_limit_bytes=VMEM_LIMIT),
        name="dsa",
    )(proj, proj, proj, proj, proj, pos3, qg_t, kg_p, rope_pat, head_sum)


def _merge_kernel(x_ref, oa_ref, ob_ref, g_ref, wg_ref, bg_ref, wa_ref, wb_ref, wo_ref, o_ref):
    x = x_ref[...]
    hb = _rms(x, g_ref[...]).astype(BF16)
    oa = oa_ref[...]
    ob = ob_ref[...]
    acc = x
    step = 512
    for n in range(0, D_MODEL, step):
        ga = _sigmoid(jnp.dot(hb, wg_ref[:, n:n + step], preferred_element_type=F32)
                      + bg_ref[:, n:n + step])
        gb = _sigmoid(jnp.dot(hb, wg_ref[:, D_MODEL + n:D_MODEL + n + step],
                              preferred_element_type=F32)
                      + bg_ref[:, D_MODEL + n:D_MODEL + n + step])
        ya = jnp.dot(oa, wa_ref[:, n:n + step], preferred_element_type=F32)
        yb = jnp.dot(ob, wb_ref[:, n:n + step], preferred_element_type=F32)
        merged = (ga * ya + gb * yb).astype(BF16)
        acc = acc + jnp.dot(merged, wo_ref[n:n + step, :], preferred_element_type=F32)
    o_ref[...] = acc


def _merge(x2d, oa2d, ob2d, gain, w_gate, b_gate, w_a, w_b, w_o, tm):
    t = x2d.shape[0]
    tok = lambda w: pl.BlockSpec((tm, w), lambda i: (i, 0))
    return pl.pallas_call(
        _merge_kernel,
        grid=(t // tm,),
        in_specs=[tok(D_MODEL), tok(DN_V), tok(SA_HEADS * SA_HEAD_DIM),
                  _const_spec((1, D_MODEL)), _const_spec((D_MODEL, 2 * D_MODEL)),
                  _const_spec((1, 2 * D_MODEL)), _const_spec((DN_V, D_MODEL)),
                  _const_spec((SA_HEADS * SA_HEAD_DIM, D_MODEL)), _const_spec((D_MODEL, D_MODEL))],
        out_specs=tok(D_MODEL),
        out_shape=jax.ShapeDtypeStruct((t, D_MODEL), F32),
        compiler_params=pltpu.CompilerParams(dimension_semantics=("arbitrary",),
                                             vmem_limit_bytes=VMEM_LIMIT),
        name="merge",
    )(x2d, oa2d, ob2d, gain, w_gate, b_gate, w_a, w_b, w_o)


def _ffn_kernel(x_ref, p_ref, g2_ref, wg_ref, wu_ref, wd_ref, g3_ref, wpg_ref, wp_ref, o_ref):
    x = x_ref[...]
    hb = _rms(x, g2_ref[...]).astype(BF16)
    acc = x
    step = 256
    for n in range(0, D_FF, step):
        gate = jnp.dot(hb, wg_ref[:, n:n + step], preferred_element_type=F32)
        up = jnp.dot(hb, wu_ref[:, n:n + step], preferred_element_type=F32)
        acc = acc + jnp.dot((_silu(gate) * up).astype(BF16), wd_ref[n:n + step, :],
                            preferred_element_type=F32)
    h3 = _rms(acc, g3_ref[...]).astype(BF16)
    pg = _sigmoid(jnp.dot(h3, wpg_ref[...], preferred_element_type=F32))
    pe = jnp.dot(p_ref[...].astype(BF16), wp_ref[...], preferred_element_type=F32)
    o_ref[...] = acc + pg * pe


def _ffn(x2d, p2d, g2, w_g, w_u, w_d, g3, w_pg, w_p, tm):
    t = x2d.shape[0]
    tok = lambda w: pl.BlockSpec((tm, w), lambda i: (i, 0))
    return pl.pallas_call(
        _ffn_kernel,
        grid=(t // tm,),
        in_specs=[tok(D_MODEL), tok(PLE_DIM), _const_spec((1, D_MODEL)),
                  _const_spec((D_MODEL, D_FF)), _const_spec((D_MODEL, D_FF)),
                  _const_spec((D_FF, D_MODEL)), _const_spec((1, D_MODEL)),
                  _const_spec((D_MODEL, D_MODEL)), _const_spec((PLE_DIM, D_MODEL))],
        out_specs=tok(D_MODEL),
        out_shape=jax.ShapeDtypeStruct((t, D_MODEL), F32),
        compiler_params=pltpu.CompilerParams(dimension_semantics=("arbitrary",),
                                             vmem_limit_bytes=VMEM_LIMIT),
        name="ffn_ple",
    )(x2d, p2d, g2, w_g, w_u, w_d, g3, w_pg, w_p)


def _regroup_w_in(w):
    sizes = (DN_CONV_CH, DN_V, DN_HEADS, DN_HEADS, SA_HEADS * SA_HEAD_DIM, SA_HEAD_DIM, SA_HEAD_DIM,
             IDX_HEADS * IDX_DIM, IDX_DIM, IDX_HEADS)
    parts, off = [], 0
    for s in sizes:
        parts.append(w[:, off:off + s])
        off += s
    qkv, z, b, a, saq, sak, sav, ixq, ixk, ixw = parts
    cols = [qkv, z, saq, ixq, sak, sav, ixk, b, a, ixw]
    used = sum(c.shape[1] for c in cols)
    cols.append(jnp.zeros((w.shape[0], PROJ_W - used), w.dtype))
    return jnp.concatenate(cols, axis=1)


def _rope_pattern():
    lane = jnp.arange(LANES) % SA_HEAD_DIM
    inv_freq = ROPE_THETA ** (-jnp.arange(0, ROT, 2, dtype=F32) / ROT)
    freq = inv_freq[jnp.arange(LANES) % HALF]
    s1 = jnp.where(lane < HALF, -1.0, 0.0)
    s2 = jnp.where((lane >= HALF) & (lane < ROT), 1.0, 0.0)
    pat = jnp.zeros((SUBLANES, LANES), F32)
    return pat.at[0].set(freq).at[1].set(s1).at[2].set(s2)


def _layer(x, p, positions, attn_norm, w_in, conv_w, a_log, dt_bias, dn_norm, q_norm, k_norm,
           w_branch_a, w_branch_b, w_gate, b_gate, w_o, ffn_norm, w_ffn_gate, w_ffn_up, w_ffn_down,
           ple_norm, w_ple_gate, w_ple):
    bsz, seq, _ = x.shape
    t = bsz * seq
    tm = 1024
    topk = min(IDX_TOPK_MAX, seq // 4)
    row = lambda v: v.reshape(1, -1).astype(F32)
    x2d = x.reshape(t, D_MODEL)

    proj = _in_proj(x2d, row(attn_norm), _regroup_w_in(w_in).astype(BF16), tm)
    proj = proj.reshape(bsz, seq, PROJ_W)

    pad4 = jnp.zeros((DN_HEADS,), F32)
    per_head = lambda v: jnp.broadcast_to(jnp.concatenate([pad4, v.astype(F32)])[:, None],
                                          (SUBLANES, LANES))
    o_a = _deltanet(proj, conv_w.astype(F32), per_head(a_log), per_head(dt_bias), row(dn_norm))

    qg_t = jnp.tile(q_norm.astype(F32), SA_HEADS).reshape(1, -1)
    kg_p = jnp.concatenate([k_norm.astype(F32), jnp.ones((LANES - SA_HEAD_DIM,), F32)]).reshape(1, -1)
    per_row = LANES // ROT
    pos_packed = jnp.repeat(positions.astype(I32).reshape(bsz, seq // per_row, per_row), ROT, axis=2)
    o_b = _dsa(proj, pos_packed, qg_t, kg_p, _rope_pattern(), topk)

    x1 = _merge(x2d, o_a.reshape(t, DN_V), o_b.reshape(t, -1), row(attn_norm),
                w_gate.astype(BF16), row(b_gate), w_branch_a.astype(BF16), w_branch_b.astype(BF16),
                w_o.astype(BF16), tm)
    x2 = _ffn(x1, p.reshape(t, PLE_DIM), row(ffn_norm), w_ffn_gate.astype(BF16),
              w_ffn_up.astype(BF16), w_ffn_down.astype(BF16), row(ple_norm),
              w_ple_gate.astype(BF16), w_ple.astype(BF16), tm)
    return x2.reshape(bsz, seq, D_MODEL)


def kernel(x, p, positions, attn_norm, w_in, conv_w, a_log, dt_bias, dn_norm, q_norm, k_norm,
           w_branch_a, w_branch_b, w_gate, b_gate, w_o, ffn_norm, w_ffn_gate, w_ffn_up, w_ffn_down,
           ple_norm, w_ple_gate, w_ple):
    depth = w_in.shape[0]
    for i in range(depth):
        x = _layer(x, p[i], positions, attn_norm[i], w_in[i], conv_w[i], a_log[i], dt_bias[i],
                   dn_norm[i], q_norm[i], k_norm[i], w_branch_a[i], w_branch_b[i], w_gate[i],
                   b_gate[i], w_o[i], ffn_norm[i], w_ffn_gate[i], w_ffn_up[i], w_ffn_down[i],
                   ple_norm[i], w_ple_gate[i], w_ple[i])
    return x
```

```python
import functools

import jax
import jax.numpy as jnp
from jax import lax
from jax.experimental import pallas as pl
from jax.experimental.pallas import tpu as pltpu

F32 = jnp.float32
BF16 = jnp.bfloat16
I32 = jnp.int32

D_MODEL = 1024
PLE_DIM = 256
RMS_EPS = 1e-6
DN_HEADS = 4
DN_DK = 128
DN_DV = 128
DN_CONV = 4
DN_QK = DN_HEADS * DN_DK
DN_V = DN_HEADS * DN_DV
DN_CONV_CH = 2 * DN_QK + DN_V
SA_HEADS = 8
SA_HEAD_DIM = 64
IDX_HEADS = 4
IDX_DIM = 64
IDX_TOPK_MAX = 256
ROPE_THETA = 500000.0
ROPE_FRACTION = 4
NEG_INF = -1e30
D_FF = 2816
LOG2E = 1.4426950408889634

LANES = 128
SUBLANES = 8
CHUNK = 128
DN_BASE = 32
DN_PARTS = 2
KPAIR = 2 * CHUNK
QBLK = KPAIR
BISECT_STEPS = 19

C_QKV = 0
C_Z = DN_CONV_CH
C_SAQ = C_Z + DN_V
C_IXQ = C_SAQ + SA_HEADS * SA_HEAD_DIM
C_KV = C_IXQ + IDX_HEADS * IDX_DIM
C_MISC = C_KV + 2 * SA_HEAD_DIM
PROJ_W = C_MISC + LANES
M_IXK = 0
M_B = IDX_DIM
M_A = M_B + DN_HEADS
M_W = M_A + DN_HEADS

ROT = SA_HEAD_DIM // ROPE_FRACTION
HALF = ROT // 2

VMEM_LIMIT = 56 * 1024 * 1024
TOKEN_TILE = 1024
PROJ_COLS = 512
FF_COLS = 256


def _const_spec(shape):
    nd = len(shape)
    return pl.BlockSpec(shape, lambda *_: (0,) * nd, pipeline_mode=pl.Buffered(1))


def _rms(x, g):
    return x * lax.rsqrt(jnp.mean(x * x, axis=-1, keepdims=True) + RMS_EPS) * g


def _mm(a, b):
    return jnp.dot(a.astype(BF16), b.astype(BF16), preferred_element_type=F32)


def _mm_nt(a, b):
    return lax.dot_general(a.astype(BF16), b.astype(BF16), (((1,), (1,)), ((), ())),
                           preferred_element_type=F32)


def _sigmoid(x):
    return 1.0 / (1.0 + jnp.exp(-x))


def _silu(x):
    return x * _sigmoid(x)


def _softplus(x):
    return jnp.maximum(x, 0.0) + jnp.log1p(jnp.exp(-jnp.abs(x)))


def _in_proj_kernel(x_ref, g_ref, w_ref, o_ref):
    hb = _rms(x_ref[...], g_ref[...]).astype(BF16)
    for n in range(0, PROJ_W, PROJ_COLS):
        o_ref[:, n:n + PROJ_COLS] = jnp.dot(hb, w_ref[:, n:n + PROJ_COLS],
                                            preferred_element_type=F32)


def _in_proj(x2d, gain, w_perm, tm):
    t = x2d.shape[0]
    return pl.pallas_call(
        _in_proj_kernel,
        grid=(t // tm,),
        in_specs=[pl.BlockSpec((tm, D_MODEL), lambda i: (i, 0)),
                  _const_spec((1, D_MODEL)),
                  _const_spec((D_MODEL, PROJ_W))],
        out_specs=pl.BlockSpec((tm, PROJ_W), lambda i: (i, 0)),
        out_shape=jax.ShapeDtypeStruct((t, PROJ_W), F32),
        compiler_params=pltpu.CompilerParams(dimension_semantics=("arbitrary",),
                                             vmem_limit_bytes=VMEM_LIMIT),
        name="in_proj",
    )(x2d, gain, w_perm)


def _dn_kernel(q_ref, k_ref, v_ref, z_ref, misc_ref, cwq_ref, cwk_ref, cwv_ref, alog_ref, dtb_ref,
               ng_ref, o_ref,
               pad_s, cq_s, ck_s, cv_s, brow_s, grow_s, u_s, w_s, a_s, qg_s, c_s, n_s, el_s):
    seq = q_ref.shape[1]
    nc = seq // CHUNK
    h = pl.program_id(1)
    row = lax.broadcasted_iota(I32, (CHUNK, CHUNK), 0)
    col = lax.broadcasted_iota(I32, (CHUNK, CHUNK), 1)

    def conv(chunks):
        for n in chunks:
            for src, cw_ref, dst, kind in ((q_ref, cwq_ref, cq_s, "q"), (k_ref, cwk_ref, ck_s, "k"),
                                           (v_ref, cwv_ref, cv_s, "v")):
                if n == 0:
                    pad_s[0:SUBLANES, :] = jnp.zeros((SUBLANES, LANES), F32)
                    pad_s[SUBLANES:SUBLANES + CHUNK, :] = src[0, 0:CHUNK, :]
                    tap = lambda j: pad_s[SUBLANES - j:SUBLANES - j + CHUNK, :]
                else:
                    tap = lambda j, src=src: src[0, n * CHUNK - j:(n + 1) * CHUNK - j, :]
                y = cw_ref[DN_CONV - 1:DN_CONV, :] * tap(0)
                for j in range(1, DN_CONV):
                    y = y + cw_ref[DN_CONV - 1 - j:DN_CONV - j, :] * tap(j)
                y = _silu(y)
                if kind != "v":
                    y = y * lax.rsqrt(jnp.sum(y * y, axis=-1, keepdims=True) + 1e-6)
                if kind == "q":
                    y = y * (DN_DK ** -0.5)
                dst[n * CHUNK:(n + 1) * CHUNK, :] = y
            yield

    upper = jnp.where(row <= col, 1.0, 0.0).astype(F32)
    sub8 = lax.broadcasted_iota(I32, (SUBLANES, LANES), 0)
    for n in range(nc):
        mt = misc_ref[0, n * CHUNK:(n + 1) * CHUNK, :].T
        slab = mt[M_B:M_B + SUBLANES, :]
        beta8 = _sigmoid(slab)
        g8 = -jnp.exp(alog_ref[...]) * _softplus(slab + dtb_ref[...])
        gc8 = jnp.dot(g8, upper, preferred_element_type=F32, precision=lax.Precision.HIGHEST)
        b_row = jnp.sum(jnp.where(sub8 == h, beta8, 0.0), axis=0, keepdims=True)
        g_row = jnp.sum(jnp.where(sub8 == h + DN_HEADS, gc8, 0.0), axis=0, keepdims=True)
        brow_s[n] = jnp.broadcast_to(b_row, (SUBLANES, LANES))
        grow_s[n] = jnp.broadcast_to(g_row, (SUBLANES, LANES))

    eye = jnp.where(row == col, 1.0, 0.0).astype(F32)

    def prep_one(n):
        rows = slice(n * CHUNK, (n + 1) * CHUNK)
        q = cq_s[rows, :]
        k = ck_s[rows, :]
        v = cv_s[rows, :]
        g_r = jnp.broadcast_to(grow_s[n][0:1, :], (CHUNK, CHUNK))
        g_c = g_r.T
        b_c = jnp.broadcast_to(brow_s[n][0:1, :], (CHUNK, CHUNK)).T
        g_last = jnp.broadcast_to(g_c[CHUNK - 1:CHUNK, :], (CHUNK, CHUNK))
        decay = jnp.where(row >= col, jnp.exp(jnp.minimum(g_c - g_r, 0.0)), 0.0)
        kb = k * b_c
        kq = _mm_nt(jnp.concatenate([kb, q], axis=0), k)
        yield
        lmat = jnp.where(row > col, kq[:CHUNK] * decay, 0.0)
        a = jnp.where(row >= col, kq[CHUNK:] * decay, 0.0)
        same = lambda b: (row // b) == (col // b)
        l0 = jnp.where(same(DN_BASE), lmat, 0.0)
        x = eye - l0
        m = _mm(l0, l0)
        yield
        power = 2
        while 2 * power < DN_BASE:
            xm = _mm(jnp.concatenate([x, m], axis=0), m)
            yield
            x = x + xm[:CHUNK]
            m = xm[CHUNK:]
            power *= 2
        x = x + _mm(x, m)
        yield
        blk = DN_BASE
        while blk < CHUNK:
            below = jnp.where(same(2 * blk) & jnp.logical_not(same(blk)), lmat, 0.0)
            y = _mm(below, x)
            yield
            x = x - _mm(x, y)
            yield
            blk *= 2
        eg = jnp.exp(g_c)
        sol = _mm(x, jnp.concatenate([v * b_c, kb * eg], axis=1))
        yield
        cn = _mm((k * jnp.exp(g_last - g_c)).T, sol)
        yield
        u_s[n] = sol[:, :DN_DV]
        w_s[n] = sol[:, DN_DV:]
        a_s[n] = a
        qg_s[n] = q * eg
        c_s[n] = cn[:, :DN_DV]
        n_s[n] = cn[:, DN_DV:]
        el_s[n] = jnp.exp(g_last)[0:SUBLANES, :]

    carried = [jnp.zeros((DN_DK, DN_DV), F32)]

    def recur(chunks):
        for n in chunks:
            state = carried[0]
            el = jnp.broadcast_to(el_s[n][0:1, :], (CHUNK, CHUNK))
            nws = _mm(jnp.concatenate([n_s[n], w_s[n], qg_s[n]], axis=0), state)
            yield
            carried[0] = state * el + c_s[n] - nws[:CHUNK]
            o = nws[2 * CHUNK:] + _mm(a_s[n], u_s[n] - nws[CHUNK:2 * CHUNK])
            yield
            rows = slice(n * CHUNK, (n + 1) * CHUNK)
            o_ref[0, rows, :] = (_rms(o, ng_ref[...]) * _silu(z_ref[0, rows, :])).astype(o_ref.dtype)

    def weave(*gens):
        live = list(gens)
        while live:
            live = [g for g in live if next(g, True) is None]

    parts = [range(i * nc // DN_PARTS, (i + 1) * nc // DN_PARTS) for i in range(DN_PARTS)]
    for i in range(DN_PARTS + 2):
        stage = []
        if i < DN_PARTS:
            stage.append(conv(parts[i]))
        if 1 <= i <= DN_PARTS:
            stage.extend(prep_one(n) for n in parts[i - 1])
        if i >= 2:
            stage.append(recur(parts[i - 2]))
        weave(*stage)


def _deltanet(proj, conv_w, alog8, dtb8, norm_g):
    bsz, seq, _ = proj.shape
    nc = seq // CHUNK
    assert seq % (DN_PARTS * CHUNK) == 0
    qb, kb, vb, zb = (C_QKV // LANES, (C_QKV + DN_QK) // LANES, (C_QKV + 2 * DN_QK) // LANES,
                      C_Z // LANES)
    col = lambda off: pl.BlockSpec((1, seq, LANES), lambda b, h, off=off: (b, 0, off + h))
    cw = lambda off: pl.BlockSpec((DN_CONV, LANES), lambda b, h, off=off: (0, off + h))
    mat = lambda: pltpu.VMEM((nc, CHUNK, CHUNK), F32)
    return pl.pallas_call(
        _dn_kernel,
        grid=(bsz, DN_HEADS),
        in_specs=[col(qb), col(kb), col(vb), col(zb),
                  pl.BlockSpec((1, seq, LANES), lambda b, h: (b, 0, C_MISC // LANES)),
                  cw(qb), cw(kb), cw(vb),
                  _const_spec((SUBLANES, LANES)), _const_spec((SUBLANES, LANES)),
                  _const_spec((1, DN_DV))],
        out_specs=pl.BlockSpec((1, seq, LANES), lambda b, h: (b, 0, h)),
        out_shape=jax.ShapeDtypeStruct((bsz, seq, DN_V), BF16),
        scratch_shapes=[pltpu.VMEM((SUBLANES + CHUNK, LANES), F32),
                        pltpu.VMEM((seq, LANES), F32), pltpu.VMEM((seq, LANES), F32),
                        pltpu.VMEM((seq, LANES), F32),
                        pltpu.VMEM((nc, SUBLANES, LANES), F32), pltpu.VMEM((nc, SUBLANES, LANES), F32),
                        mat(), mat(), mat(), mat(), mat(), mat(),
                        pltpu.VMEM((nc, SUBLANES, LANES), F32)],
        compiler_params=pltpu.CompilerParams(dimension_semantics=("arbitrary", "arbitrary"),
                                             vmem_limit_bytes=VMEM_LIMIT),
        name="deltanet",
    )(proj, proj, proj, proj, proj, conv_w, conv_w, conv_w, alog8, dtb8, norm_g)


def _even_odd(n):
    return list(range(0, n, 2)) + list(range(1, n, 2))


def _rope(x, c, s, s1, s2):
    w = x.shape[-1]
    return x * c + pltpu.roll(x, w - HALF, 1) * (s * s1) + pltpu.roll(x, HALF, 1) * (s * s2)


def _dsa_kernel(topk, q_ref, qi_ref, misct_ref, kv_ref, miscf_ref, pos_ref, qg_ref, kg_ref, rp_ref,
                hsum_ref, o_ref,
                cos_s, sin_s, ka_s, kb_s, vt_s, kia_s, kib_s, q2_s, qi2_s, sc_s, bias_s, lg_s, pa_s,
                pb_s, l8_s, acc_s):
    seq = kv_ref.shape[1]
    nc = seq // CHUNK
    qb = pl.program_id(1)
    npair = qb + 1
    r0 = pl.multiple_of(qb * QBLK, QBLK)
    col = lax.broadcasted_iota(I32, (CHUNK, CHUNK), 1)
    s1 = rp_ref[1:2, :]
    s2 = rp_ref[2:3, :]

    @pl.when(qb == 0)
    def _():
        ang = pos_ref[0].astype(F32) * rp_ref[0:1, :]
        packed = (jnp.cos(ang), jnp.sin(ang))
        lane = lax.broadcasted_iota(I32, ang.shape, 1)
        second = (lane >= SA_HEAD_DIM) & (lane < SA_HEAD_DIM + ROT)
        per_row = LANES // ROT
        for j in range(per_row):
            for tab, dst, rest in zip(packed, (cos_s, sin_s), (1.0, 0.0)):
                mine = tab if j == 0 else pltpu.roll(tab, LANES - j * ROT, 1)
                full = jnp.where(lane < ROT, mine,
                                 jnp.where(second, pltpu.roll(mine, SA_HEAD_DIM, 1), rest))
                dst[pl.ds(j, seq // per_row, stride=per_row), :] = full
        for n in range(nc):
            rows = slice(n * CHUNK, (n + 1) * CHUNK)
            c = cos_s[rows, :]
            s = sin_s[rows, :]
            ck = jnp.where(col < ROT, c, 1.0)
            sk = jnp.where(col < ROT, s, 0.0)
            kv = kv_ref[0, rows, :]
            ss = jnp.dot((kv * kv).astype(BF16), hsum_ref[0:LANES, 0:LANES],
                         preferred_element_type=F32)
            r = lax.rsqrt(ss * (1.0 / SA_HEAD_DIM) + RMS_EPS)
            y = jnp.where(col < SA_HEAD_DIM, kv * r * kg_ref[...], kv)
            y = _rope(y, ck, sk, s1, s2)
            ka = jnp.where(col < SA_HEAD_DIM, y, 0.0)
            ka_s[rows, :] = ka.astype(BF16)
            kb_s[rows, :] = pltpu.roll(ka, SA_HEAD_DIM, 1).astype(BF16)
            half = (n % 2) * CHUNK
            vt_s[n // 2, :, half:half + CHUNK] = y.T[SA_HEAD_DIM:, :].astype(BF16)
            yi = _rope(miscf_ref[0, rows, :], ck, sk, s1, s2)
            kia = jnp.where(col < IDX_DIM, yi, 0.0)
            kia_s[rows, :] = kia.astype(BF16)
            kib_s[rows, :] = pltpu.roll(kia, IDX_DIM, 1).astype(BF16)

    c1 = cos_s[pl.ds(r0, QBLK), :]
    sn1 = sin_s[pl.ds(r0, QBLK), :]
    rep = lambda a, k: jnp.concatenate([a] * k, axis=1)
    nq = SA_HEADS * SA_HEAD_DIM // LANES
    xq = q_ref[0]
    yq = _rope(xq * qg_ref[...], rep(c1, nq), rep(sn1, nq), rep(s1, nq), rep(s2, nq))
    ss = jnp.dot((xq * xq).astype(BF16), hsum_ref[...], preferred_element_type=F32)
    r = lax.rsqrt(ss * (1.0 / SA_HEAD_DIM) + RMS_EPS)
    qn = (yq * (r * (SA_HEAD_DIM ** -0.5 * LOG2E))).astype(BF16)
    for pr in range(nq):
        q2_s[pr * QBLK:(pr + 1) * QBLK, :] = qn[:, pr * LANES:(pr + 1) * LANES]
    ni = IDX_HEADS * IDX_DIM // LANES
    yi = _rope(qi_ref[0], rep(c1, ni), rep(sn1, ni), rep(s1, ni), rep(s2, ni)).astype(BF16)
    for pr in range(ni):
        qi2_s[pr * QBLK:(pr + 1) * QBLK, :] = yi[:, pr * LANES:(pr + 1) * LANES]
    mt = misct_ref[0].T
    wscale = (IDX_HEADS ** -0.5) * (IDX_DIM ** -0.5)
    wrow = jnp.concatenate([mt[M_W + hh:M_W + hh + 1, :] for hh in _even_odd(IDX_HEADS)],
                           axis=1) * wscale

    krow = lax.broadcasted_iota(I32, (KPAIR, QBLK), 0)
    qpos = r0 + lax.broadcasted_iota(I32, (KPAIR, QBLK), 1)
    big = -NEG_INF

    def fold8(x, op):
        parts = [x[i:i + SUBLANES] for i in range(0, x.shape[0], SUBLANES)]
        while len(parts) > 1:
            parts = [op(parts[i], parts[i + 1]) for i in range(0, len(parts), 2)]
        return parts[0]

    def idx_body(c, carry):
        k0 = pl.multiple_of(c * KPAIR, KPAIR)
        terms = []
        for keys_s in (kia_s, kib_s):
            keys = keys_s[pl.ds(k0, KPAIR), :]
            for pr in range(IDX_HEADS // 2):
                lt = _mm_nt(keys, qi2_s[pr * QBLK:(pr + 1) * QBLK, :])
                j = len(terms)
                terms.append(jnp.maximum(lt, 0.0) * wrow[:, j * QBLK:(j + 1) * QBLK])
        acc = (terms[0] + terms[1]) + (terms[2] + terms[3])
        causal = k0 + krow <= qpos
        sc = jnp.where(causal, acc, NEG_INF)
        sc_s[c] = sc
        lo8, hi8 = carry
        lo8 = jnp.minimum(lo8, fold8(jnp.where(causal, acc, big), jnp.minimum))
        hi8 = jnp.maximum(hi8, fold8(sc, jnp.maximum))
        return lo8, hi8

    lo8, hi8 = lax.fori_loop(0, npair, idx_body, (jnp.full((SUBLANES, QBLK), big, F32),
                                                  jnp.full((SUBLANES, QBLK), NEG_INF, F32)))
    lo = jnp.min(lo8, axis=0, keepdims=True)
    hi = jnp.max(hi8, axis=0, keepdims=True)

    def count(pred):
        def body(c, acc):
            return acc + fold8(jnp.where(pred(sc_s[c]), 1.0, 0.0), jnp.add)
        acc = lax.fori_loop(0, npair, body, jnp.zeros((SUBLANES, QBLK), F32))
        return jnp.sum(acc, axis=0, keepdims=True)

    def max_where(pred):
        def body(c, acc):
            s = sc_s[c]
            return jnp.maximum(acc, fold8(jnp.where(pred(s), s, NEG_INF), jnp.maximum))
        acc = lax.fori_loop(0, npair, body, jnp.full((SUBLANES, QBLK), NEG_INF, F32))
        return jnp.max(acc, axis=0, keepdims=True)

    kf = jnp.minimum(float(topk), (qpos[0:1, :] + 1).astype(F32))

    def bisect(_, bounds):
        lo, hi = bounds
        mid = lo + 0.5 * (hi - lo)
        above = count(lambda s: s > mid) >= kf
        return jnp.where(above, mid, lo), jnp.where(above, hi, mid)

    lo, hi = lax.fori_loop(0, BISECT_STEPS, bisect, (lo, hi))

    thr = max_where(lambda s: s <= hi)
    cge = count(lambda s: s >= thr)

    def short(cge):
        return jnp.max(jnp.where(cge < kf, 1.0, 0.0)) > 0.5

    def peel(state):
        thr, cge, _ = state
        nxt = max_where(lambda s: s < thr)
        cnx = count(lambda s: s >= nxt)
        step = cge < kf
        thr = jnp.where(step, nxt, thr)
        cge = jnp.where(step, cnx, cge)
        return thr, cge, short(cge)

    thr, cge, _ = lax.while_loop(lambda st: st[2], peel, (thr, cge, short(cge)))
    need = kf - count(lambda s: s > thr)

    lower = jnp.where(lax.broadcasted_iota(I32, (KPAIR, KPAIR), 0)
                      >= lax.broadcasted_iota(I32, (KPAIR, KPAIR), 1), 1.0, 0.0).astype(BF16)

    tile_heads = lambda b: jnp.concatenate([b] * SA_HEADS, axis=1)

    def sel_body(c, carry):
        off, m8 = carry
        k0 = pl.multiple_of(c * KPAIR, KPAIR)
        s = sc_s[c]
        eqf = jnp.where(s == thr, 1.0, 0.0)
        pref = jnp.dot(lower, eqf.astype(BF16), preferred_element_type=F32)
        tie = jnp.where(pref + off <= need, eqf, 0.0)
        keep = jnp.where(k0 + krow <= qpos, jnp.where(s > thr, 1.0, tie), 0.0)
        bias = jnp.where(keep > 0.5, 0.0, NEG_INF)
        bias_s[c] = bias
        tops = []
        for keys_s in (ka_s, kb_s):
            keys = keys_s[pl.ds(k0, KPAIR), :]
            for pr in range(SA_HEADS // 2):
                lg = _mm_nt(keys, q2_s[pr * QBLK:(pr + 1) * QBLK, :])
                lg_s[c, :, len(tops) * QBLK:(len(tops) + 1) * QBLK] = lg
                tops.append(fold8(lg + bias, jnp.maximum))
        m8 = jnp.maximum(m8, jnp.concatenate(tops, axis=1))
        return off + pref[KPAIR - 1:KPAIR, :], m8

    _, m8 = lax.fori_loop(0, npair, sel_body,
                          (jnp.zeros((1, QBLK), F32),
                           jnp.full((SUBLANES, SA_HEADS * QBLK), NEG_INF, F32)))
    mrow = jnp.max(m8, axis=0, keepdims=True)

    def probs(c, dst):
        slab = 2 * SUBLANES
        l8 = l8_s[...]
        for r in range(0, KPAIR, slab):
            p = jnp.exp2(lg_s[c, r:r + slab, :] + tile_heads(bias_s[c, r:r + slab, :]) - mrow)
            dst[r:r + slab, :] = p.astype(BF16)
            l8 = l8 + (p[:SUBLANES] + p[SUBLANES:])
        l8_s[...] = l8

    def pv_acc(c, src):
        acc_s[...] += jnp.dot(vt_s[c], src[...], preferred_element_type=F32)

    acc_s[...] = jnp.zeros(acc_s.shape, F32)
    l8_s[...] = jnp.zeros(l8_s.shape, F32)
    probs(0, pa_s)

    def p2(i, carry):
        c = 2 * i + 1
        pv_acc(c - 1, pa_s)
        probs(c, pb_s)

        @pl.when(c + 1 < npair)
        def _():
            pv_acc(c, pb_s)
            probs(c + 1, pa_s)
        return carry

    lax.fori_loop(0, npair // 2, p2, 0)
    last = npair - 1

    @pl.when(last % 2 == 0)
    def _():
        pv_acc(last, pa_s)

    @pl.when(last % 2 == 1)
    def _():
        pv_acc(last, pb_s)

    out_t = acc_s[...] / jnp.sum(l8_s[...], axis=0, keepdims=True)
    order = _even_odd(SA_HEADS)
    for pr in range(SA_HEADS // 2):
        ev, od = order.index(2 * pr), order.index(2 * pr + 1)
        two = jnp.concatenate([out_t[:, ev * QBLK:(ev + 1) * QBLK],
                               out_t[:, od * QBLK:(od + 1) * QBLK]], axis=0)
        o_ref[0, :, pr * LANES:(pr + 1) * LANES] = two.T.astype(o_ref.dtype)


def _dsa(proj, pos3, qg_t, kg_p, rope_pat, topk):
    bsz, seq, _ = proj.shape
    nc = seq // CHUNK
    qw = SA_HEADS * SA_HEAD_DIM
    iw = IDX_HEADS * IDX_DIM
    assert seq % QBLK == 0 and QBLK == KPAIR
    npairs = seq // KPAIR
    pairs = lambda: pltpu.VMEM((npairs, KPAIR, QBLK), F32)
    head_of = jnp.arange(qw) // SA_HEAD_DIM
    head_sum = (head_of[:, None] == head_of[None, :]).astype(BF16)
    return pl.pallas_call(
        functools.partial(_dsa_kernel, topk),
        grid=(bsz, seq // QBLK),
        in_specs=[pl.BlockSpec((1, QBLK, qw), lambda b, j: (b, j, C_SAQ // qw)),
                  pl.BlockSpec((1, QBLK, iw), lambda b, j: (b, j, C_IXQ // iw)),
                  pl.BlockSpec((1, QBLK, LANES), lambda b, j: (b, j, C_MISC // LANES)),
                  pl.BlockSpec((1, seq, LANES), lambda b, j: (b, 0, C_KV // LANES)),
                  pl.BlockSpec((1, seq, LANES), lambda b, j: (b, 0, C_MISC // LANES)),
                  pl.BlockSpec((1, seq * ROT // LANES, LANES), lambda b, j: (b, 0, 0)),
                  _const_spec((1, qw)), _const_spec((1, LANES)), _const_spec((SUBLANES, LANES)),
                  _const_spec((qw, qw))],
        out_specs=pl.BlockSpec((1, QBLK, qw), lambda b, j: (b, j, 0)),
        out_shape=jax.ShapeDtypeStruct((bsz, seq, qw), BF16),
        scratch_shapes=[pltpu.VMEM((seq, LANES), F32), pltpu.VMEM((seq, LANES), F32),
                        pltpu.VMEM((seq, LANES), BF16), pltpu.VMEM((seq, LANES), BF16),
                        pltpu.VMEM((npairs, SA_HEAD_DIM, KPAIR), BF16),
                        pltpu.VMEM((seq, LANES), BF16), pltpu.VMEM((seq, LANES), BF16),
                        pltpu.VMEM((qw // LANES * QBLK, LANES), BF16),
                        pltpu.VMEM((iw // LANES * QBLK, LANES), BF16),
                        pairs(), pairs(),
                        pltpu.VMEM((npairs, KPAIR, SA_HEADS * QBLK), F32),
                        pltpu.VMEM((KPAIR, SA_HEADS * QBLK), BF16),
                        pltpu.VMEM((KPAIR, SA_HEADS * QBLK), BF16),
                        pltpu.VMEM((SUBLANES, SA_HEADS * QBLK), F32),
                        pltpu.VMEM((SA_HEAD_DIM, SA_HEADS * QBLK), F32)],
        compiler_params=pltpu.CompilerParams(dimension_semantics=("arbitrary", "arbitrary"),
                                             vmem_limit_bytes=VMEM_LIMIT),
        name="dsa",
    )(proj, proj, proj, proj, proj, pos3, qg_t, kg_p, rope_pat, head_sum)


def _merge_kernel(x_ref, oa_ref, ob_ref, g_ref, wg_ref, bg_ref, wa_ref, wb_ref, wo_ref, o_ref):
    x = x_ref[...]
    hb = _rms(x, g_ref[...]).astype(BF16)
    oa = oa_ref[...]
    ob = ob_ref[...]
    acc = x
    step = PROJ_COLS
    for n in range(0, D_MODEL, step):
        ga = _sigmoid(jnp.dot(hb, wg_ref[:, n:n + step], preferred_element_type=F32)
                      + bg_ref[:, n:n + step])
        gb = _sigmoid(jnp.dot(hb, wg_ref[:, D_MODEL + n:D_MODEL + n + step],
                              preferred_element_type=F32)
                      + bg_ref[:, D_MODEL + n:D_MODEL + n + step])
        ya = jnp.dot(oa, wa_ref[:, n:n + step], preferred_element_type=F32)
        yb = jnp.dot(ob, wb_ref[:, n:n + step], preferred_element_type=F32)
        merged = (ga * ya + gb * yb).astype(BF16)
        acc = acc + jnp.dot(merged, wo_ref[n:n + step, :], preferred_element_type=F32)
    o_ref[...] = acc


def _merge(x2d, oa2d, ob2d, gain, w_gate, b_gate, w_a, w_b, w_o, tm):
    t = x2d.shape[0]
    tok = lambda w: pl.BlockSpec((tm, w), lambda i: (i, 0))
    return pl.pallas_call(
        _merge_kernel,
        grid=(t // tm,),
        in_specs=[tok(D_MODEL), tok(DN_V), tok(SA_HEADS * SA_HEAD_DIM),
                  _const_spec((1, D_MODEL)), _const_spec((D_MODEL, 2 * D_MODEL)),
                  _const_spec((1, 2 * D_MODEL)), _const_spec((DN_V, D_MODEL)),
                  _const_spec((SA_HEADS * SA_HEAD_DIM, D_MODEL)), _const_spec((D_MODEL, D_MODEL))],
        out_specs=tok(D_MODEL),
        out_shape=jax.ShapeDtypeStruct((t, D_MODEL), F32),
        compiler_params=pltpu.CompilerParams(dimension_semantics=("arbitrary",),
                                             vmem_limit_bytes=VMEM_LIMIT),
        name="merge",
    )(x2d, oa2d, ob2d, gain, w_gate, b_gate, w_a, w_b, w_o)


def _ffn_kernel(x_ref, p_ref, g2_ref, wg_ref, wu_ref, wd_ref, g3_ref, wpg_ref, wp_ref, o_ref):
    x = x_ref[...]
    hb = _rms(x, g2_ref[...]).astype(BF16)
    acc = x
    step = FF_COLS
    for n in range(0, D_FF, step):
        gate = jnp.dot(hb, wg_ref[:, n:n + step], preferred_element_type=F32)
        up = jnp.dot(hb, wu_ref[:, n:n + step], preferred_element_type=F32)
        acc = acc + jnp.dot((_silu(gate) * up).astype(BF16), wd_ref[n:n + step, :],
                            preferred_element_type=F32)
    h3 = _rms(acc, g3_ref[...]).astype(BF16)
    pg = _sigmoid(jnp.dot(h3, wpg_ref[...], preferred_element_type=F32))
    pe = jnp.dot(p_ref[...].astype(BF16), wp_ref[...], preferred_element_type=F32)
    o_ref[...] = acc + pg * pe


def _ffn(x2d, p2d, g2, w_g, w_u, w_d, g3, w_pg, w_p, tm):
    t = x2d.shape[0]
    tok = lambda w: pl.BlockSpec((tm, w), lambda i: (i, 0))
    return pl.pallas_call(
        _ffn_kernel,
        grid=(t // tm,),
        in_specs=[tok(D_MODEL), tok(PLE_DIM), _const_spec((1, D_MODEL)),
                  _const_spec((D_MODEL, D_FF)), _const_spec((D_MODEL, D_FF)),
                  _const_spec((D_FF, D_MODEL)), _const_spec((1, D_MODEL)),
                  _const_spec((D_MODEL, D_MODEL)), _const_spec((PLE_DIM, D_MODEL))],
        out_specs=tok(D_MODEL),
        out_shape=jax.ShapeDtypeStruct((t, D_MODEL), F32),
        compiler_params=pltpu.CompilerParams(dimension_semantics=("arbitrary",),
                                             vmem_limit_bytes=VMEM_LIMIT),
        name="ffn_ple",
    )(x2d, p2d, g2, w_g, w_u, w_d, g3, w_pg, w_p)


def _regroup_w_in(w):
    sizes = (DN_CONV_CH, DN_V, DN_HEADS, DN_HEADS, SA_HEADS * SA_HEAD_DIM, SA_HEAD_DIM, SA_HEAD_DIM,
             IDX_HEADS * IDX_DIM, IDX_DIM, IDX_HEADS)
    parts, off = [], 0
    for s in sizes:
        parts.append(w[:, off:off + s])
        off += s
    qkv, z, b, a, saq, sak, sav, ixq, ixk, ixw = parts
    cols = [qkv, z, saq, ixq, sak, sav, ixk, b, a, ixw]
    used = sum(c.shape[1] for c in cols)
    cols.append(jnp.zeros((w.shape[0], PROJ_W - used), w.dtype))
    return jnp.concatenate(cols, axis=1)


def _rope_pattern():
    lane = jnp.arange(LANES) % SA_HEAD_DIM
    inv_freq = ROPE_THETA ** (-jnp.arange(0, ROT, 2, dtype=F32) / ROT)
    freq = inv_freq[jnp.arange(LANES) % HALF]
    s1 = jnp.where(lane < HALF, -1.0, 0.0)
    s2 = jnp.where((lane >= HALF) & (lane < ROT), 1.0, 0.0)
    pat = jnp.zeros((SUBLANES, LANES), F32)
    return pat.at[0].set(freq).at[1].set(s1).at[2].set(s2)


def _layer(x, p, positions, attn_norm, w_in, conv_w, a_log, dt_bias, dn_norm, q_norm, k_norm,
           w_branch_a, w_branch_b, w_gate, b_gate, w_o, ffn_norm, w_ffn_gate, w_ffn_up, w_ffn_down,
           ple_norm, w_ple_gate, w_ple):
    bsz, seq, _ = x.shape
    t = bsz * seq
    tm = TOKEN_TILE
    assert t % tm == 0
    topk = min(IDX_TOPK_MAX, seq // 4)
    row = lambda v: v.reshape(1, -1).astype(F32)
    x2d = x.reshape(t, D_MODEL)

    proj = _in_proj(x2d, row(attn_norm), _regroup_w_in(w_in).astype(BF16), tm)
    proj = proj.reshape(bsz, seq, PROJ_W)

    pad4 = jnp.zeros((DN_HEADS,), F32)
    per_head = lambda v: jnp.broadcast_to(jnp.concatenate([pad4, v.astype(F32)])[:, None],
                                          (SUBLANES, LANES))
    o_a = _deltanet(proj, conv_w.astype(F32), per_head(a_log), per_head(dt_bias), row(dn_norm))

    qg_t = jnp.tile(q_norm.astype(F32), SA_HEADS).reshape(1, -1)
    kg_p = jnp.concatenate([k_norm.astype(F32), jnp.ones((LANES - SA_HEAD_DIM,), F32)]).reshape(1, -1)
    per_row = LANES // ROT
    pos_packed = jnp.repeat(positions.astype(I32).reshape(bsz, seq // per_row, per_row), ROT, axis=2)
    o_b = _dsa(proj, pos_packed, qg_t, kg_p, _rope_pattern(), topk)

    x1 = _merge(x2d, o_a.reshape(t, DN_V), o_b.reshape(t, -1), row(attn_norm),
                w_gate.astype(BF16), row(b_gate), w_branch_a.astype(BF16), w_branch_b.astype(BF16),
                w_o.astype(BF16), tm)
    x2 = _ffn(x1, p.reshape(t, PLE_DIM), row(ffn_norm), w_ffn_gate.astype(BF16),
              w_ffn_up.astype(BF16), w_ffn_down.astype(BF16), row(ple_norm),
              w_ple_gate.astype(BF16), w_ple.astype(BF16), tm)
    return x2.reshape(bsz, seq, D_MODEL)


def kernel(x, p, positions, attn_norm, w_in, conv_w, a_log, dt_bias, dn_norm, q_norm, k_norm,
           w_branch_a, w_branch_b, w_gate, b_gate, w_o, ffn_norm, w_ffn_gate, w_ffn_up, w_ffn_down,
           ple_norm, w_ple_gate, w_ple):
    depth = w_in.shape[0]
    for i in range(depth):
        x = _layer(x, p[i], positions, attn_norm[i], w_in[i], conv_w[i], a_log[i], dt_bias[i],
                   dn_norm[i], q_norm[i], k_norm[i], w_branch_a[i], w_branch_b[i], w_gate[i],
                   b_gate[i], w_o[i], ffn_norm[i], w_ffn_gate[i], w_ffn_up[i], w_ffn_down[i],
                   ple_norm[i], w_ple_gate[i], w_ple[i])
    return x
```

```python
import functools

import jax
import jax.numpy as jnp
from jax import lax
from jax.experimental import pallas as pl
from jax.experimental.pallas import tpu as pltpu

F32 = jnp.float32
BF16 = jnp.bfloat16
I32 = jnp.int32

D_MODEL = 1024
PLE_DIM = 256
RMS_EPS = 1e-6
DN_HEADS = 4
DN_DK = 128
DN_DV = 128
DN_CONV = 4
DN_QK = DN_HEADS * DN_DK
DN_V = DN_HEADS * DN_DV
DN_CONV_CH = 2 * DN_QK + DN_V
SA_HEADS = 8
SA_HEAD_DIM = 64
IDX_HEADS = 4
IDX_DIM = 64
IDX_TOPK_MAX = 256
ROPE_THETA = 500000.0
ROPE_FRACTION = 4
NEG_INF = -1e30
D_FF = 2816
LOG2E = 1.4426950408889634

LANES = 128
SUBLANES = 8
CHUNK = 128
DN_BASE = 32
DN_PARTS = 2
KPAIR = 2 * CHUNK
QBLK = KPAIR
BISECT_STEPS = 19

C_QKV = 0
C_Z = DN_CONV_CH
C_SAQ = C_Z + DN_V
C_IXQ = C_SAQ + SA_HEADS * SA_HEAD_DIM
C_KV = C_IXQ + IDX_HEADS * IDX_DIM
C_MISC = C_KV + 2 * SA_HEAD_DIM
PROJ_W = C_MISC + LANES
M_IXK = 0
M_B = IDX_DIM
M_A = M_B + DN_HEADS
M_W = M_A + DN_HEADS

ROT = SA_HEAD_DIM // ROPE_FRACTION
HALF = ROT // 2

VMEM_LIMIT = 56 * 1024 * 1024
TOKEN_TILE = 1024
PROJ_COLS = 512
FF_COLS = 256


def _const_spec(shape):
    nd = len(shape)
    return pl.BlockSpec(shape, lambda *_: (0,) * nd, pipeline_mode=pl.Buffered(1))


def _rms(x, g):
    return x * lax.rsqrt(jnp.mean(x * x, axis=-1, keepdims=True) + RMS_EPS) * g


def _mm(a, b):
    return jnp.dot(a.astype(BF16), b.astype(BF16), preferred_element_type=F32)


def _mm_nt(a, b):
    return lax.dot_general(a.astype(BF16), b.astype(BF16), (((1,), (1,)), ((), ())),
                           preferred_element_type=F32)


def _weave(*gens):
    live = list(gens)
    while live:
        live = [g for g in live if next(g, True) is None]


def _sigmoid(x):
    return 1.0 / (1.0 + jnp.exp(-x))


def _silu(x):
    return x * _sigmoid(x)


def _softplus(x):
    return jnp.maximum(x, 0.0) + jnp.log1p(jnp.exp(-jnp.abs(x)))


def _in_proj_kernel(x_ref, g_ref, w_ref, o_ref):
    hb = _rms(x_ref[...], g_ref[...]).astype(BF16)
    for n in range(0, PROJ_W, PROJ_COLS):
        o_ref[:, n:n + PROJ_COLS] = jnp.dot(hb, w_ref[:, n:n + PROJ_COLS],
                                            preferred_element_type=F32)


def _in_proj(x2d, gain, w_perm, tm):
    t = x2d.shape[0]
    return pl.pallas_call(
        _in_proj_kernel,
        grid=(t // tm,),
        in_specs=[pl.BlockSpec((tm, D_MODEL), lambda i: (i, 0)),
                  _const_spec((1, D_MODEL)),
                  _const_spec((D_MODEL, PROJ_W))],
        out_specs=pl.BlockSpec((tm, PROJ_W), lambda i: (i, 0)),
        out_shape=jax.ShapeDtypeStruct((t, PROJ_W), F32),
        compiler_params=pltpu.CompilerParams(dimension_semantics=("arbitrary",),
                                             vmem_limit_bytes=VMEM_LIMIT),
        name="in_proj",
    )(x2d, gain, w_perm)


def _dn_kernel(q_ref, k_ref, v_ref, z_ref, misc_ref, cwq_ref, cwk_ref, cwv_ref, alog_ref, dtb_ref,
               ng_ref, o_ref,
               pad_s, cq_s, ck_s, cv_s, brow_s, grow_s, u_s, w_s, a_s, qg_s, c_s, n_s, el_s):
    seq = q_ref.shape[1]
    nc = seq // CHUNK
    h = pl.program_id(1)
    row = lax.broadcasted_iota(I32, (CHUNK, CHUNK), 0)
    col = lax.broadcasted_iota(I32, (CHUNK, CHUNK), 1)

    def conv(chunks):
        for n in chunks:
            for src, cw_ref, dst, kind in ((q_ref, cwq_ref, cq_s, "q"), (k_ref, cwk_ref, ck_s, "k"),
                                           (v_ref, cwv_ref, cv_s, "v")):
                if n == 0:
                    pad_s[0:SUBLANES, :] = jnp.zeros((SUBLANES, LANES), F32)
                    pad_s[SUBLANES:SUBLANES + CHUNK, :] = src[0, 0:CHUNK, :]
                    tap = lambda j: pad_s[SUBLANES - j:SUBLANES - j + CHUNK, :]
                else:
                    tap = lambda j, src=src: src[0, n * CHUNK - j:(n + 1) * CHUNK - j, :]
                y = cw_ref[DN_CONV - 1:DN_CONV, :] * tap(0)
                for j in range(1, DN_CONV):
                    y = y + cw_ref[DN_CONV - 1 - j:DN_CONV - j, :] * tap(j)
                y = _silu(y)
                if kind != "v":
                    y = y * lax.rsqrt(jnp.sum(y * y, axis=-1, keepdims=True) + 1e-6)
                if kind == "q":
                    y = y * (DN_DK ** -0.5)
                dst[n * CHUNK:(n + 1) * CHUNK, :] = y
            yield

    upper = jnp.where(row <= col, 1.0, 0.0).astype(F32)
    sub8 = lax.broadcasted_iota(I32, (SUBLANES, LANES), 0)
    for n in range(nc):
        mt = misc_ref[0, n * CHUNK:(n + 1) * CHUNK, :].T
        slab = mt[M_B:M_B + SUBLANES, :]
        beta8 = _sigmoid(slab)
        g8 = -jnp.exp(alog_ref[...]) * _softplus(slab + dtb_ref[...])
        gc8 = jnp.dot(g8, upper, preferred_element_type=F32, precision=lax.Precision.HIGHEST)
        b_row = jnp.sum(jnp.where(sub8 == h, beta8, 0.0), axis=0, keepdims=True)
        g_row = jnp.sum(jnp.where(sub8 == h + DN_HEADS, gc8, 0.0), axis=0, keepdims=True)
        brow_s[n] = jnp.broadcast_to(b_row, (SUBLANES, LANES))
        grow_s[n] = jnp.broadcast_to(g_row, (SUBLANES, LANES))

    eye = jnp.where(row == col, 1.0, 0.0).astype(F32)

    def prep_one(n):
        rows = slice(n * CHUNK, (n + 1) * CHUNK)
        q = cq_s[rows, :]
        k = ck_s[rows, :]
        v = cv_s[rows, :]
        g_r = jnp.broadcast_to(grow_s[n][0:1, :], (CHUNK, CHUNK))
        g_c = g_r.T
        b_c = jnp.broadcast_to(brow_s[n][0:1, :], (CHUNK, CHUNK)).T
        g_last = jnp.broadcast_to(g_c[CHUNK - 1:CHUNK, :], (CHUNK, CHUNK))
        decay = jnp.where(row >= col, jnp.exp(jnp.minimum(g_c - g_r, 0.0)), 0.0)
        kb = k * b_c
        kq = _mm_nt(jnp.concatenate([kb, q], axis=0), k)
        yield
        lmat = jnp.where(row > col, kq[:CHUNK] * decay, 0.0)
        a = jnp.where(row >= col, kq[CHUNK:] * decay, 0.0)
        same = lambda b: (row // b) == (col // b)
        l0 = jnp.where(same(DN_BASE), lmat, 0.0)
        x = eye - l0
        m = _mm(l0, l0)
        yield
        power = 2
        while 2 * power < DN_BASE:
            xm = _mm(jnp.concatenate([x, m], axis=0), m)
            yield
            x = x + xm[:CHUNK]
            m = xm[CHUNK:]
            power *= 2
        x = x + _mm(x, m)
        yield
        blk = DN_BASE
        while blk < CHUNK:
            below = jnp.where(same(2 * blk) & jnp.logical_not(same(blk)), lmat, 0.0)
            y = _mm(below, x)
            yield
            x = x - _mm(x, y)
            yield
            blk *= 2
        eg = jnp.exp(g_c)
        sol = _mm(x, jnp.concatenate([v * b_c, kb * eg], axis=1))
        yield
        cn = _mm((k * jnp.exp(g_last - g_c)).T, sol)
        yield
        u_s[n] = sol[:, :DN_DV]
        w_s[n] = sol[:, DN_DV:]
        a_s[n] = a
        qg_s[n] = q * eg
        c_s[n] = cn[:, :DN_DV]
        n_s[n] = cn[:, DN_DV:]
        el_s[n] = jnp.exp(g_last)[0:SUBLANES, :]

    carried = [jnp.zeros((DN_DK, DN_DV), F32)]

    def recur(chunks):
        for n in chunks:
            state = carried[0]
            el = jnp.broadcast_to(el_s[n][0:1, :], (CHUNK, CHUNK))
            nws = _mm(jnp.concatenate([n_s[n], w_s[n], qg_s[n]], axis=0), state)
            yield
            carried[0] = state * el + c_s[n] - nws[:CHUNK]
            o = nws[2 * CHUNK:] + _mm(a_s[n], u_s[n] - nws[CHUNK:2 * CHUNK])
            yield
            rows = slice(n * CHUNK, (n + 1) * CHUNK)
            o_ref[0, rows, :] = (_rms(o, ng_ref[...]) * _silu(z_ref[0, rows, :])).astype(o_ref.dtype)

    parts = [range(i * nc // DN_PARTS, (i + 1) * nc // DN_PARTS) for i in range(DN_PARTS)]
    for i in range(DN_PARTS + 2):
        stage = []
        if i < DN_PARTS:
            stage.append(conv(parts[i]))
        if 1 <= i <= DN_PARTS:
            stage.extend(prep_one(n) for n in parts[i - 1])
        if i >= 2:
            stage.append(recur(parts[i - 2]))
        _weave(*stage)


def _deltanet(proj, conv_w, alog8, dtb8, norm_g):
    bsz, seq, _ = proj.shape
    nc = seq // CHUNK
    assert seq % (DN_PARTS * CHUNK) == 0
    qb, kb, vb, zb = (C_QKV // LANES, (C_QKV + DN_QK) // LANES, (C_QKV + 2 * DN_QK) // LANES,
                      C_Z // LANES)
    col = lambda off: pl.BlockSpec((1, seq, LANES), lambda b, h, off=off: (b, 0, off + h))
    cw = lambda off: pl.BlockSpec((DN_CONV, LANES), lambda b, h, off=off: (0, off + h))
    mat = lambda: pltpu.VMEM((nc, CHUNK, CHUNK), F32)
    return pl.pallas_call(
        _dn_kernel,
        grid=(bsz, DN_HEADS),
        in_specs=[col(qb), col(kb), col(vb), col(zb),
                  pl.BlockSpec((1, seq, LANES), lambda b, h: (b, 0, C_MISC // LANES)),
                  cw(qb), cw(kb), cw(vb),
                  _const_spec((SUBLANES, LANES)), _const_spec((SUBLANES, LANES)),
                  _const_spec((1, DN_DV))],
        out_specs=pl.BlockSpec((1, seq, LANES), lambda b, h: (b, 0, h)),
        out_shape=jax.ShapeDtypeStruct((bsz, seq, DN_V), BF16),
        scratch_shapes=[pltpu.VMEM((SUBLANES + CHUNK, LANES), F32),
                        pltpu.VMEM((seq, LANES), F32), pltpu.VMEM((seq, LANES), F32),
                        pltpu.VMEM((seq, LANES), F32),
                        pltpu.VMEM((nc, SUBLANES, LANES), F32), pltpu.VMEM((nc, SUBLANES, LANES), F32),
                        mat(), mat(), mat(), mat(), mat(), mat(),
                        pltpu.VMEM((nc, SUBLANES, LANES), F32)],
        compiler_params=pltpu.CompilerParams(dimension_semantics=("arbitrary", "arbitrary"),
                                             vmem_limit_bytes=VMEM_LIMIT),
        name="deltanet",
    )(proj, proj, proj, proj, proj, conv_w, conv_w, conv_w, alog8, dtb8, norm_g)


def _even_odd(n):
    return list(range(0, n, 2)) + list(range(1, n, 2))


def _rope(x, c, s, s1, s2):
    w = x.shape[-1]
    return x * c + pltpu.roll(x, w - HALF, 1) * (s * s1) + pltpu.roll(x, HALF, 1) * (s * s2)


def _dsa_kernel(topk, q_ref, qi_ref, misct_ref, kv_ref, miscf_ref, pos_ref, qg_ref, kg_ref, rp_ref,
                hsum_ref, o_ref,
                cos_s, sin_s, ka_s, kb_s, vt_s, kia_s, kib_s, q2_s, qi2_s, sc_s, bias_s, lg_s, pa_s,
                pb_s, l8_s, acc_s):
    seq = kv_ref.shape[1]
    nc = seq // CHUNK
    qb = pl.program_id(1)
    npair = qb + 1
    r0 = pl.multiple_of(qb * QBLK, QBLK)
    col = lax.broadcasted_iota(I32, (CHUNK, CHUNK), 1)
    s1 = rp_ref[1:2, :]
    s2 = rp_ref[2:3, :]

    @pl.when(qb == 0)
    def _():
        ang = pos_ref[0].astype(F32) * rp_ref[0:1, :]
        packed = (jnp.cos(ang), jnp.sin(ang))
        lane = lax.broadcasted_iota(I32, ang.shape, 1)
        second = (lane >= SA_HEAD_DIM) & (lane < SA_HEAD_DIM + ROT)
        per_row = LANES // ROT
        for j in range(per_row):
            for tab, dst, rest in zip(packed, (cos_s, sin_s), (1.0, 0.0)):
                mine = tab if j == 0 else pltpu.roll(tab, LANES - j * ROT, 1)
                full = jnp.where(lane < ROT, mine,
                                 jnp.where(second, pltpu.roll(mine, SA_HEAD_DIM, 1), rest))
                dst[pl.ds(j, seq // per_row, stride=per_row), :] = full
        for n in range(nc):
            rows = slice(n * CHUNK, (n + 1) * CHUNK)
            c = cos_s[rows, :]
            s = sin_s[rows, :]
            ck = jnp.where(col < ROT, c, 1.0)
            sk = jnp.where(col < ROT, s, 0.0)
            kv = kv_ref[0, rows, :]
            ss = jnp.dot((kv * kv).astype(BF16), hsum_ref[0:LANES, 0:LANES],
                         preferred_element_type=F32)
            r = lax.rsqrt(ss * (1.0 / SA_HEAD_DIM) + RMS_EPS)
            y = jnp.where(col < SA_HEAD_DIM, kv * r * kg_ref[...], kv)
            y = _rope(y, ck, sk, s1, s2)
            ka = jnp.where(col < SA_HEAD_DIM, y, 0.0)
            ka_s[rows, :] = ka.astype(BF16)
            kb_s[rows, :] = pltpu.roll(ka, SA_HEAD_DIM, 1).astype(BF16)
            half = (n % 2) * CHUNK
            vt_s[n // 2, :, half:half + CHUNK] = y.T[SA_HEAD_DIM:, :].astype(BF16)
            yi = _rope(miscf_ref[0, rows, :], ck, sk, s1, s2)
            kia = jnp.where(col < IDX_DIM, yi, 0.0)
            kia_s[rows, :] = kia.astype(BF16)
            kib_s[rows, :] = pltpu.roll(kia, IDX_DIM, 1).astype(BF16)

    c1 = cos_s[pl.ds(r0, QBLK), :]
    sn1 = sin_s[pl.ds(r0, QBLK), :]
    rep = lambda a, k: jnp.concatenate([a] * k, axis=1)
    nq = SA_HEADS * SA_HEAD_DIM // LANES
    xq = q_ref[0]
    yq = _rope(xq * qg_ref[...], rep(c1, nq), rep(sn1, nq), rep(s1, nq), rep(s2, nq))
    ss = jnp.dot((xq * xq).astype(BF16), hsum_ref[...], preferred_element_type=F32)
    r = lax.rsqrt(ss * (1.0 / SA_HEAD_DIM) + RMS_EPS)
    qn = (yq * (r * (SA_HEAD_DIM ** -0.5 * LOG2E))).astype(BF16)
    for pr in range(nq):
        q2_s[pr * QBLK:(pr + 1) * QBLK, :] = qn[:, pr * LANES:(pr + 1) * LANES]
    ni = IDX_HEADS * IDX_DIM // LANES
    yi = _rope(qi_ref[0], rep(c1, ni), rep(sn1, ni), rep(s1, ni), rep(s2, ni)).astype(BF16)
    for pr in range(ni):
        qi2_s[pr * QBLK:(pr + 1) * QBLK, :] = yi[:, pr * LANES:(pr + 1) * LANES]
    mt = misct_ref[0].T
    wscale = (IDX_HEADS ** -0.5) * (IDX_DIM ** -0.5)
    wrow = jnp.concatenate([mt[M_W + hh:M_W + hh + 1, :] for hh in _even_odd(IDX_HEADS)],
                           axis=1) * wscale

    krow = lax.broadcasted_iota(I32, (KPAIR, QBLK), 0)
    qpos = r0 + lax.broadcasted_iota(I32, (KPAIR, QBLK), 1)
    big = -NEG_INF

    def fold8(x, op):
        parts = [x[i:i + SUBLANES] for i in range(0, x.shape[0], SUBLANES)]
        while len(parts) > 1:
            parts = [op(parts[i], parts[i + 1]) for i in range(0, len(parts), 2)]
        return parts[0]

    def idx_pair(c, bounds):
        k0 = pl.multiple_of(c * KPAIR, KPAIR)
        terms = []
        for keys_s in (kia_s, kib_s):
            keys = keys_s[pl.ds(k0, KPAIR), :]
            for pr in range(IDX_HEADS // 2):
                lt = _mm_nt(keys, qi2_s[pr * QBLK:(pr + 1) * QBLK, :])
                yield
                j = len(terms)
                terms.append(jnp.maximum(lt, 0.0) * wrow[:, j * QBLK:(j + 1) * QBLK])
        acc = (terms[0] + terms[1]) + (terms[2] + terms[3])
        causal = k0 + krow <= qpos
        sc = jnp.where(causal, acc, NEG_INF)
        sc_s[c] = sc
        bounds.append((fold8(jnp.where(causal, acc, big), jnp.minimum), fold8(sc, jnp.maximum)))

    def idx_pairs(pairs, carry):
        bounds = []
        _weave(*[idx_pair(c, bounds) for c in pairs])
        lo8, hi8 = carry
        for lo_c, hi_c in bounds:
            lo8 = jnp.minimum(lo8, lo_c)
            hi8 = jnp.maximum(hi8, hi_c)
        return lo8, hi8

    nsteps = npair // 2
    odd = npair % 2 == 1
    lo8, hi8 = lax.fori_loop(0, nsteps, lambda i, carry: idx_pairs((2 * i, 2 * i + 1), carry),
                             (jnp.full((SUBLANES, QBLK), big, F32),
                              jnp.full((SUBLANES, QBLK), NEG_INF, F32)))
    lo8, hi8 = lax.cond(odd, lambda carry: idx_pairs((npair - 1,), carry), lambda carry: carry,
                        (lo8, hi8))
    lo = jnp.min(lo8, axis=0, keepdims=True)
    hi = jnp.max(hi8, axis=0, keepdims=True)

    def count(pred):
        def body(c, acc):
            return acc + fold8(jnp.where(pred(sc_s[c]), 1.0, 0.0), jnp.add)
        acc = lax.fori_loop(0, npair, body, jnp.zeros((SUBLANES, QBLK), F32))
        return jnp.sum(acc, axis=0, keepdims=True)

    def max_where(pred):
        def body(c, acc):
            s = sc_s[c]
            return jnp.maximum(acc, fold8(jnp.where(pred(s), s, NEG_INF), jnp.maximum))
        acc = lax.fori_loop(0, npair, body, jnp.full((SUBLANES, QBLK), NEG_INF, F32))
        return jnp.max(acc, axis=0, keepdims=True)

    kf = jnp.minimum(float(topk), (qpos[0:1, :] + 1).astype(F32))

    def bisect(_, bounds):
        lo, hi = bounds
        mid = lo + 0.5 * (hi - lo)
        above = count(lambda s: s > mid) >= kf
        return jnp.where(above, mid, lo), jnp.where(above, hi, mid)

    lo, hi = lax.fori_loop(0, BISECT_STEPS, bisect, (lo, hi))

    thr = max_where(lambda s: s <= hi)
    cge = count(lambda s: s >= thr)

    def short(cge):
        return jnp.max(jnp.where(cge < kf, 1.0, 0.0)) > 0.5

    def peel(state):
        thr, cge, _ = state
        nxt = max_where(lambda s: s < thr)
        cnx = count(lambda s: s >= nxt)
        step = cge < kf
        thr = jnp.where(step, nxt, thr)
        cge = jnp.where(step, cnx, cge)
        return thr, cge, short(cge)

    thr, cge, _ = lax.while_loop(lambda st: st[2], peel, (thr, cge, short(cge)))
    need = kf - count(lambda s: s > thr)

    lower = jnp.where(lax.broadcasted_iota(I32, (KPAIR, KPAIR), 0)
                      >= lax.broadcasted_iota(I32, (KPAIR, KPAIR), 1), 1.0, 0.0).astype(BF16)

    tile_heads = lambda b: jnp.concatenate([b] * SA_HEADS, axis=1)

    def sel_pair(c, shared):
        k0 = pl.multiple_of(c * KPAIR, KPAIR)
        s = sc_s[c]
        eqf = jnp.where(s == thr, 1.0, 0.0)
        pref = jnp.dot(lower, eqf.astype(BF16), preferred_element_type=F32)
        yield
        off = shared["off"]
        shared["off"] = off + pref[KPAIR - 1:KPAIR, :]
        tie = jnp.where(pref + off <= need, eqf, 0.0)
        keep = jnp.where(k0 + krow <= qpos, jnp.where(s > thr, 1.0, tie), 0.0)
        bias = jnp.where(keep > 0.5, 0.0, NEG_INF)
        bias_s[c] = bias
        tops = []
        for keys_s in (ka_s, kb_s):
            keys = keys_s[pl.ds(k0, KPAIR), :]
            for pr in range(SA_HEADS // 2):
                lg = _mm_nt(keys, q2_s[pr * QBLK:(pr + 1) * QBLK, :])
                yield
                lg_s[c, :, len(tops) * QBLK:(len(tops) + 1) * QBLK] = lg
                tops.append(fold8(lg + bias, jnp.maximum))
        shared["m8"] = jnp.maximum(shared["m8"], jnp.concatenate(tops, axis=1))

    def sel_pairs(pairs, carry):
        shared = {"off": carry[0], "m8": carry[1]}
        _weave(*[sel_pair(c, shared) for c in pairs])
        return shared["off"], shared["m8"]

    carry = lax.fori_loop(0, nsteps, lambda i, carry: sel_pairs((2 * i, 2 * i + 1), carry),
                          (jnp.zeros((1, QBLK), F32),
                           jnp.full((SUBLANES, SA_HEADS * QBLK), NEG_INF, F32)))
    _, m8 = lax.cond(odd, lambda carry: sel_pairs((npair - 1,), carry), lambda carry: carry, carry)
    mrow = jnp.max(m8, axis=0, keepdims=True)

    def probs(c, dst):
        slab = 2 * SUBLANES
        l8 = l8_s[...]
        for r in range(0, KPAIR, slab):
            p = jnp.exp2(lg_s[c, r:r + slab, :] + tile_heads(bias_s[c, r:r + slab, :]) - mrow)
            dst[r:r + slab, :] = p.astype(BF16)
            l8 = l8 + (p[:SUBLANES] + p[SUBLANES:])
        l8_s[...] = l8

    def pv_acc(c, src):
        acc_s[...] += jnp.dot(vt_s[c], src[...], preferred_element_type=F32)

    acc_s[...] = jnp.zeros(acc_s.shape, F32)
    l8_s[...] = jnp.zeros(l8_s.shape, F32)
    probs(0, pa_s)

    def p2(i, carry):
        c = 2 * i + 1
        pv_acc(c - 1, pa_s)
        probs(c, pb_s)

        @pl.when(c + 1 < npair)
        def _():
            pv_acc(c, pb_s)
            probs(c + 1, pa_s)
        return carry

    lax.fori_loop(0, npair // 2, p2, 0)
    last = npair - 1

    @pl.when(last % 2 == 0)
    def _():
        pv_acc(last, pa_s)

    @pl.when(last % 2 == 1)
    def _():
        pv_acc(last, pb_s)

    out_t = acc_s[...] / jnp.sum(l8_s[...], axis=0, keepdims=True)
    order = _even_odd(SA_HEADS)
    for pr in range(SA_HEADS // 2):
        ev, od = order.index(2 * pr), order.index(2 * pr + 1)
        two = jnp.concatenate([out_t[:, ev * QBLK:(ev + 1) * QBLK],
                               out_t[:, od * QBLK:(od + 1) * QBLK]], axis=0)
        o_ref[0, :, pr * LANES:(pr + 1) * LANES] = two.T.astype(o_ref.dtype)


def _dsa(proj, pos3, qg_t, kg_p, rope_pat, topk):
    bsz, seq, _ = proj.shape
    nc = seq // CHUNK
    qw = SA_HEADS * SA_HEAD_DIM
    iw = IDX_HEADS * IDX_DIM
    assert seq % QBLK == 0 and QBLK == KPAIR
    npairs = seq // KPAIR
    pairs = lambda: pltpu.VMEM((npairs, KPAIR, QBLK), F32)
    head_of = jnp.arange(qw) // SA_HEAD_DIM
    head_sum = (head_of[:, None] == head_of[None, :]).astype(BF16)
    return pl.pallas_call(
        functools.partial(_dsa_kernel, topk),
        grid=(bsz, seq // QBLK),
        in_specs=[pl.BlockSpec((1, QBLK, qw), lambda b, j: (b, j, C_SAQ // qw)),
                  pl.BlockSpec((1, QBLK, iw), lambda b, j: (b, j, C_IXQ // iw)),
                  pl.BlockSpec((1, QBLK, LANES), lambda b, j: (b, j, C_MISC // LANES)),
                  pl.BlockSpec((1, seq, LANES), lambda b, j: (b, 0, C_KV // LANES)),
                  pl.BlockSpec((1, seq, LANES), lambda b, j: (b, 0, C_MISC // LANES)),
                  pl.BlockSpec((1, seq * ROT // LANES, LANES), lambda b, j: (b, 0, 0)),
                  _const_spec((1, qw)), _const_spec((1, LANES)), _const_spec((SUBLANES, LANES)),
                  _const_spec((qw, qw))],
        out_specs=pl.BlockSpec((1, QBLK, qw), lambda b, j: (b, j, 0)),
        out_shape=jax.ShapeDtypeStruct((bsz, seq, qw), BF16),
        scratch_shapes=[pltpu.VMEM((seq, LANES), F32), pltpu.VMEM((seq, LANES), F32),
                        pltpu.VMEM((seq, LANES), BF16), pltpu.VMEM((seq, LANES), BF16),
                        pltpu.VMEM((npairs, SA_HEAD_DIM, KPAIR), BF16),
                        pltpu.VMEM((seq, LANES), BF16), pltpu.VMEM((seq, LANES), BF16),
                        pltpu.VMEM((qw // LANES * QBLK, LANES), BF16),
                        pltpu.VMEM((iw // LANES * QBLK, LANES), BF16),
                        pairs(), pairs(),
                        pltpu.VMEM((npairs, KPAIR, SA_HEADS * QBLK), F32),
                        pltpu.VMEM((KPAIR, SA_HEADS * QBLK), BF16),
                        pltpu.VMEM((KPAIR, SA_HEADS * QBLK), BF16),
                        pltpu.VMEM((SUBLANES, SA_HEADS * QBLK), F32),
                        pltpu.VMEM((SA_HEAD_DIM, SA_HEADS * QBLK), F32)],
        compiler_params=pltpu.CompilerParams(dimension_semantics=("arbitrary", "arbitrary"),
                                             vmem_limit_bytes=VMEM_LIMIT),
        name="dsa",
    )(proj, proj, proj, proj, proj, pos3, qg_t, kg_p, rope_pat, head_sum)


def _merge_kernel(x_ref, oa_ref, ob_ref, g_ref, wg_ref, bg_ref, wa_ref, wb_ref, wo_ref, o_ref):
    x = x_ref[...]
    hb = _rms(x, g_ref[...]).astype(BF16)
    oa = oa_ref[...]
    ob = ob_ref[...]
    acc = x
    step = PROJ_COLS
    for n in range(0, D_MODEL, step):
        ga = _sigmoid(jnp.dot(hb, wg_ref[:, n:n + step], preferred_element_type=F32)
                      + bg_ref[:, n:n + step])
        gb = _sigmoid(jnp.dot(hb, wg_ref[:, D_MODEL + n:D_MODEL + n + step],
                              preferred_element_type=F32)
                      + bg_ref[:, D_MODEL + n:D_MODEL + n + step])
        ya = jnp.dot(oa, wa_ref[:, n:n + step], preferred_element_type=F32)
        yb = jnp.dot(ob, wb_ref[:, n:n + step], preferred_element_type=F32)
        merged = (ga * ya + gb * yb).astype(BF16)
        acc = acc + jnp.dot(merged, wo_ref[n:n + step, :], preferred_element_type=F32)
    o_ref[...] = acc


def _merge(x2d, oa2d, ob2d, gain, w_gate, b_gate, w_a, w_b, w_o, tm):
    t = x2d.shape[0]
    tok = lambda w: pl.BlockSpec((tm, w), lambda i: (i, 0))
    return pl.pallas_call(
        _merge_kernel,
        grid=(t // tm,),
        in_specs=[tok(D_MODEL), tok(DN_V), tok(SA_HEADS * SA_HEAD_DIM),
                  _const_spec((1, D_MODEL)), _const_spec((D_MODEL, 2 * D_MODEL)),
                  _const_spec((1, 2 * D_MODEL)), _const_spec((DN_V, D_MODEL)),
                  _const_spec((SA_HEADS * SA_HEAD_DIM, D_MODEL)), _const_spec((D_MODEL, D_MODEL))],
        out_specs=tok(D_MODEL),
        out_shape=jax.ShapeDtypeStruct((t, D_MODEL), F32),
        compiler_params=pltpu.CompilerParams(dimension_semantics=("arbitrary",),
                                             vmem_limit_bytes=VMEM_LIMIT),
        name="merge",
    )(x2d, oa2d, ob2d, gain, w_gate, b_gate, w_a, w_b, w_o)


def _ffn_kernel(x_ref, p_ref, g2_ref, wg_ref, wu_ref, wd_ref, g3_ref, wpg_ref, wp_ref, o_ref):
    x = x_ref[...]
    hb = _rms(x, g2_ref[...]).astype(BF16)
    acc = x
    step = FF_COLS
    for n in range(0, D_FF, step):
        gate = jnp.dot(hb, wg_ref[:, n:n + step], preferred_element_type=F32)
        up = jnp.dot(hb, wu_ref[:, n:n + step], preferred_element_type=F32)
        acc = acc + jnp.dot((_silu(gate) * up).astype(BF16), wd_ref[n:n + step, :],
                            preferred_element_type=F32)
    h3 = _rms(acc, g3_ref[...]).astype(BF16)
    pg = _sigmoid(jnp.dot(h3, wpg_ref[...], preferred_element_type=F32))
    pe = jnp.dot(p_ref[...].astype(BF16), wp_ref[...], preferred_element_type=F32)
    o_ref[...] = acc + pg * pe


def _ffn(x2d, p2d, g2, w_g, w_u, w_d, g3, w_pg, w_p, tm):
    t = x2d.shape[0]
    tok = lambda w: pl.BlockSpec((tm, w), lambda i: (i, 0))
    return pl.pallas_call(
        _ffn_kernel,
        grid=(t // tm,),
        in_specs=[tok(D_MODEL), tok(PLE_DIM), _const_spec((1, D_MODEL)),
                  _const_spec((D_MODEL, D_FF)), _const_spec((D_MODEL, D_FF)),
                  _const_spec((D_FF, D_MODEL)), _const_spec((1, D_MODEL)),
                  _const_spec((D_MODEL, D_MODEL)), _const_spec((PLE_DIM, D_MODEL))],
        out_specs=tok(D_MODEL),
        out_shape=jax.ShapeDtypeStruct((t, D_MODEL), F32),
        compiler_params=pltpu.CompilerParams(dimension_semantics=("arbitrary",),
                                             vmem_limit_bytes=VMEM_LIMIT),
        name="ffn_ple",
    )(x2d, p2d, g2, w_g, w_u, w_d, g3, w_pg, w_p)


def _regroup_w_in(w):
    sizes = (DN_CONV_CH, DN_V, DN_HEADS, DN_HEADS, SA_HEADS * SA_HEAD_DIM, SA_HEAD_DIM, SA_HEAD_DIM,
             IDX_HEADS * IDX_DIM, IDX_DIM, IDX_HEADS)
    parts, off = [], 0
    for s in sizes:
        parts.append(w[:, off:off + s])
        off += s
    qkv, z, b, a, saq, sak, sav, ixq, ixk, ixw = parts
    cols = [qkv, z, saq, ixq, sak, sav, ixk, b, a, ixw]
    used = sum(c.shape[1] for c in cols)
    cols.append(jnp.zeros((w.shape[0], PROJ_W - used), w.dtype))
    return jnp.concatenate(cols, axis=1)


def _rope_pattern():
    lane = jnp.arange(LANES) % SA_HEAD_DIM
    inv_freq = ROPE_THETA ** (-jnp.arange(0, ROT, 2, dtype=F32) / ROT)
    freq = inv_freq[jnp.arange(LANES) % HALF]
    s1 = jnp.where(lane < HALF, -1.0, 0.0)
    s2 = jnp.where((lane >= HALF) & (lane < ROT), 1.0, 0.0)
    pat = jnp.zeros((SUBLANES, LANES), F32)
    return pat.at[0].set(freq).at[1].set(s1).at[2].set(s2)


def _layer(x, p, positions, attn_norm, w_in, conv_w, a_log, dt_bias, dn_norm, q_norm, k_norm,
           w_branch_a, w_branch_b, w_gate, b_gate, w_o, ffn_norm, w_ffn_gate, w_ffn_up, w_ffn_down,
           ple_norm, w_ple_gate, w_ple):
    bsz, seq, _ = x.shape
    t = bsz * seq
    tm = TOKEN_TILE
    assert t % tm == 0
    topk = min(IDX_TOPK_MAX, seq // 4)
    row = lambda v: v.reshape(1, -1).astype(F32)
    x2d = x.reshape(t, D_MODEL)

    proj = _in_proj(x2d, row(attn_norm), _regroup_w_in(w_in).astype(BF16), tm)
    proj = proj.reshape(bsz, seq, PROJ_W)

    pad4 = jnp.zeros((DN_HEADS,), F32)
    per_head = lambda v: jnp.broadcast_to(jnp.concatenate([pad4, v.astype(F32)])[:, None],
                                          (SUBLANES, LANES))
    o_a = _deltanet(proj, conv_w.astype(F32), per_head(a_log), per_head(dt_bias), row(dn_norm))

    qg_t = jnp.tile(q_norm.astype(F32), SA_HEADS).reshape(1, -1)
    kg_p = jnp.concatenate([k_norm.astype(F32), jnp.ones((LANES - SA_HEAD_DIM,), F32)]).reshape(1, -1)
    per_row = LANES // ROT
    pos_packed = jnp.repeat(positions.astype(I32).reshape(bsz, seq // per_row, per_row), ROT, axis=2)
    o_b = _dsa(proj, pos_packed, qg_t, kg_p, _rope_pattern(), topk)

    x1 = _merge(x2d, o_a.reshape(t, DN_V), o_b.reshape(t, -1), row(attn_norm),
                w_gate.astype(BF16), row(b_gate), w_branch_a.astype(BF16), w_branch_b.astype(BF16),
                w_o.astype(BF16), tm)
    x2 = _ffn(x1, p.reshape(t, PLE_DIM), row(ffn_norm), w_ffn_gate.astype(BF16),
              w_ffn_up.astype(BF16), w_ffn_down.astype(BF16), row(ple_norm),
              w_ple_gate.astype(BF16), w_ple.astype(BF16), tm)
    return x2.reshape(bsz, seq, D_MODEL)


def kernel(x, p, positions, attn_norm, w_in, conv_w, a_log, dt_bias, dn_norm, q_norm, k_norm,
           w_branch_a, w_branch_b, w_gate, b_gate, w_o, ffn_norm, w_ffn_gate, w_ffn_up, w_ffn_down,
           ple_norm, w_ple_gate, w_ple):
    depth = w_in.shape[0]
    for i in range(depth):
        x = _layer(x, p[i], positions, attn_norm[i], w_in[i], conv_w[i], a_log[i], dt_bias[i],
                   dn_norm[i], q_norm[i], k_norm[i], w_branch_a[i], w_branch_b[i], w_gate[i],
                   b_gate[i], w_o[i], ffn_norm[i], w_ffn_gate[i], w_ffn_up[i], w_ffn_down[i],
                   ple_norm[i], w_ple_gate[i], w_ple[i])
    return x
```

```python
import functools

import jax
import jax.numpy as jnp
from jax import lax
from jax.experimental import pallas as pl
from jax.experimental.pallas import tpu as pltpu

F32 = jnp.float32
BF16 = jnp.bfloat16
I32 = jnp.int32

D_MODEL = 1024
PLE_DIM = 256
RMS_EPS = 1e-6
DN_HEADS = 4
DN_DK = 128
DN_DV = 128
DN_CONV = 4
DN_QK = DN_HEADS * DN_DK
DN_V = DN_HEADS * DN_DV
DN_CONV_CH = 2 * DN_QK + DN_V
SA_HEADS = 8
SA_HEAD_DIM = 64
IDX_HEADS = 4
IDX_DIM = 64
IDX_TOPK_MAX = 256
ROPE_THETA = 500000.0
ROPE_FRACTION = 4
NEG_INF = -1e30
D_FF = 2816
LOG2E = 1.4426950408889634

LANES = 128
SUBLANES = 8
CHUNK = 128
DN_BASE = 32
DN_PARTS = 2
KPAIR = 2 * CHUNK
QBLK = KPAIR
BISECT_STEPS = 19

C_QKV = 0
C_Z = DN_CONV_CH
C_SAQ = C_Z + DN_V
C_IXQ = C_SAQ + SA_HEADS * SA_HEAD_DIM
C_KV = C_IXQ + IDX_HEADS * IDX_DIM
C_MISC = C_KV + 2 * SA_HEAD_DIM
PROJ_W = C_MISC + LANES
M_IXK = 0
M_B = IDX_DIM
M_A = M_B + DN_HEADS
M_W = M_A + DN_HEADS

ROT = SA_HEAD_DIM // ROPE_FRACTION
HALF = ROT // 2

VMEM_LIMIT = 56 * 1024 * 1024
TOKEN_TILE = 1024
PROJ_COLS = 512
FF_COLS = 256


def _const_spec(shape):
    nd = len(shape)
    return pl.BlockSpec(shape, lambda *_: (0,) * nd, pipeline_mode=pl.Buffered(1))


def _rms(x, g):
    return x * lax.rsqrt(jnp.mean(x * x, axis=-1, keepdims=True) + RMS_EPS) * g


def _mm(a, b):
    return jnp.dot(a.astype(BF16), b.astype(BF16), preferred_element_type=F32)


def _mm_nt(a, b):
    return lax.dot_general(a.astype(BF16), b.astype(BF16), (((1,), (1,)), ((), ())),
                           preferred_element_type=F32)


def _weave(*gens):
    live = list(gens)
    while live:
        live = [g for g in live if next(g, True) is None]


def _sigmoid(x):
    return 1.0 / (1.0 + jnp.exp(-x))


def _silu(x):
    return x * _sigmoid(x)


def _softplus(x):
    return jnp.maximum(x, 0.0) + jnp.log1p(jnp.exp(-jnp.abs(x)))


def _in_proj_kernel(x_ref, g_ref, w_ref, o_ref):
    hb = _rms(x_ref[...], g_ref[...]).astype(BF16)
    for n in range(0, PROJ_W, PROJ_COLS):
        o_ref[:, n:n + PROJ_COLS] = jnp.dot(hb, w_ref[:, n:n + PROJ_COLS],
                                            preferred_element_type=F32)


def _in_proj(x2d, gain, w_perm, tm):
    t = x2d.shape[0]
    return pl.pallas_call(
        _in_proj_kernel,
        grid=(t // tm,),
        in_specs=[pl.BlockSpec((tm, D_MODEL), lambda i: (i, 0)),
                  _const_spec((1, D_MODEL)),
                  _const_spec((D_MODEL, PROJ_W))],
        out_specs=pl.BlockSpec((tm, PROJ_W), lambda i: (i, 0)),
        out_shape=jax.ShapeDtypeStruct((t, PROJ_W), F32),
        compiler_params=pltpu.CompilerParams(dimension_semantics=("arbitrary",),
                                             vmem_limit_bytes=VMEM_LIMIT),
        name="in_proj",
    )(x2d, gain, w_perm)


def _dn_kernel(q_ref, k_ref, v_ref, z_ref, misc_ref, cwq_ref, cwk_ref, cwv_ref, alog_ref, dtb_ref,
               ng_ref, o_ref,
               pad_s, cq_s, ck_s, cv_s, brow_s, grow_s, u_s, w_s, a_s, qg_s, c_s, n_s, el_s):
    seq = q_ref.shape[1]
    nc = seq // CHUNK
    h = pl.program_id(1)
    row = lax.broadcasted_iota(I32, (CHUNK, CHUNK), 0)
    col = lax.broadcasted_iota(I32, (CHUNK, CHUNK), 1)

    def conv(chunks):
        for n in chunks:
            for src, cw_ref, dst, kind in ((q_ref, cwq_ref, cq_s, "q"), (k_ref, cwk_ref, ck_s, "k"),
                                           (v_ref, cwv_ref, cv_s, "v")):
                if n == 0:
                    pad_s[0:SUBLANES, :] = jnp.zeros((SUBLANES, LANES), F32)
                    pad_s[SUBLANES:SUBLANES + CHUNK, :] = src[0, 0:CHUNK, :]
                    tap = lambda j: pad_s[SUBLANES - j:SUBLANES - j + CHUNK, :]
                else:
                    tap = lambda j, src=src: src[0, n * CHUNK - j:(n + 1) * CHUNK - j, :]
                y = cw_ref[DN_CONV - 1:DN_CONV, :] * tap(0)
                for j in range(1, DN_CONV):
                    y = y + cw_ref[DN_CONV - 1 - j:DN_CONV - j, :] * tap(j)
                y = _silu(y)
                if kind != "v":
                    y = y * lax.rsqrt(jnp.sum(y * y, axis=-1, keepdims=True) + 1e-6)
                if kind == "q":
                    y = y * (DN_DK ** -0.5)
                dst[n * CHUNK:(n + 1) * CHUNK, :] = y
            yield

    upper = jnp.where(row <= col, 1.0, 0.0).astype(F32)
    sub8 = lax.broadcasted_iota(I32, (SUBLANES, LANES), 0)
    for n in range(nc):
        mt = misc_ref[0, n * CHUNK:(n + 1) * CHUNK, :].T
        slab = mt[M_B:M_B + SUBLANES, :]
        beta8 = _sigmoid(slab)
        g8 = -jnp.exp(alog_ref[...]) * _softplus(slab + dtb_ref[...])
        gc8 = jnp.dot(g8, upper, preferred_element_type=F32, precision=lax.Precision.HIGHEST)
        b_row = jnp.sum(jnp.where(sub8 == h, beta8, 0.0), axis=0, keepdims=True)
        g_row = jnp.sum(jnp.where(sub8 == h + DN_HEADS, gc8, 0.0), axis=0, keepdims=True)
        brow_s[n] = jnp.broadcast_to(b_row, (SUBLANES, LANES))
        grow_s[n] = jnp.broadcast_to(g_row, (SUBLANES, LANES))

    eye = jnp.where(row == col, 1.0, 0.0).astype(F32)

    def prep_one(n):
        rows = slice(n * CHUNK, (n + 1) * CHUNK)
        q = cq_s[rows, :]
        k = ck_s[rows, :]
        v = cv_s[rows, :]
        g_r = jnp.broadcast_to(grow_s[n][0:1, :], (CHUNK, CHUNK))
        g_c = g_r.T
        b_c = jnp.broadcast_to(brow_s[n][0:1, :], (CHUNK, CHUNK)).T
        g_last = jnp.broadcast_to(g_c[CHUNK - 1:CHUNK, :], (CHUNK, CHUNK))
        decay = jnp.where(row >= col, jnp.exp(jnp.minimum(g_c - g_r, 0.0)), 0.0)
        kb = k * b_c
        kq = _mm_nt(jnp.concatenate([kb, q], axis=0), k)
        yield
        lmat = jnp.where(row > col, kq[:CHUNK] * decay, 0.0)
        a = jnp.where(row >= col, kq[CHUNK:] * decay, 0.0)
        same = lambda b: (row // b) == (col // b)
        l0 = jnp.where(same(DN_BASE), lmat, 0.0)
        x = eye - l0
        m = _mm(l0, l0)
        yield
        power = 2
        while 2 * power < DN_BASE:
            xm = _mm(jnp.concatenate([x, m], axis=0), m)
            yield
            x = x + xm[:CHUNK]
            m = xm[CHUNK:]
            power *= 2
        x = x + _mm(x, m)
        yield
        blk = DN_BASE
        while blk < CHUNK:
            below = jnp.where(same(2 * blk) & jnp.logical_not(same(blk)), lmat, 0.0)
            y = _mm(below, x)
            yield
            x = x - _mm(x, y)
            yield
            blk *= 2
        eg = jnp.exp(g_c)
        sol = _mm(x, jnp.concatenate([v * b_c, kb * eg], axis=1))
        yield
        cn = _mm((k * jnp.exp(g_last - g_c)).T, sol)
        yield
        u_s[n] = sol[:, :DN_DV]
        w_s[n] = sol[:, DN_DV:]
        a_s[n] = a
        qg_s[n] = q * eg
        c_s[n] = cn[:, :DN_DV]
        n_s[n] = cn[:, DN_DV:]
        el_s[n] = jnp.exp(g_last)[0:SUBLANES, :]

    carried = [jnp.zeros((DN_DK, DN_DV), F32)]

    def recur(chunks):
        for n in chunks:
            state = carried[0]
            el = jnp.broadcast_to(el_s[n][0:1, :], (CHUNK, CHUNK))
            nws = _mm(jnp.concatenate([n_s[n], w_s[n], qg_s[n]], axis=0), state)
            yield
            carried[0] = state * el + c_s[n] - nws[:CHUNK]
            o = nws[2 * CHUNK:] + _mm(a_s[n], u_s[n] - nws[CHUNK:2 * CHUNK])
            yield
            rows = slice(n * CHUNK, (n + 1) * CHUNK)
            o_ref[0, rows, :] = (_rms(o, ng_ref[...]) * _silu(z_ref[0, rows, :])).astype(o_ref.dtype)

    parts = [range(i * nc // DN_PARTS, (i + 1) * nc // DN_PARTS) for i in range(DN_PARTS)]
    for i in range(DN_PARTS + 2):
        stage = []
        if i < DN_PARTS:
            stage.append(conv(parts[i]))
        if 1 <= i <= DN_PARTS:
            stage.extend(prep_one(n) for n in parts[i - 1])
        if i >= 2:
            stage.append(recur(parts[i - 2]))
        _weave(*stage)


def _deltanet(proj, conv_w, alog8, dtb8, norm_g):
    bsz, seq, _ = proj.shape
    nc = seq // CHUNK
    assert seq % (DN_PARTS * CHUNK) == 0
    qb, kb, vb, zb = (C_QKV // LANES, (C_QKV + DN_QK) // LANES, (C_QKV + 2 * DN_QK) // LANES,
                      C_Z // LANES)
    col = lambda off: pl.BlockSpec((1, seq, LANES), lambda b, h, off=off: (b, 0, off + h))
    cw = lambda off: pl.BlockSpec((DN_CONV, LANES), lambda b, h, off=off: (0, off + h))
    mat = lambda: pltpu.VMEM((nc, CHUNK, CHUNK), F32)
    return pl.pallas_call(
        _dn_kernel,
        grid=(bsz, DN_HEADS),
        in_specs=[col(qb), col(kb), col(vb), col(zb),
                  pl.BlockSpec((1, seq, LANES), lambda b, h: (b, 0, C_MISC // LANES)),
                  cw(qb), cw(kb), cw(vb),
                  _const_spec((SUBLANES, LANES)), _const_spec((SUBLANES, LANES)),
                  _const_spec((1, DN_DV))],
        out_specs=pl.BlockSpec((1, seq, LANES), lambda b, h: (b, 0, h)),
        out_shape=jax.ShapeDtypeStruct((bsz, seq, DN_V), BF16),
        scratch_shapes=[pltpu.VMEM((SUBLANES + CHUNK, LANES), F32),
                        pltpu.VMEM((seq, LANES), F32), pltpu.VMEM((seq, LANES), F32),
                        pltpu.VMEM((seq, LANES), F32),
                        pltpu.VMEM((nc, SUBLANES, LANES), F32), pltpu.VMEM((nc, SUBLANES, LANES), F32),
                        mat(), mat(), mat(), mat(), mat(), mat(),
                        pltpu.VMEM((nc, SUBLANES, LANES), F32)],
        compiler_params=pltpu.CompilerParams(dimension_semantics=("arbitrary", "arbitrary"),
                                             vmem_limit_bytes=VMEM_LIMIT),
        name="deltanet",
    )(proj, proj, proj, proj, proj, conv_w, conv_w, conv_w, alog8, dtb8, norm_g)


def _even_odd(n):
    return list(range(0, n, 2)) + list(range(1, n, 2))


def _rope(x, c, s, s1, s2):
    w = x.shape[-1]
    return x * c + pltpu.roll(x, w - HALF, 1) * (s * s1) + pltpu.roll(x, HALF, 1) * (s * s2)


def _dsa_kernel(topk, q_ref, qi_ref, misct_ref, kv_ref, miscf_ref, pos_ref, qg_ref, kg_ref, rp_ref,
                hsum_ref, o_ref,
                cos_s, sin_s, ka_s, kb_s, vt_s, kia_s, kib_s, q2_s, qi2_s, sc_s, lg_s, pa_s,
                pb_s, l8_s, acc_s):
    seq = kv_ref.shape[1]
    nc = seq // CHUNK
    qb = pl.program_id(1)
    npair = qb + 1
    r0 = pl.multiple_of(qb * QBLK, QBLK)
    col = lax.broadcasted_iota(I32, (CHUNK, CHUNK), 1)
    s1 = rp_ref[1:2, :]
    s2 = rp_ref[2:3, :]

    @pl.when(qb == 0)
    def _():
        ang = pos_ref[0].astype(F32) * rp_ref[0:1, :]
        packed = (jnp.cos(ang), jnp.sin(ang))
        lane = lax.broadcasted_iota(I32, ang.shape, 1)
        second = (lane >= SA_HEAD_DIM) & (lane < SA_HEAD_DIM + ROT)
        per_row = LANES // ROT
        for j in range(per_row):
            for tab, dst, rest in zip(packed, (cos_s, sin_s), (1.0, 0.0)):
                mine = tab if j == 0 else pltpu.roll(tab, LANES - j * ROT, 1)
                full = jnp.where(lane < ROT, mine,
                                 jnp.where(second, pltpu.roll(mine, SA_HEAD_DIM, 1), rest))
                dst[pl.ds(j, seq // per_row, stride=per_row), :] = full
        for n in range(nc):
            rows = slice(n * CHUNK, (n + 1) * CHUNK)
            c = cos_s[rows, :]
            s = sin_s[rows, :]
            ck = jnp.where(col < ROT, c, 1.0)
            sk = jnp.where(col < ROT, s, 0.0)
            kv = kv_ref[0, rows, :]
            ss = jnp.dot((kv * kv).astype(BF16), hsum_ref[0:LANES, 0:LANES],
                         preferred_element_type=F32)
            r = lax.rsqrt(ss * (1.0 / SA_HEAD_DIM) + RMS_EPS)
            y = jnp.where(col < SA_HEAD_DIM, kv * r * kg_ref[...], kv)
            y = _rope(y, ck, sk, s1, s2)
            ka = jnp.where(col < SA_HEAD_DIM, y, 0.0)
            ka_s[rows, :] = ka.astype(BF16)
            kb_s[rows, :] = pltpu.roll(ka, SA_HEAD_DIM, 1).astype(BF16)
            half = (n % 2) * CHUNK
            vt_s[n // 2, :, half:half + CHUNK] = y.T[SA_HEAD_DIM:, :].astype(BF16)
            yi = _rope(miscf_ref[0, rows, :], ck, sk, s1, s2)
            kia = jnp.where(col < IDX_DIM, yi, 0.0)
            kia_s[rows, :] = kia.astype(BF16)
            kib_s[rows, :] = pltpu.roll(kia, IDX_DIM, 1).astype(BF16)

    c1 = cos_s[pl.ds(r0, QBLK), :]
    sn1 = sin_s[pl.ds(r0, QBLK), :]
    rep = lambda a, k: jnp.concatenate([a] * k, axis=1)
    nq = SA_HEADS * SA_HEAD_DIM // LANES
    xq = q_ref[0]
    yq = _rope(xq * qg_ref[...], rep(c1, nq), rep(sn1, nq), rep(s1, nq), rep(s2, nq))
    ss = jnp.dot((xq * xq).astype(BF16), hsum_ref[...], preferred_element_type=F32)
    r = lax.rsqrt(ss * (1.0 / SA_HEAD_DIM) + RMS_EPS)
    qn = (yq * (r * (SA_HEAD_DIM ** -0.5 * LOG2E))).astype(BF16)
    for pr in range(nq):
        q2_s[pr * QBLK:(pr + 1) * QBLK, :] = qn[:, pr * LANES:(pr + 1) * LANES]
    ni = IDX_HEADS * IDX_DIM // LANES
    yi = _rope(qi_ref[0], rep(c1, ni), rep(sn1, ni), rep(s1, ni), rep(s2, ni)).astype(BF16)
    for pr in range(ni):
        qi2_s[pr * QBLK:(pr + 1) * QBLK, :] = yi[:, pr * LANES:(pr + 1) * LANES]
    mt = misct_ref[0].T
    wscale = (IDX_HEADS ** -0.5) * (IDX_DIM ** -0.5)
    wrow = jnp.concatenate([mt[M_W + hh:M_W + hh + 1, :] for hh in _even_odd(IDX_HEADS)],
                           axis=1) * wscale

    krow = lax.broadcasted_iota(I32, (KPAIR, QBLK), 0)
    qpos = r0 + lax.broadcasted_iota(I32, (KPAIR, QBLK), 1)
    big = -NEG_INF

    def fold8(x, op):
        parts = [x[i:i + SUBLANES] for i in range(0, x.shape[0], SUBLANES)]
        while len(parts) > 1:
            parts = [op(parts[i], parts[i + 1]) for i in range(0, len(parts), 2)]
        return parts[0]

    def idx_pair(c, bounds):
        k0 = pl.multiple_of(c * KPAIR, KPAIR)
        terms = []
        for keys_s in (kia_s, kib_s):
            keys = keys_s[pl.ds(k0, KPAIR), :]
            for pr in range(IDX_HEADS // 2):
                lt = _mm_nt(keys, qi2_s[pr * QBLK:(pr + 1) * QBLK, :])
                yield
                j = len(terms)
                terms.append(jnp.maximum(lt, 0.0) * wrow[:, j * QBLK:(j + 1) * QBLK])
        acc = (terms[0] + terms[1]) + (terms[2] + terms[3])
        causal = k0 + krow <= qpos
        sc = jnp.where(causal, acc, NEG_INF)
        sc_s[c] = sc
        bounds.append((fold8(jnp.where(causal, acc, big), jnp.minimum), fold8(sc, jnp.maximum)))

    def idx_pairs(pairs, carry):
        bounds = []
        _weave(*[idx_pair(c, bounds) for c in pairs])
        lo8, hi8 = carry
        for lo_c, hi_c in bounds:
            lo8 = jnp.minimum(lo8, lo_c)
            hi8 = jnp.maximum(hi8, hi_c)
        return lo8, hi8

    nsteps = npair // 2
    odd = npair % 2 == 1
    lo8, hi8 = lax.fori_loop(0, nsteps, lambda i, carry: idx_pairs((2 * i, 2 * i + 1), carry),
                             (jnp.full((SUBLANES, QBLK), big, F32),
                              jnp.full((SUBLANES, QBLK), NEG_INF, F32)))
    lo8, hi8 = lax.cond(odd, lambda carry: idx_pairs((npair - 1,), carry), lambda carry: carry,
                        (lo8, hi8))
    lo = jnp.min(lo8, axis=0, keepdims=True)
    hi = jnp.max(hi8, axis=0, keepdims=True)

    def count(pred):
        def body(c, acc):
            return acc + fold8(jnp.where(pred(sc_s[c]), 1.0, 0.0), jnp.add)
        acc = lax.fori_loop(0, npair, body, jnp.zeros((SUBLANES, QBLK), F32))
        return jnp.sum(acc, axis=0, keepdims=True)

    def max_where(pred):
        def body(c, acc):
            s = sc_s[c]
            return jnp.maximum(acc, fold8(jnp.where(pred(s), s, NEG_INF), jnp.maximum))
        acc = lax.fori_loop(0, npair, body, jnp.full((SUBLANES, QBLK), NEG_INF, F32))
        return jnp.max(acc, axis=0, keepdims=True)

    kf = jnp.minimum(float(topk), (qpos[0:1, :] + 1).astype(F32))

    def bisect(_, bounds):
        lo, hi = bounds
        mid = lo + 0.5 * (hi - lo)
        above = count(lambda s: s > mid) >= kf
        return jnp.where(above, mid, lo), jnp.where(above, hi, mid)

    lo, hi = lax.fori_loop(0, BISECT_STEPS, bisect, (lo, hi))

    thr = max_where(lambda s: s <= hi)
    cge = count(lambda s: s >= thr)

    def short(cge):
        return jnp.max(jnp.where(cge < kf, 1.0, 0.0)) > 0.5

    def peel(state):
        thr, cge, _ = state
        nxt = max_where(lambda s: s < thr)
        cnx = count(lambda s: s >= nxt)
        step = cge < kf
        thr = jnp.where(step, nxt, thr)
        cge = jnp.where(step, cnx, cge)
        return thr, cge, short(cge)

    thr, cge, _ = lax.while_loop(lambda st: st[2], peel, (thr, cge, short(cge)))
    need = kf - count(lambda s: s > thr)

    lower = jnp.where(lax.broadcasted_iota(I32, (KPAIR, KPAIR), 0)
                      >= lax.broadcasted_iota(I32, (KPAIR, KPAIR), 1), 1.0, 0.0).astype(BF16)

    def sel_pair(c, shared):
        k0 = pl.multiple_of(c * KPAIR, KPAIR)
        s = sc_s[c]
        eqf = jnp.where(s == thr, 1.0, 0.0)
        pref = jnp.dot(lower, eqf.astype(BF16), preferred_element_type=F32)
        yield
        off = shared["off"]
        shared["off"] = off + pref[KPAIR - 1:KPAIR, :]
        tie = jnp.where(pref + off <= need, eqf, 0.0)
        keep = jnp.where(k0 + krow <= qpos, jnp.where(s > thr, 1.0, tie), 0.0)
        bias = jnp.where(keep > 0.5, 0.0, NEG_INF)
        tops = []
        for keys_s in (ka_s, kb_s):
            keys = keys_s[pl.ds(k0, KPAIR), :]
            for pr in range(SA_HEADS // 2):
                lg = _mm_nt(keys, q2_s[pr * QBLK:(pr + 1) * QBLK, :])
                yield
                lg = lg + bias
                lg_s[c, :, len(tops) * QBLK:(len(tops) + 1) * QBLK] = lg
                tops.append(fold8(lg, jnp.maximum))
        shared["m8"] = jnp.maximum(shared["m8"], jnp.concatenate(tops, axis=1))

    def sel_pairs(pairs, carry):
        shared = {"off": carry[0], "m8": carry[1]}
        _weave(*[sel_pair(c, shared) for c in pairs])
        return shared["off"], shared["m8"]

    carry = lax.fori_loop(0, nsteps, lambda i, carry: sel_pairs((2 * i, 2 * i + 1), carry),
                          (jnp.zeros((1, QBLK), F32),
                           jnp.full((SUBLANES, SA_HEADS * QBLK), NEG_INF, F32)))
    _, m8 = lax.cond(odd, lambda carry: sel_pairs((npair - 1,), carry), lambda carry: carry, carry)
    mrow = jnp.max(m8, axis=0, keepdims=True)

    def probs(c, dst):
        slab = 2 * SUBLANES
        l8 = l8_s[...]
        for r in range(0, KPAIR, slab):
            p = jnp.exp2(lg_s[c, r:r + slab, :] - mrow)
            dst[r:r + slab, :] = p.astype(BF16)
            l8 = l8 + (p[:SUBLANES] + p[SUBLANES:])
        l8_s[...] = l8

    def pv_acc(c, src):
        acc_s[...] += jnp.dot(vt_s[c], src[...], preferred_element_type=F32)

    acc_s[...] = jnp.zeros(acc_s.shape, F32)
    l8_s[...] = jnp.zeros(l8_s.shape, F32)
    probs(0, pa_s)

    def p2(i, carry):
        c = 2 * i + 1
        pv_acc(c - 1, pa_s)
        probs(c, pb_s)

        @pl.when(c + 1 < npair)
        def _():
            pv_acc(c, pb_s)
            probs(c + 1, pa_s)
        return carry

    lax.fori_loop(0, npair // 2, p2, 0)
    last = npair - 1

    @pl.when(last % 2 == 0)
    def _():
        pv_acc(last, pa_s)

    @pl.when(last % 2 == 1)
    def _():
        pv_acc(last, pb_s)

    out_t = acc_s[...] / jnp.sum(l8_s[...], axis=0, keepdims=True)
    order = _even_odd(SA_HEADS)
    for pr in range(SA_HEADS // 2):
        ev, od = order.index(2 * pr), order.index(2 * pr + 1)
        two = jnp.concatenate([out_t[:, ev * QBLK:(ev + 1) * QBLK],
                               out_t[:, od * QBLK:(od + 1) * QBLK]], axis=0)
        o_ref[0, :, pr * LANES:(pr + 1) * LANES] = two.T.astype(o_ref.dtype)


def _dsa(proj, pos3, qg_t, kg_p, rope_pat, topk):
    bsz, seq, _ = proj.shape
    nc = seq // CHUNK
    qw = SA_HEADS * SA_HEAD_DIM
    iw = IDX_HEADS * IDX_DIM
    assert seq % QBLK == 0 and QBLK == KPAIR
    npairs = seq // KPAIR
    pairs = lambda: pltpu.VMEM((npairs, KPAIR, QBLK), F32)
    head_of = jnp.arange(qw) // SA_HEAD_DIM
    head_sum = (head_of[:, None] == head_of[None, :]).astype(BF16)
    return pl.pallas_call(
        functools.partial(_dsa_kernel, topk),
        grid=(bsz, seq // QBLK),
        in_specs=[pl.BlockSpec((1, QBLK, qw), lambda b, j: (b, j, C_SAQ // qw)),
                  pl.BlockSpec((1, QBLK, iw), lambda b, j: (b, j, C_IXQ // iw)),
                  pl.BlockSpec((1, QBLK, LANES), lambda b, j: (b, j, C_MISC // LANES)),
                  pl.BlockSpec((1, seq, LANES), lambda b, j: (b, 0, C_KV // LANES)),
                  pl.BlockSpec((1, seq, LANES), lambda b, j: (b, 0, C_MISC // LANES)),
                  pl.BlockSpec((1, seq * ROT // LANES, LANES), lambda b, j: (b, 0, 0)),
                  _const_spec((1, qw)), _const_spec((1, LANES)), _const_spec((SUBLANES, LANES)),
                  _const_spec((qw, qw))],
        out_specs=pl.BlockSpec((1, QBLK, qw), lambda b, j: (b, j, 0)),
        out_shape=jax.ShapeDtypeStruct((bsz, seq, qw), BF16),
        scratch_shapes=[pltpu.VMEM((seq, LANES), F32), pltpu.VMEM((seq, LANES), F32),
                        pltpu.VMEM((seq, LANES), BF16), pltpu.VMEM((seq, LANES), BF16),
                        pltpu.VMEM((npairs, SA_HEAD_DIM, KPAIR), BF16),
                        pltpu.VMEM((seq, LANES), BF16), pltpu.VMEM((seq, LANES), BF16),
                        pltpu.VMEM((qw // LANES * QBLK, LANES), BF16),
                        pltpu.VMEM((iw // LANES * QBLK, LANES), BF16),
                        pairs(),
                        pltpu.VMEM((npairs, KPAIR, SA_HEADS * QBLK), F32),
                        pltpu.VMEM((KPAIR, SA_HEADS * QBLK), BF16),
                        pltpu.VMEM((KPAIR, SA_HEADS * QBLK), BF16),
                        pltpu.VMEM((SUBLANES, SA_HEADS * QBLK), F32),
                        pltpu.VMEM((SA_HEAD_DIM, SA_HEADS * QBLK), F32)],
        compiler_params=pltpu.CompilerParams(dimension_semantics=("arbitrary", "arbitrary"),
                                             vmem_limit_bytes=VMEM_LIMIT),
        name="dsa",
    )(proj, proj, proj, proj, proj, pos3, qg_t, kg_p, rope_pat, head_sum)


def _merge_kernel(x_ref, oa_ref, ob_ref, g_ref, wg_ref, bg_ref, wa_ref, wb_ref, wo_ref, o_ref):
    x = x_ref[...]
    hb = _rms(x, g_ref[...]).astype(BF16)
    oa = oa_ref[...]
    ob = ob_ref[...]
    acc = x
    step = PROJ_COLS
    for n in range(0, D_MODEL, step):
        ga = _sigmoid(jnp.dot(hb, wg_ref[:, n:n + step], preferred_element_type=F32)
                      + bg_ref[:, n:n + step])
        gb = _sigmoid(jnp.dot(hb, wg_ref[:, D_MODEL + n:D_MODEL + n + step],
                              preferred_element_type=F32)
                      + bg_ref[:, D_MODEL + n:D_MODEL + n + step])
        ya = jnp.dot(oa, wa_ref[:, n:n + step], preferred_element_type=F32)
        yb = jnp.dot(ob, wb_ref[:, n:n + step], preferred_element_type=F32)
        merged = (ga * ya + gb * yb).astype(BF16)
        acc = acc + jnp.dot(merged, wo_ref[n:n + step, :], preferred_element_type=F32)
    o_ref[...] = acc


def _merge(x2d, oa2d, ob2d, gain, w_gate, b_gate, w_a, w_b, w_o, tm):
    t = x2d.shape[0]
    tok = lambda w: pl.BlockSpec((tm, w), lambda i: (i, 0))
    return pl.pallas_call(
        _merge_kernel,
        grid=(t // tm,),
        in_specs=[tok(D_MODEL), tok(DN_V), tok(SA_HEADS * SA_HEAD_DIM),
                  _const_spec((1, D_MODEL)), _const_spec((D_MODEL, 2 * D_MODEL)),
                  _const_spec((1, 2 * D_MODEL)), _const_spec((DN_V, D_MODEL)),
                  _const_spec((SA_HEADS * SA_HEAD_DIM, D_MODEL)), _const_spec((D_MODEL, D_MODEL))],
        out_specs=tok(D_MODEL),
        out_shape=jax.ShapeDtypeStruct((t, D_MODEL), F32),
        compiler_params=pltpu.CompilerParams(dimension_semantics=("arbitrary",),
                                             vmem_limit_bytes=VMEM_LIMIT),
        name="merge",
    )(x2d, oa2d, ob2d, gain, w_gate, b_gate, w_a, w_b, w_o)


def _ffn_kernel(x_ref, p_ref, g2_ref, wg_ref, wu_ref, wd_ref, g3_ref, wpg_ref, wp_ref, o_ref):
    x = x_ref[...]
    hb = _rms(x, g2_ref[...]).astype(BF16)
    acc = x
    step = FF_COLS
    for n in range(0, D_FF, step):
        gate = jnp.dot(hb, wg_ref[:, n:n + step], preferred_element_type=F32)
        up = jnp.dot(hb, wu_ref[:, n:n + step], preferred_element_type=F32)
        acc = acc + jnp.dot((_silu(gate) * up).astype(BF16), wd_ref[n:n + step, :],
                            preferred_element_type=F32)
    h3 = _rms(acc, g3_ref[...]).astype(BF16)
    pg = _sigmoid(jnp.dot(h3, wpg_ref[...], preferred_element_type=F32))
    pe = jnp.dot(p_ref[...].astype(BF16), wp_ref[...], preferred_element_type=F32)
    o_ref[...] = acc + pg * pe


def _ffn(x2d, p2d, g2, w_g, w_u, w_d, g3, w_pg, w_p, tm):
    t = x2d.shape[0]
    tok = lambda w: pl.BlockSpec((tm, w), lambda i: (i, 0))
    return pl.pallas_call(
        _ffn_kernel,
        grid=(t // tm,),
        in_specs=[tok(D_MODEL), tok(PLE_DIM), _const_spec((1, D_MODEL)),
                  _const_spec((D_MODEL, D_FF)), _const_spec((D_MODEL, D_FF)),
                  _const_spec((D_FF, D_MODEL)), _const_spec((1, D_MODEL)),
                  _const_spec((D_MODEL, D_MODEL)), _const_spec((PLE_DIM, D_MODEL))],
        out_specs=tok(D_MODEL),
        out_shape=jax.ShapeDtypeStruct((t, D_MODEL), F32),
        compiler_params=pltpu.CompilerParams(dimension_semantics=("arbitrary",),
                                             vmem_limit_bytes=VMEM_LIMIT),
        name="ffn_ple",
    )(x2d, p2d, g2, w_g, w_u, w_d, g3, w_pg, w_p)


def _regroup_w_in(w):
    sizes = (DN_CONV_CH, DN_V, DN_HEADS, DN_HEADS, SA_HEADS * SA_HEAD_DIM, SA_HEAD_DIM, SA_HEAD_DIM,
             IDX_HEADS * IDX_DIM, IDX_DIM, IDX_HEADS)
    parts, off = [], 0
    for s in sizes:
        parts.append(w[:, off:off + s])
        off += s
    qkv, z, b, a, saq, sak, sav, ixq, ixk, ixw = parts
    cols = [qkv, z, saq, ixq, sak, sav, ixk, b, a, ixw]
    used = sum(c.shape[1] for c in cols)
    cols.append(jnp.zeros((w.shape[0], PROJ_W - used), w.dtype))
    return jnp.concatenate(cols, axis=1)


def _rope_pattern():
    lane = jnp.arange(LANES) % SA_HEAD_DIM
    inv_freq = ROPE_THETA ** (-jnp.arange(0, ROT, 2, dtype=F32) / ROT)
    freq = inv_freq[jnp.arange(LANES) % HALF]
    s1 = jnp.where(lane < HALF, -1.0, 0.0)
    s2 = jnp.where((lane >= HALF) & (lane < ROT), 1.0, 0.0)
    pat = jnp.zeros((SUBLANES, LANES), F32)
    return pat.at[0].set(freq).at[1].set(s1).at[2].set(s2)


def _layer(x, p, positions, attn_norm, w_in, conv_w, a_log, dt_bias, dn_norm, q_norm, k_norm,
           w_branch_a, w_branch_b, w_gate, b_gate, w_o, ffn_norm, w_ffn_gate, w_ffn_up, w_ffn_down,
           ple_norm, w_ple_gate, w_ple):
    bsz, seq, _ = x.shape
    t = bsz * seq
    tm = TOKEN_TILE
    assert t % tm == 0
    topk = min(IDX_TOPK_MAX, seq // 4)
    row = lambda v: v.reshape(1, -1).astype(F32)
    x2d = x.reshape(t, D_MODEL)

    proj = _in_proj(x2d, row(attn_norm), _regroup_w_in(w_in).astype(BF16), tm)
    proj = proj.reshape(bsz, seq, PROJ_W)

    pad4 = jnp.zeros((DN_HEADS,), F32)
    per_head = lambda v: jnp.broadcast_to(jnp.concatenate([pad4, v.astype(F32)])[:, None],
                                          (SUBLANES, LANES))
    o_a = _deltanet(proj, conv_w.astype(F32), per_head(a_log), per_head(dt_bias), row(dn_norm))

    qg_t = jnp.tile(q_norm.astype(F32), SA_HEADS).reshape(1, -1)
    kg_p = jnp.concatenate([k_norm.astype(F32), jnp.ones((LANES - SA_HEAD_DIM,), F32)]).reshape(1, -1)
    per_row = LANES // ROT
    pos_packed = jnp.repeat(positions.astype(I32).reshape(bsz, seq // per_row, per_row), ROT, axis=2)
    o_b = _dsa(proj, pos_packed, qg_t, kg_p, _rope_pattern(), topk)

    x1 = _merge(x2d, o_a.reshape(t, DN_V), o_b.reshape(t, -1), row(attn_norm),
                w_gate.astype(BF16), row(b_gate), w_branch_a.astype(BF16), w_branch_b.astype(BF16),
                w_o.astype(BF16), tm)
    x2 = _ffn(x1, p.reshape(t, PLE_DIM), row(ffn_norm), w_ffn_gate.astype(BF16),
              w_ffn_up.astype(BF16), w_ffn_down.astype(BF16), row(ple_norm),
              w_ple_gate.astype(BF16), w_ple.astype(BF16), tm)
    return x2.reshape(bsz, seq, D_MODEL)


def kernel(x, p, positions, attn_norm, w_in, conv_w, a_log, dt_bias, dn_norm, q_norm, k_norm,
           w_branch_a, w_branch_b, w_gate, b_gate, w_o, ffn_norm, w_ffn_gate, w_ffn_up, w_ffn_down,
           ple_norm, w_ple_gate, w_ple):
    depth = w_in.shape[0]
    for i in range(depth):
        x = _layer(x, p[i], positions, attn_norm[i], w_in[i], conv_w[i], a_log[i], dt_bias[i],
                   dn_norm[i], q_norm[i], k_norm[i], w_branch_a[i], w_branch_b[i], w_gate[i],
                   b_gate[i], w_o[i], ffn_norm[i], w_ffn_gate[i], w_ffn_up[i], w_ffn_down[i],
                   ple_norm[i], w_ple_gate[i], w_ple[i])
    return x
```

```python
import functools

import jax
import jax.numpy as jnp
from jax import lax
from jax.experimental import pallas as pl
from jax.experimental.pallas import tpu as pltpu

F32 = jnp.float32
BF16 = jnp.bfloat16
I32 = jnp.int32

D_MODEL = 1024
PLE_DIM = 256
RMS_EPS = 1e-6
DN_HEADS = 4
DN_DK = 128
DN_DV = 128
DN_CONV = 4
DN_QK = DN_HEADS * DN_DK
DN_V = DN_HEADS * DN_DV
DN_CONV_CH = 2 * DN_QK + DN_V
SA_HEADS = 8
SA_HEAD_DIM = 64
IDX_HEADS = 4
IDX_DIM = 64
IDX_TOPK_MAX = 256
ROPE_THETA = 500000.0
ROPE_FRACTION = 4
NEG_INF = -1e30
D_FF = 2816
LOG2E = 1.4426950408889634

LANES = 128
SUBLANES = 8
CHUNK = 128
DN_BASE = 32
DN_PARTS = 2
KPAIR = 2 * CHUNK
QBLK = KPAIR
BISECT_STEPS = 19

C_QKV = 0
C_Z = DN_CONV_CH
C_SAQ = C_Z + DN_V
C_IXQ = C_SAQ + SA_HEADS * SA_HEAD_DIM
C_KV = C_IXQ + IDX_HEADS * IDX_DIM
C_MISC = C_KV + 2 * SA_HEAD_DIM
PROJ_W = C_MISC + LANES
M_IXK = 0
M_B = IDX_DIM
M_A = M_B + DN_HEADS
M_W = M_A + DN_HEADS

ROT = SA_HEAD_DIM // ROPE_FRACTION
HALF = ROT // 2

VMEM_LIMIT = 56 * 1024 * 1024
TOKEN_TILE = 1024
PROJ_COLS = 512
FF_COLS = 256


def _const_spec(shape):
    nd = len(shape)
    return pl.BlockSpec(shape, lambda *_: (0,) * nd, pipeline_mode=pl.Buffered(1))


def _rms(x, g):
    return x * lax.rsqrt(jnp.mean(x * x, axis=-1, keepdims=True) + RMS_EPS) * g


def _mm(a, b):
    return jnp.dot(a.astype(BF16), b.astype(BF16), preferred_element_type=F32)


def _mm_nt(a, b):
    return lax.dot_general(a.astype(BF16), b.astype(BF16), (((1,), (1,)), ((), ())),
                           preferred_element_type=F32)


def _weave(*gens):
    live = list(gens)
    while live:
        live = [g for g in live if next(g, True) is None]


def _sigmoid(x):
    return 1.0 / (1.0 + jnp.exp(-x))


def _silu(x):
    return x * _sigmoid(x)


def _softplus(x):
    return jnp.maximum(x, 0.0) + jnp.log1p(jnp.exp(-jnp.abs(x)))


def _in_proj_kernel(x_ref, g_ref, w_ref, o_ref):
    hb = _rms(x_ref[...], g_ref[...]).astype(BF16)
    for n in range(0, PROJ_W, PROJ_COLS):
        o_ref[:, n:n + PROJ_COLS] = jnp.dot(hb, w_ref[:, n:n + PROJ_COLS],
                                            preferred_element_type=F32)


def _in_proj(x2d, gain, w_perm, tm):
    t = x2d.shape[0]
    return pl.pallas_call(
        _in_proj_kernel,
        grid=(t // tm,),
        in_specs=[pl.BlockSpec((tm, D_MODEL), lambda i: (i, 0)),
                  _const_spec((1, D_MODEL)),
                  _const_spec((D_MODEL, PROJ_W))],
        out_specs=pl.BlockSpec((tm, PROJ_W), lambda i: (i, 0)),
        out_shape=jax.ShapeDtypeStruct((t, PROJ_W), F32),
        compiler_params=pltpu.CompilerParams(dimension_semantics=("arbitrary",),
                                             vmem_limit_bytes=VMEM_LIMIT),
        name="in_proj",
    )(x2d, gain, w_perm)


def _dn_kernel(q_ref, k_ref, v_ref, z_ref, misc_ref, cwq_ref, cwk_ref, cwv_ref, alog_ref, dtb_ref,
               ng_ref, o_ref,
               pad_s, cq_s, ck_s, cv_s, brow_s, grow_s, u_s, w_s, a_s, qg_s, c_s, n_s, el_s):
    seq = q_ref.shape[1]
    nc = seq // CHUNK
    h = pl.program_id(1)
    row = lax.broadcasted_iota(I32, (CHUNK, CHUNK), 0)
    col = lax.broadcasted_iota(I32, (CHUNK, CHUNK), 1)

    def conv(chunks):
        for n in chunks:
            for src, cw_ref, dst, kind in ((q_ref, cwq_ref, cq_s, "q"), (k_ref, cwk_ref, ck_s, "k"),
                                           (v_ref, cwv_ref, cv_s, "v")):
                if n == 0:
                    pad_s[0:SUBLANES, :] = jnp.zeros((SUBLANES, LANES), F32)
                    pad_s[SUBLANES:SUBLANES + CHUNK, :] = src[0, 0:CHUNK, :]
                    tap = lambda j: pad_s[SUBLANES - j:SUBLANES - j + CHUNK, :]
                else:
                    tap = lambda j, src=src: src[0, n * CHUNK - j:(n + 1) * CHUNK - j, :]
                y = cw_ref[DN_CONV - 1:DN_CONV, :] * tap(0)
                for j in range(1, DN_CONV):
                    y = y + cw_ref[DN_CONV - 1 - j:DN_CONV - j, :] * tap(j)
                y = _silu(y)
                if kind != "v":
                    y = y * lax.rsqrt(jnp.sum(y * y, axis=-1, keepdims=True) + 1e-6)
                if kind == "q":
                    y = y * (DN_DK ** -0.5)
                dst[n * CHUNK:(n + 1) * CHUNK, :] = y
            yield

    upper = jnp.where(row <= col, 1.0, 0.0).astype(F32)
    sub8 = lax.broadcasted_iota(I32, (SUBLANES, LANES), 0)
    for n in range(nc):
        mt = misc_ref[0, n * CHUNK:(n + 1) * CHUNK, :].T
        slab = mt[M_B:M_B + SUBLANES, :]
        beta8 = _sigmoid(slab)
        g8 = -jnp.exp(alog_ref[...]) * _softplus(slab + dtb_ref[...])
        gc8 = jnp.dot(g8, upper, preferred_element_type=F32, precision=lax.Precision.HIGHEST)
        b_row = jnp.sum(jnp.where(sub8 == h, beta8, 0.0), axis=0, keepdims=True)
        g_row = jnp.sum(jnp.where(sub8 == h + DN_HEADS, gc8, 0.0), axis=0, keepdims=True)
        brow_s[n] = jnp.broadcast_to(b_row, (SUBLANES, LANES))
        grow_s[n] = jnp.broadcast_to(g_row, (SUBLANES, LANES))

    eye = jnp.where(row == col, 1.0, 0.0).astype(F32)

    def prep_one(n):
        rows = slice(n * CHUNK, (n + 1) * CHUNK)
        q = cq_s[rows, :]
        k = ck_s[rows, :]
        v = cv_s[rows, :]
        g_r = jnp.broadcast_to(grow_s[n][0:1, :], (CHUNK, CHUNK))
        g_c = g_r.T
        b_c = jnp.broadcast_to(brow_s[n][0:1, :], (CHUNK, CHUNK)).T
        g_last = jnp.broadcast_to(g_c[CHUNK - 1:CHUNK, :], (CHUNK, CHUNK))
        decay = jnp.where(row >= col, jnp.exp(jnp.minimum(g_c - g_r, 0.0)), 0.0)
        kb = k * b_c
        kq = _mm_nt(jnp.concatenate([kb, q], axis=0), k)
        yield
        lmat = jnp.where(row > col, kq[:CHUNK] * decay, 0.0)
        a = jnp.where(row >= col, kq[CHUNK:] * decay, 0.0)
        same = lambda b: (row // b) == (col // b)
        l0 = jnp.where(same(DN_BASE), lmat, 0.0)
        x = eye - l0
        m = _mm(l0, l0)
        yield
        power = 2
        while 2 * power < DN_BASE:
            xm = _mm(jnp.concatenate([x, m], axis=0), m)
            yield
            x = x + xm[:CHUNK]
            m = xm[CHUNK:]
            power *= 2
        x = x + _mm(x, m)
        yield
        blk = DN_BASE
        while blk < CHUNK:
            below = jnp.where(same(2 * blk) & jnp.logical_not(same(blk)), lmat, 0.0)
            y = _mm(below, x)
            yield
            x = x - _mm(x, y)
            yield
            blk *= 2
        eg = jnp.exp(g_c)
        sol = _mm(x, jnp.concatenate([v * b_c, kb * eg], axis=1))
        yield
        cn = _mm((k * jnp.exp(g_last - g_c)).T, sol)
        yield
        u_s[n] = sol[:, :DN_DV]
        w_s[n] = sol[:, DN_DV:]
        a_s[n] = a
        qg_s[n] = q * eg
        c_s[n] = cn[:, :DN_DV]
        n_s[n] = cn[:, DN_DV:]
        el_s[n] = jnp.exp(g_last)[0:SUBLANES, :]

    carried = [jnp.zeros((DN_DK, DN_DV), F32)]

    def recur(chunks):
        for n in chunks:
            state = carried[0]
            el = jnp.broadcast_to(el_s[n][0:1, :], (CHUNK, CHUNK))
            nws = _mm(jnp.concatenate([n_s[n], w_s[n], qg_s[n]], axis=0), state)
            yield
            carried[0] = state * el + c_s[n] - nws[:CHUNK]
            o = nws[2 * CHUNK:] + _mm(a_s[n], u_s[n] - nws[CHUNK:2 * CHUNK])
            yield
            rows = slice(n * CHUNK, (n + 1) * CHUNK)
            o_ref[0, rows, :] = (_rms(o, ng_ref[...]) * _silu(z_ref[0, rows, :])).astype(o_ref.dtype)

    parts = [range(i * nc // DN_PARTS, (i + 1) * nc // DN_PARTS) for i in range(DN_PARTS)]
    for i in range(DN_PARTS + 2):
        stage = []
        if i < DN_PARTS:
            stage.append(conv(parts[i]))
        if 1 <= i <= DN_PARTS:
            stage.extend(prep_one(n) for n in parts[i - 1])
        if i >= 2:
            stage.append(recur(parts[i - 2]))
        _weave(*stage)


def _deltanet(proj, conv_w, alog8, dtb8, norm_g):
    bsz, seq, _ = proj.shape
    nc = seq // CHUNK
    assert seq % (DN_PARTS * CHUNK) == 0
    qb, kb, vb, zb = (C_QKV // LANES, (C_QKV + DN_QK) // LANES, (C_QKV + 2 * DN_QK) // LANES,
                      C_Z // LANES)
    col = lambda off: pl.BlockSpec((1, seq, LANES), lambda b, h, off=off: (b, 0, off + h))
    cw = lambda off: pl.BlockSpec((DN_CONV, LANES), lambda b, h, off=off: (0, off + h))
    mat = lambda: pltpu.VMEM((nc, CHUNK, CHUNK), F32)
    return pl.pallas_call(
        _dn_kernel,
        grid=(bsz, DN_HEADS),
        in_specs=[col(qb), col(kb), col(vb), col(zb),
                  pl.BlockSpec((1, seq, LANES), lambda b, h: (b, 0, C_MISC // LANES)),
                  cw(qb), cw(kb), cw(vb),
                  _const_spec((SUBLANES, LANES)), _const_spec((SUBLANES, LANES)),
                  _const_spec((1, DN_DV))],
        out_specs=pl.BlockSpec((1, seq, LANES), lambda b, h: (b, 0, h)),
        out_shape=jax.ShapeDtypeStruct((bsz, seq, DN_V), BF16),
        scratch_shapes=[pltpu.VMEM((SUBLANES + CHUNK, LANES), F32),
                        pltpu.VMEM((seq, LANES), F32), pltpu.VMEM((seq, LANES), F32),
                        pltpu.VMEM((seq, LANES), F32),
                        pltpu.VMEM((nc, SUBLANES, LANES), F32), pltpu.VMEM((nc, SUBLANES, LANES), F32),
                        mat(), mat(), mat(), mat(), mat(), mat(),
                        pltpu.VMEM((nc, SUBLANES, LANES), F32)],
        compiler_params=pltpu.CompilerParams(dimension_semantics=("arbitrary", "arbitrary"),
                                             vmem_limit_bytes=VMEM_LIMIT),
        name="deltanet",
    )(proj, proj, proj, proj, proj, conv_w, conv_w, conv_w, alog8, dtb8, norm_g)


def _even_odd(n):
    return list(range(0, n, 2)) + list(range(1, n, 2))


def _rope(x, c, s, s1, s2):
    w = x.shape[-1]
    return x * c + pltpu.roll(x, w - HALF, 1) * (s * s1) + pltpu.roll(x, HALF, 1) * (s * s2)


def _dsa_kernel(topk, q_ref, qi_ref, misct_ref, kv_ref, miscf_ref, pos_ref, qg_ref, kg_ref, rp_ref,
                hsum_ref, o_ref,
                cos_s, sin_s, ka_s, kb_s, vt_s, kia_s, kib_s, q2_s, qi2_s, sc_s, lg_s, pa_s,
                pb_s, l8_s, acc_s):
    seq = kv_ref.shape[1]
    nc = seq // CHUNK
    qb = pl.program_id(1)
    npair = qb + 1
    r0 = pl.multiple_of(qb * QBLK, QBLK)
    col = lax.broadcasted_iota(I32, (CHUNK, CHUNK), 1)
    s1 = rp_ref[1:2, :]
    s2 = rp_ref[2:3, :]

    @pl.when(qb == 0)
    def _():
        ang = pos_ref[0].astype(F32) * rp_ref[0:1, :]
        packed = (jnp.cos(ang), jnp.sin(ang))
        lane = lax.broadcasted_iota(I32, ang.shape, 1)
        second = (lane >= SA_HEAD_DIM) & (lane < SA_HEAD_DIM + ROT)
        per_row = LANES // ROT
        for j in range(per_row):
            for tab, dst, rest in zip(packed, (cos_s, sin_s), (1.0, 0.0)):
                mine = tab if j == 0 else pltpu.roll(tab, LANES - j * ROT, 1)
                full = jnp.where(lane < ROT, mine,
                                 jnp.where(second, pltpu.roll(mine, SA_HEAD_DIM, 1), rest))
                dst[pl.ds(j, seq // per_row, stride=per_row), :] = full
        for n in range(nc):
            rows = slice(n * CHUNK, (n + 1) * CHUNK)
            c = cos_s[rows, :]
            s = sin_s[rows, :]
            ck = jnp.where(col < ROT, c, 1.0)
            sk = jnp.where(col < ROT, s, 0.0)
            kv = kv_ref[0, rows, :]
            ss = jnp.dot((kv * kv).astype(BF16), hsum_ref[0:LANES, 0:LANES],
                         preferred_element_type=F32)
            r = lax.rsqrt(ss * (1.0 / SA_HEAD_DIM) + RMS_EPS)
            y = jnp.where(col < SA_HEAD_DIM, kv * r * kg_ref[...], kv)
            y = _rope(y, ck, sk, s1, s2)
            ka = jnp.where(col < SA_HEAD_DIM, y, 0.0)
            ka_s[rows, :] = ka.astype(BF16)
            kb_s[rows, :] = pltpu.roll(ka, SA_HEAD_DIM, 1).astype(BF16)
            half = (n % 2) * CHUNK
            vt_s[n // 2, :, half:half + CHUNK] = y.T[SA_HEAD_DIM:, :].astype(BF16)
            yi = _rope(miscf_ref[0, rows, :], ck, sk, s1, s2)
            kia = jnp.where(col < IDX_DIM, yi, 0.0)
            kia_s[rows, :] = kia.astype(BF16)
            kib_s[rows, :] = pltpu.roll(kia, IDX_DIM, 1).astype(BF16)

    c1 = cos_s[pl.ds(r0, QBLK), :]
    sn1 = sin_s[pl.ds(r0, QBLK), :]
    rep = lambda a, k: jnp.concatenate([a] * k, axis=1)
    nq = SA_HEADS * SA_HEAD_DIM // LANES
    xq = q_ref[0]
    yq = _rope(xq * qg_ref[...], rep(c1, nq), rep(sn1, nq), rep(s1, nq), rep(s2, nq))
    ss = jnp.dot((xq * xq).astype(BF16), hsum_ref[...], preferred_element_type=F32)
    r = lax.rsqrt(ss * (1.0 / SA_HEAD_DIM) + RMS_EPS)
    qn = (yq * (r * (SA_HEAD_DIM ** -0.5 * LOG2E))).astype(BF16)
    for pr in range(nq):
        q2_s[pr * QBLK:(pr + 1) * QBLK, :] = qn[:, pr * LANES:(pr + 1) * LANES]
    ni = IDX_HEADS * IDX_DIM // LANES
    yi = _rope(qi_ref[0], rep(c1, ni), rep(sn1, ni), rep(s1, ni), rep(s2, ni)).astype(BF16)
    for pr in range(ni):
        qi2_s[pr * QBLK:(pr + 1) * QBLK, :] = yi[:, pr * LANES:(pr + 1) * LANES]
    mt = misct_ref[0].T
    wscale = (IDX_HEADS ** -0.5) * (IDX_DIM ** -0.5)
    wrow = jnp.concatenate([mt[M_W + hh:M_W + hh + 1, :] for hh in _even_odd(IDX_HEADS)],
                           axis=1) * wscale

    krow = lax.broadcasted_iota(I32, (KPAIR, QBLK), 0)
    qpos = r0 + lax.broadcasted_iota(I32, (KPAIR, QBLK), 1)
    big = -NEG_INF

    def fold8(x, op):
        parts = [x[i:i + SUBLANES] for i in range(0, x.shape[0], SUBLANES)]
        while len(parts) > 1:
            parts = [op(parts[i], parts[i + 1]) for i in range(0, len(parts), 2)]
        return parts[0]

    def idx_pair(c, bounds):
        k0 = pl.multiple_of(c * KPAIR, KPAIR)
        terms = []
        for keys_s in (kia_s, kib_s):
            keys = keys_s[pl.ds(k0, KPAIR), :]
            for pr in range(IDX_HEADS // 2):
                lt = _mm_nt(keys, qi2_s[pr * QBLK:(pr + 1) * QBLK, :])
                yield
                j = len(terms)
                terms.append(jnp.maximum(lt, 0.0) * wrow[:, j * QBLK:(j + 1) * QBLK])
        acc = (terms[0] + terms[1]) + (terms[2] + terms[3])
        causal = k0 + krow <= qpos
        sc = jnp.where(causal, acc, NEG_INF)
        sc_s[c] = sc
        bounds.append((fold8(jnp.where(causal, acc, big), jnp.minimum), fold8(sc, jnp.maximum)))

    def idx_pairs(pairs, carry):
        bounds = []
        _weave(*[idx_pair(c, bounds) for c in pairs])
        lo8, hi8 = carry
        for lo_c, hi_c in bounds:
            lo8 = jnp.minimum(lo8, lo_c)
            hi8 = jnp.maximum(hi8, hi_c)
        return lo8, hi8

    nsteps = npair // 2
    odd = npair % 2 == 1
    lo8, hi8 = lax.fori_loop(0, nsteps, lambda i, carry: idx_pairs((2 * i, 2 * i + 1), carry),
                             (jnp.full((SUBLANES, QBLK), big, F32),
                              jnp.full((SUBLANES, QBLK), NEG_INF, F32)))
    lo8, hi8 = lax.cond(odd, lambda carry: idx_pairs((npair - 1,), carry), lambda carry: carry,
                        (lo8, hi8))
    lo = jnp.min(lo8, axis=0, keepdims=True)
    hi = jnp.max(hi8, axis=0, keepdims=True)

    def count(pred):
        one = lambda c: fold8(jnp.where(pred(sc_s[c]), 1.0, 0.0), jnp.add)
        acc = lax.fori_loop(0, nsteps, lambda i, acc: acc + (one(2 * i) + one(2 * i + 1)),
                            jnp.zeros((SUBLANES, QBLK), F32))
        acc = lax.cond(odd, lambda acc: acc + one(npair - 1), lambda acc: acc, acc)
        return jnp.sum(acc, axis=0, keepdims=True)

    def max_where(pred):
        def body(c, acc):
            s = sc_s[c]
            return jnp.maximum(acc, fold8(jnp.where(pred(s), s, NEG_INF), jnp.maximum))
        acc = lax.fori_loop(0, npair, body, jnp.full((SUBLANES, QBLK), NEG_INF, F32))
        return jnp.max(acc, axis=0, keepdims=True)

    kf = jnp.minimum(float(topk), (qpos[0:1, :] + 1).astype(F32))

    def bisect(_, bounds):
        lo, hi = bounds
        mid = lo + 0.5 * (hi - lo)
        above = count(lambda s: s > mid) >= kf
        return jnp.where(above, mid, lo), jnp.where(above, hi, mid)

    lo, hi = lax.fori_loop(0, BISECT_STEPS, bisect, (lo, hi))

    thr = max_where(lambda s: s <= hi)
    cge = count(lambda s: s >= thr)

    def short(cge):
        return jnp.max(jnp.where(cge < kf, 1.0, 0.0)) > 0.5

    def peel(state):
        thr, cge, _ = state
        nxt = max_where(lambda s: s < thr)
        cnx = count(lambda s: s >= nxt)
        step = cge < kf
        thr = jnp.where(step, nxt, thr)
        cge = jnp.where(step, cnx, cge)
        return thr, cge, short(cge)

    thr, cge, _ = lax.while_loop(lambda st: st[2], peel, (thr, cge, short(cge)))
    need = kf - count(lambda s: s > thr)

    lower = jnp.where(lax.broadcasted_iota(I32, (KPAIR, KPAIR), 0)
                      >= lax.broadcasted_iota(I32, (KPAIR, KPAIR), 1), 1.0, 0.0).astype(BF16)

    def sel_pair(c, shared):
        k0 = pl.multiple_of(c * KPAIR, KPAIR)
        s = sc_s[c]
        eqf = jnp.where(s == thr, 1.0, 0.0)
        pref = jnp.dot(lower, eqf.astype(BF16), preferred_element_type=F32)
        yield
        off = shared["off"]
        shared["off"] = off + pref[KPAIR - 1:KPAIR, :]
        tie = jnp.where(pref + off <= need, eqf, 0.0)
        keep = jnp.where(k0 + krow <= qpos, jnp.where(s > thr, 1.0, tie), 0.0)
        bias = jnp.where(keep > 0.5, 0.0, NEG_INF)
        tops = []
        for keys_s in (ka_s, kb_s):
            keys = keys_s[pl.ds(k0, KPAIR), :]
            for pr in range(SA_HEADS // 2):
                lg = _mm_nt(keys, q2_s[pr * QBLK:(pr + 1) * QBLK, :])
                yield
                lg = lg + bias
                lg_s[c, :, len(tops) * QBLK:(len(tops) + 1) * QBLK] = lg
                tops.append(fold8(lg, jnp.maximum))
        shared["m8"] = jnp.maximum(shared["m8"], jnp.concatenate(tops, axis=1))

    def sel_pairs(pairs, carry):
        shared = {"off": carry[0], "m8": carry[1]}
        _weave(*[sel_pair(c, shared) for c in pairs])
        return shared["off"], shared["m8"]

    carry = lax.fori_loop(0, nsteps, lambda i, carry: sel_pairs((2 * i, 2 * i + 1), carry),
                          (jnp.zeros((1, QBLK), F32),
                           jnp.full((SUBLANES, SA_HEADS * QBLK), NEG_INF, F32)))
    _, m8 = lax.cond(odd, lambda carry: sel_pairs((npair - 1,), carry), lambda carry: carry, carry)
    mrow = jnp.max(m8, axis=0, keepdims=True)

    def probs(c, dst):
        slab = 2 * SUBLANES
        l8 = l8_s[...]
        for r in range(0, KPAIR, slab):
            p = jnp.exp2(lg_s[c, r:r + slab, :] - mrow)
            dst[r:r + slab, :] = p.astype(BF16)
            l8 = l8 + (p[:SUBLANES] + p[SUBLANES:])
        l8_s[...] = l8

    def pv_acc(c, src):
        acc_s[...] += jnp.dot(vt_s[c], src[...], preferred_element_type=F32)

    acc_s[...] = jnp.zeros(acc_s.shape, F32)
    l8_s[...] = jnp.zeros(l8_s.shape, F32)
    probs(0, pa_s)

    def p2(i, carry):
        c = 2 * i + 1
        pv_acc(c - 1, pa_s)
        probs(c, pb_s)

        @pl.when(c + 1 < npair)
        def _():
            pv_acc(c, pb_s)
            probs(c + 1, pa_s)
        return carry

    lax.fori_loop(0, npair // 2, p2, 0)
    last = npair - 1

    @pl.when(last % 2 == 0)
    def _():
        pv_acc(last, pa_s)

    @pl.when(last % 2 == 1)
    def _():
        pv_acc(last, pb_s)

    out_t = acc_s[...] / jnp.sum(l8_s[...], axis=0, keepdims=True)
    order = _even_odd(SA_HEADS)
    for pr in range(SA_HEADS // 2):
        ev, od = order.index(2 * pr), order.index(2 * pr + 1)
        two = jnp.concatenate([out_t[:, ev * QBLK:(ev + 1) * QBLK],
                               out_t[:, od * QBLK:(od + 1) * QBLK]], axis=0)
        o_ref[0, :, pr * LANES:(pr + 1) * LANES] = two.T.astype(o_ref.dtype)


def _dsa(proj, pos3, qg_t, kg_p, rope_pat, topk):
    bsz, seq, _ = proj.shape
    nc = seq // CHUNK
    qw = SA_HEADS * SA_HEAD_DIM
    iw = IDX_HEADS * IDX_DIM
    assert seq % QBLK == 0 and QBLK == KPAIR
    npairs = seq // KPAIR
    pairs = lambda: pltpu.VMEM((npairs, KPAIR, QBLK), F32)
    head_of = jnp.arange(qw) // SA_HEAD_DIM
    head_sum = (head_of[:, None] == head_of[None, :]).astype(BF16)
    return pl.pallas_call(
        functools.partial(_dsa_kernel, topk),
        grid=(bsz, seq // QBLK),
        in_specs=[pl.BlockSpec((1, QBLK, qw), lambda b, j: (b, j, C_SAQ // qw)),
                  pl.BlockSpec((1, QBLK, iw), lambda b, j: (b, j, C_IXQ // iw)),
                  pl.BlockSpec((1, QBLK, LANES), lambda b, j: (b, j, C_MISC // LANES)),
                  pl.BlockSpec((1, seq, LANES), lambda b, j: (b, 0, C_KV // LANES)),
                  pl.BlockSpec((1, seq, LANES), lambda b, j: (b, 0, C_MISC // LANES)),
                  pl.BlockSpec((1, seq * ROT // LANES, LANES), lambda b, j: (b, 0, 0)),
                  _const_spec((1, qw)), _const_spec((1, LANES)), _const_spec((SUBLANES, LANES)),
                  _const_spec((qw, qw))],
        out_specs=pl.BlockSpec((1, QBLK, qw), lambda b, j: (b, j, 0)),
        out_shape=jax.ShapeDtypeStruct((bsz, seq, qw), BF16),
        scratch_shapes=[pltpu.VMEM((seq, LANES), F32), pltpu.VMEM((seq, LANES), F32),
                        pltpu.VMEM((seq, LANES), BF16), pltpu.VMEM((seq, LANES), BF16),
                        pltpu.VMEM((npairs, SA_HEAD_DIM, KPAIR), BF16),
                        pltpu.VMEM((seq, LANES), BF16), pltpu.VMEM((seq, LANES), BF16),
                        pltpu.VMEM((qw // LANES * QBLK, LANES), BF16),
                        pltpu.VMEM((iw // LANES * QBLK, LANES), BF16),
                        pairs(),
                        pltpu.VMEM((npairs, KPAIR, SA_HEADS * QBLK), F32),
                        pltpu.VMEM((KPAIR, SA_HEADS * QBLK), BF16),
                        pltpu.VMEM((KPAIR, SA_HEADS * QBLK), BF16),
                        pltpu.VMEM((SUBLANES, SA_HEADS * QBLK), F32),
                        pltpu.VMEM((SA_HEAD_DIM, SA_HEADS * QBLK), F32)],
        compiler_params=pltpu.CompilerParams(dimension_semantics=("arbitrary", "arbitrary"),
                                             vmem_limit_bytes=VMEM_LIMIT),
        name="dsa",
    )(proj, proj, proj, proj, proj, pos3, qg_t, kg_p, rope_pat, head_sum)


def _merge_kernel(x_ref, oa_ref, ob_ref, g_ref, wg_ref, bg_ref, wa_ref, wb_ref, wo_ref, o_ref):
    x = x_ref[...]
    hb = _rms(x, g_ref[...]).astype(BF16)
    oa = oa_ref[...]
    ob = ob_ref[...]
    acc = x
    step = PROJ_COLS
    for n in range(0, D_MODEL, step):
        ga = _sigmoid(jnp.dot(hb, wg_ref[:, n:n + step], preferred_element_type=F32)
                      + bg_ref[:, n:n + step])
        gb = _sigmoid(jnp.dot(hb, wg_ref[:, D_MODEL + n:D_MODEL + n + step],
                              preferred_element_type=F32)
                      + bg_ref[:, D_MODEL + n:D_MODEL + n + step])
        ya = jnp.dot(oa, wa_ref[:, n:n + step], preferred_element_type=F32)
        yb = jnp.dot(ob, wb_ref[:, n:n + step], preferred_element_type=F32)
        merged = (ga * ya + gb * yb).astype(BF16)
        acc = acc + jnp.dot(merged, wo_ref[n:n + step, :], preferred_element_type=F32)
    o_ref[...] = acc


def _merge(x2d, oa2d, ob2d, gain, w_gate, b_gate, w_a, w_b, w_o, tm):
    t = x2d.shape[0]
    tok = lambda w: pl.BlockSpec((tm, w), lambda i: (i, 0))
    return pl.pallas_call(
        _merge_kernel,
        grid=(t // tm,),
        in_specs=[tok(D_MODEL), tok(DN_V), tok(SA_HEADS * SA_HEAD_DIM),
                  _const_spec((1, D_MODEL)), _const_spec((D_MODEL, 2 * D_MODEL)),
                  _const_spec((1, 2 * D_MODEL)), _const_spec((DN_V, D_MODEL)),
                  _const_spec((SA_HEADS * SA_HEAD_DIM, D_MODEL)), _const_spec((D_MODEL, D_MODEL))],
        out_specs=tok(D_MODEL),
        out_shape=jax.ShapeDtypeStruct((t, D_MODEL), F32),
        compiler_params=pltpu.CompilerParams(dimension_semantics=("arbitrary",),
                                             vmem_limit_bytes=VMEM_LIMIT),
        name="merge",
    )(x2d, oa2d, ob2d, gain, w_gate, b_gate, w_a, w_b, w_o)


def _ffn_kernel(x_ref, p_ref, g2_ref, wg_ref, wu_ref, wd_ref, g3_ref, wpg_ref, wp_ref, o_ref):
    x = x_ref[...]
    hb = _rms(x, g2_ref[...]).astype(BF16)
    acc = x
    step = FF_COLS
    for n in range(0, D_FF, step):
        gate = jnp.dot(hb, wg_ref[:, n:n + step], preferred_element_type=F32)
        up = jnp.dot(hb, wu_ref[:, n:n + step], preferred_element_type=F32)
        acc = acc + jnp.dot((_silu(gate) * up).astype(BF16), wd_ref[n:n + step, :],
                            preferred_element_type=F32)
    h3 = _rms(acc, g3_ref[...]).astype(BF16)
    pg = _sigmoid(jnp.dot(h3, wpg_ref[...], preferred_element_type=F32))
    pe = jnp.dot(p_ref[...].astype(BF16), wp_ref[...], preferred_element_type=F32)
    o_ref[...] = acc + pg * pe


def _ffn(x2d, p2d, g2, w_g, w_u, w_d, g3, w_pg, w_p, tm):
    t = x2d.shape[0]
    tok = lambda w: pl.BlockSpec((tm, w), lambda i: (i, 0))
    return pl.pallas_call(
        _ffn_kernel,
        grid=(t // tm,),
        in_specs=[tok(D_MODEL), tok(PLE_DIM), _const_spec((1, D_MODEL)),
                  _const_spec((D_MODEL, D_FF)), _const_spec((D_MODEL, D_FF)),
                  _const_spec((D_FF, D_MODEL)), _const_spec((1, D_MODEL)),
                  _const_spec((D_MODEL, D_MODEL)), _const_spec((PLE_DIM, D_MODEL))],
        out_specs=tok(D_MODEL),
        out_shape=jax.ShapeDtypeStruct((t, D_MODEL), F32),
        compiler_params=pltpu.CompilerParams(dimension_semantics=("arbitrary",),
                                             vmem_limit_bytes=VMEM_LIMIT),
        name="ffn_ple",
    )(x2d, p2d, g2, w_g, w_u, w_d, g3, w_pg, w_p)


def _regroup_w_in(w):
    sizes = (DN_CONV_CH, DN_V, DN_HEADS, DN_HEADS, SA_HEADS * SA_HEAD_DIM, SA_HEAD_DIM, SA_HEAD_DIM,
             IDX_HEADS * IDX_DIM, IDX_DIM, IDX_HEADS)
    parts, off = [], 0
    for s in sizes:
        parts.append(w[:, off:off + s])
        off += s
    qkv, z, b, a, saq, sak, sav, ixq, ixk, ixw = parts
    cols = [qkv, z, saq, ixq, sak, sav, ixk, b, a, ixw]
    used = sum(c.shape[1] for c in cols)
    cols.append(jnp.zeros((w.shape[0], PROJ_W - used), w.dtype))
    return jnp.concatenate(cols, axis=1)


def _rope_pattern():
    lane = jnp.arange(LANES) % SA_HEAD_DIM
    inv_freq = ROPE_THETA ** (-jnp.arange(0, ROT, 2, dtype=F32) / ROT)
    freq = inv_freq[jnp.arange(LANES) % HALF]
    s1 = jnp.where(lane < HALF, -1.0, 0.0)
    s2 = jnp.where((lane >= HALF) & (lane < ROT), 1.0, 0.0)
    pat = jnp.zeros((SUBLANES, LANES), F32)
    return pat.at[0].set(freq).at[1].set(s1).at[2].set(s2)


def _layer(x, p, positions, attn_norm, w_in, conv_w, a_log, dt_bias, dn_norm, q_norm, k_norm,
           w_branch_a, w_branch_b, w_gate, b_gate, w_o, ffn_norm, w_ffn_gate, w_ffn_up, w_ffn_down,
           ple_norm, w_ple_gate, w_ple):
    bsz, seq, _ = x.shape
    t = bsz * seq
    tm = TOKEN_TILE
    assert t % tm == 0
    topk = min(IDX_TOPK_MAX, seq // 4)
    row = lambda v: v.reshape(1, -1).astype(F32)
    x2d = x.reshape(t, D_MODEL)

    proj = _in_proj(x2d, row(attn_norm), _regroup_w_in(w_in).astype(BF16), tm)
    proj = proj.reshape(bsz, seq, PROJ_W)

    pad4 = jnp.zeros((DN_HEADS,), F32)
    per_head = lambda v: jnp.broadcast_to(jnp.concatenate([pad4, v.astype(F32)])[:, None],
                                          (SUBLANES, LANES))
    o_a = _deltanet(proj, conv_w.astype(F32), per_head(a_log), per_head(dt_bias), row(dn_norm))

    qg_t = jnp.tile(q_norm.astype(F32), SA_HEADS).reshape(1, -1)
    kg_p = jnp.concatenate([k_norm.astype(F32), jnp.ones((LANES - SA_HEAD_DIM,), F32)]).reshape(1, -1)
    per_row = LANES // ROT
    pos_packed = jnp.repeat(positions.astype(I32).reshape(bsz, seq // per_row, per_row), ROT, axis=2)
    o_b = _dsa(proj, pos_packed, qg_t, kg_p, _rope_pattern(), topk)

    x1 = _merge(x2d, o_a.reshape(t, DN_V), o_b.reshape(t, -1), row(attn_norm),
                w_gate.astype(BF16), row(b_gate), w_branch_a.astype(BF16), w_branch_b.astype(BF16),
                w_o.astype(BF16), tm)
    x2 = _ffn(x1, p.reshape(t, PLE_DIM), row(ffn_norm), w_ffn_gate.astype(BF16),
              w_ffn_up.astype(BF16), w_ffn_down.astype(BF16), row(ple_norm),
              w_ple_gate.astype(BF16), w_ple.astype(BF16), tm)
    return x2.reshape(bsz, seq, D_MODEL)


def kernel(x, p, positions, attn_norm, w_in, conv_w, a_log, dt_bias, dn_norm, q_norm, k_norm,
           w_branch_a, w_branch_b, w_gate, b_gate, w_o, ffn_norm, w_ffn_gate, w_ffn_up, w_ffn_down,
           ple_norm, w_ple_gate, w_ple):
    depth = w_in.shape[0]
    for i in range(depth):
        x = _layer(x, p[i], positions, attn_norm[i], w_in[i], conv_w[i], a_log[i], dt_bias[i],
                   dn_norm[i], q_norm[i], k_norm[i], w_branch_a[i], w_branch_b[i], w_gate[i],
                   b_gate[i], w_o[i], ffn_norm[i], w_ffn_gate[i], w_ffn_up[i], w_ffn_down[i],
                   ple_norm[i], w_ple_gate[i], w_ple[i])
    return x
```

```python
import functools

import jax
import jax.numpy as jnp
from jax import lax
from jax.experimental import pallas as pl
from jax.experimental.pallas import tpu as pltpu

F32 = jnp.float32
BF16 = jnp.bfloat16
I32 = jnp.int32

D_MODEL = 1024
PLE_DIM = 256
RMS_EPS = 1e-6
DN_HEADS = 4
DN_DK = 128
DN_DV = 128
DN_CONV = 4
DN_QK = DN_HEADS * DN_DK
DN_V = DN_HEADS * DN_DV
DN_CONV_CH = 2 * DN_QK + DN_V
SA_HEADS = 8
SA_HEAD_DIM = 64
IDX_HEADS = 4
IDX_DIM = 64
IDX_TOPK_MAX = 256
ROPE_THETA = 500000.0
ROPE_FRACTION = 4
NEG_INF = -1e30
D_FF = 2816
LOG2E = 1.4426950408889634

LANES = 128
SUBLANES = 8
CHUNK = 128
DN_BASE = 32
DN_PARTS = 2
KPAIR = 2 * CHUNK
QBLK = KPAIR
BISECT_STEPS = 19

C_QKV = 0
C_Z = DN_CONV_CH
C_SAQ = C_Z + DN_V
C_IXQ = C_SAQ + SA_HEADS * SA_HEAD_DIM
C_KV = C_IXQ + IDX_HEADS * IDX_DIM
C_MISC = C_KV + 2 * SA_HEAD_DIM
PROJ_W = C_MISC + LANES
M_IXK = 0
M_B = IDX_DIM
M_A = M_B + DN_HEADS
M_W = M_A + DN_HEADS

ROT = SA_HEAD_DIM // ROPE_FRACTION
HALF = ROT // 2

VMEM_LIMIT = 56 * 1024 * 1024
TOKEN_TILE = 1024
PROJ_COLS = 512
FF_COLS = 256


def _const_spec(shape):
    nd = len(shape)
    return pl.BlockSpec(shape, lambda *_: (0,) * nd, pipeline_mode=pl.Buffered(1))


def _rms(x, g):
    return x * lax.rsqrt(jnp.mean(x * x, axis=-1, keepdims=True) + RMS_EPS) * g


def _mm(a, b):
    return jnp.dot(a.astype(BF16), b.astype(BF16), preferred_element_type=F32)


def _mm_nt(a, b):
    return lax.dot_general(a.astype(BF16), b.astype(BF16), (((1,), (1,)), ((), ())),
                           preferred_element_type=F32)


def _weave(*gens):
    live = list(gens)
    while live:
        live = [g for g in live if next(g, True) is None]


def _sigmoid(x):
    return 1.0 / (1.0 + jnp.exp(-x))


def _silu(x):
    return x * _sigmoid(x)


def _softplus(x):
    return jnp.maximum(x, 0.0) + jnp.log1p(jnp.exp(-jnp.abs(x)))


def _in_proj_kernel(x_ref, g_ref, w_ref, o_ref):
    hb = _rms(x_ref[...], g_ref[...]).astype(BF16)
    for n in range(0, PROJ_W, PROJ_COLS):
        o_ref[:, n:n + PROJ_COLS] = jnp.dot(hb, w_ref[:, n:n + PROJ_COLS],
                                            preferred_element_type=F32)


def _in_proj(x2d, gain, w_perm, tm):
    t = x2d.shape[0]
    return pl.pallas_call(
        _in_proj_kernel,
        grid=(t // tm,),
        in_specs=[pl.BlockSpec((tm, D_MODEL), lambda i: (i, 0)),
                  _const_spec((1, D_MODEL)),
                  _const_spec((D_MODEL, PROJ_W))],
        out_specs=pl.BlockSpec((tm, PROJ_W), lambda i: (i, 0)),
        out_shape=jax.ShapeDtypeStruct((t, PROJ_W), F32),
        compiler_params=pltpu.CompilerParams(dimension_semantics=("arbitrary",),
                                             vmem_limit_bytes=VMEM_LIMIT),
        name="in_proj",
    )(x2d, gain, w_perm)


def _dn_kernel(q_ref, k_ref, v_ref, z_ref, misc_ref, cwq_ref, cwk_ref, cwv_ref, alog_ref, dtb_ref,
               ng_ref, o_ref,
               pad_s, cq_s, ck_s, cv_s, brow_s, grow_s, u_s, w_s, a_s, qg_s, c_s, n_s, el_s):
    seq = q_ref.shape[1]
    nc = seq // CHUNK
    h = pl.program_id(1)
    row = lax.broadcasted_iota(I32, (CHUNK, CHUNK), 0)
    col = lax.broadcasted_iota(I32, (CHUNK, CHUNK), 1)

    def conv(chunks):
        for n in chunks:
            for src, cw_ref, dst, kind in ((q_ref, cwq_ref, cq_s, "q"), (k_ref, cwk_ref, ck_s, "k"),
                                           (v_ref, cwv_ref, cv_s, "v")):
                if n == 0:
                    pad_s[0:SUBLANES, :] = jnp.zeros((SUBLANES, LANES), F32)
                    pad_s[SUBLANES:SUBLANES + CHUNK, :] = src[0, 0:CHUNK, :]
                    tap = lambda j: pad_s[SUBLANES - j:SUBLANES - j + CHUNK, :]
                else:
                    tap = lambda j, src=src: src[0, n * CHUNK - j:(n + 1) * CHUNK - j, :]
                y = cw_ref[DN_CONV - 1:DN_CONV, :] * tap(0)
                for j in range(1, DN_CONV):
                    y = y + cw_ref[DN_CONV - 1 - j:DN_CONV - j, :] * tap(j)
                y = _silu(y)
                if kind != "v":
                    y = y * lax.rsqrt(jnp.sum(y * y, axis=-1, keepdims=True) + 1e-6)
                if kind == "q":
                    y = y * (DN_DK ** -0.5)
                dst[n * CHUNK:(n + 1) * CHUNK, :] = y
            yield

    upper = jnp.where(row <= col, 1.0, 0.0).astype(F32)
    sub8 = lax.broadcasted_iota(I32, (SUBLANES, LANES), 0)
    for n in range(nc):
        mt = misc_ref[0, n * CHUNK:(n + 1) * CHUNK, :].T
        slab = mt[M_B:M_B + SUBLANES, :]
        beta8 = _sigmoid(slab)
        g8 = -jnp.exp(alog_ref[...]) * _softplus(slab + dtb_ref[...])
        gc8 = jnp.dot(g8, upper, preferred_element_type=F32, precision=lax.Precision.HIGHEST)
        b_row = jnp.sum(jnp.where(sub8 == h, beta8, 0.0), axis=0, keepdims=True)
        g_row = jnp.sum(jnp.where(sub8 == h + DN_HEADS, gc8, 0.0), axis=0, keepdims=True)
        brow_s[n] = jnp.broadcast_to(b_row, (SUBLANES, LANES))
        grow_s[n] = jnp.broadcast_to(g_row, (SUBLANES, LANES))

    eye = jnp.where(row == col, 1.0, 0.0).astype(F32)

    def prep_one(n):
        rows = slice(n * CHUNK, (n + 1) * CHUNK)
        q = cq_s[rows, :]
        k = ck_s[rows, :]
        v = cv_s[rows, :]
        g_r = jnp.broadcast_to(grow_s[n][0:1, :], (CHUNK, CHUNK))
        g_c = g_r.T
        b_c = jnp.broadcast_to(brow_s[n][0:1, :], (CHUNK, CHUNK)).T
        g_last = jnp.broadcast_to(g_c[CHUNK - 1:CHUNK, :], (CHUNK, CHUNK))
        decay = jnp.where(row >= col, jnp.exp(jnp.minimum(g_c - g_r, 0.0)), 0.0)
        kb = k * b_c
        kq = _mm_nt(jnp.concatenate([kb, q], axis=0), k)
        yield
        lmat = jnp.where(row > col, kq[:CHUNK] * decay, 0.0)
        a = jnp.where(row >= col, kq[CHUNK:] * decay, 0.0)
        same = lambda b: (row // b) == (col // b)
        l0 = jnp.where(same(DN_BASE), lmat, 0.0)
        x = eye - l0
        m = _mm(l0, l0)
        yield
        power = 2
        while 2 * power < DN_BASE:
            xm = _mm(jnp.concatenate([x, m], axis=0), m)
            yield
            x = x + xm[:CHUNK]
            m = xm[CHUNK:]
            power *= 2
        x = x + _mm(x, m)
        yield
        blk = DN_BASE
        while blk < CHUNK:
            below = jnp.where(same(2 * blk) & jnp.logical_not(same(blk)), lmat, 0.0)
            y = _mm(below, x)
            yield
            x = x - _mm(x, y)
            yield
            blk *= 2
        eg = jnp.exp(g_c)
        sol = _mm(x, jnp.concatenate([v * b_c, kb * eg], axis=1))
        yield
        cn = _mm((k * jnp.exp(g_last - g_c)).T, sol)
        yield
        u_s[n] = sol[:, :DN_DV]
        w_s[n] = sol[:, DN_DV:]
        a_s[n] = a
        qg_s[n] = q * eg
        c_s[n] = cn[:, :DN_DV]
        n_s[n] = cn[:, DN_DV:]
        el_s[n] = jnp.exp(g_last)[0:SUBLANES, :]

    carried = [jnp.zeros((DN_DK, DN_DV), F32)]

    def recur(chunks):
        for n in chunks:
            state = carried[0]
            el = jnp.broadcast_to(el_s[n][0:1, :], (CHUNK, CHUNK))
            nws = _mm(jnp.concatenate([n_s[n], w_s[n], qg_s[n]], axis=0), state)
            yield
            carried[0] = state * el + c_s[n] - nws[:CHUNK]
            o = nws[2 * CHUNK:] + _mm(a_s[n], u_s[n] - nws[CHUNK:2 * CHUNK])
            yield
            rows = slice(n * CHUNK, (n + 1) * CHUNK)
            o_ref[0, rows, :] = (_rms(o, ng_ref[...]) * _silu(z_ref[0, rows, :])).astype(o_ref.dtype)

    parts = [range(i * nc // DN_PARTS, (i + 1) * nc // DN_PARTS) for i in range(DN_PARTS)]
    for i in range(DN_PARTS + 2):
        stage = []
        if i < DN_PARTS:
            stage.append(conv(parts[i]))
        if 1 <= i <= DN_PARTS:
            stage.extend(prep_one(n) for n in parts[i - 1])
        if i >= 2:
            stage.append(recur(parts[i - 2]))
        _weave(*stage)


def _deltanet(proj, conv_w, alog8, dtb8, norm_g):
    bsz, seq, _ = proj.shape
    nc = seq // CHUNK
    assert seq % (DN_PARTS * CHUNK) == 0
    qb, kb, vb, zb = (C_QKV // LANES, (C_QKV + DN_QK) // LANES, (C_QKV + 2 * DN_QK) // LANES,
                      C_Z // LANES)
    col = lambda off: pl.BlockSpec((1, seq, LANES), lambda b, h, off=off: (b, 0, off + h))
    cw = lambda off: pl.BlockSpec((DN_CONV, LANES), lambda b, h, off=off: (0, off + h))
    mat = lambda: pltpu.VMEM((nc, CHUNK, CHUNK), F32)
    return pl.pallas_call(
        _dn_kernel,
        grid=(bsz, DN_HEADS),
        in_specs=[col(qb), col(kb), col(vb), col(zb),
                  pl.BlockSpec((1, seq, LANES), lambda b, h: (b, 0, C_MISC // LANES)),
                  cw(qb), cw(kb), cw(vb),
                  _const_spec((SUBLANES, LANES)), _const_spec((SUBLANES, LANES)),
                  _const_spec((1, DN_DV))],
        out_specs=pl.BlockSpec((1, seq, LANES), lambda b, h: (b, 0, h)),
        out_shape=jax.ShapeDtypeStruct((bsz, seq, DN_V), BF16),
        scratch_shapes=[pltpu.VMEM((SUBLANES + CHUNK, LANES), F32),
                        pltpu.VMEM((seq, LANES), F32), pltpu.VMEM((seq, LANES), F32),
                        pltpu.VMEM((seq, LANES), F32),
                        pltpu.VMEM((nc, SUBLANES, LANES), F32), pltpu.VMEM((nc, SUBLANES, LANES), F32),
                        mat(), mat(), mat(), mat(), mat(), mat(),
                        pltpu.VMEM((nc, SUBLANES, LANES), F32)],
        compiler_params=pltpu.CompilerParams(dimension_semantics=("arbitrary", "arbitrary"),
                                             vmem_limit_bytes=VMEM_LIMIT),
        name="deltanet",
    )(proj, proj, proj, proj, proj, conv_w, conv_w, conv_w, alog8, dtb8, norm_g)


def _even_odd(n):
    return list(range(0, n, 2)) + list(range(1, n, 2))


def _rope(x, c, s, s1, s2):
    w = x.shape[-1]
    return x * c + pltpu.roll(x, w - HALF, 1) * (s * s1) + pltpu.roll(x, HALF, 1) * (s * s2)


def _dsa_kernel(topk, q_ref, qi_ref, misct_ref, kv_ref, miscf_ref, pos_ref, qg_ref, kg_ref, rp_ref,
                hsum_ref, o_ref,
                cos_s, sin_s, ka_s, kb_s, vt_s, kia_s, kib_s, q2_s, qi2_s, sc_s, lg_s, pa_s,
                pb_s, l8_s, acc_s):
    seq = kv_ref.shape[1]
    nc = seq // CHUNK
    qb = pl.program_id(1)
    npair = qb + 1
    r0 = pl.multiple_of(qb * QBLK, QBLK)
    col = lax.broadcasted_iota(I32, (CHUNK, CHUNK), 1)
    s1 = rp_ref[1:2, :]
    s2 = rp_ref[2:3, :]

    @pl.when(qb == 0)
    def _():
        ang = pos_ref[0].astype(F32) * rp_ref[0:1, :]
        packed = (jnp.cos(ang), jnp.sin(ang))
        lane = lax.broadcasted_iota(I32, ang.shape, 1)
        second = (lane >= SA_HEAD_DIM) & (lane < SA_HEAD_DIM + ROT)
        per_row = LANES // ROT
        for j in range(per_row):
            for tab, dst, rest in zip(packed, (cos_s, sin_s), (1.0, 0.0)):
                mine = tab if j == 0 else pltpu.roll(tab, LANES - j * ROT, 1)
                full = jnp.where(lane < ROT, mine,
                                 jnp.where(second, pltpu.roll(mine, SA_HEAD_DIM, 1), rest))
                dst[pl.ds(j, seq // per_row, stride=per_row), :] = full
        for n in range(nc):
            rows = slice(n * CHUNK, (n + 1) * CHUNK)
            c = cos_s[rows, :]
            s = sin_s[rows, :]
            ck = jnp.where(col < ROT, c, 1.0)
            sk = jnp.where(col < ROT, s, 0.0)
            kv = kv_ref[0, rows, :]
            ss = jnp.dot((kv * kv).astype(BF16), hsum_ref[0:LANES, 0:LANES],
                         preferred_element_type=F32)
            r = lax.rsqrt(ss * (1.0 / SA_HEAD_DIM) + RMS_EPS)
            y = jnp.where(col < SA_HEAD_DIM, kv * r * kg_ref[...], kv)
            y = _rope(y, ck, sk, s1, s2)
            ka = jnp.where(col < SA_HEAD_DIM, y, 0.0)
            ka_s[rows, :] = ka.astype(BF16)
            kb_s[rows, :] = pltpu.roll(ka, SA_HEAD_DIM, 1).astype(BF16)
            half = (n % 2) * CHUNK
            vt_s[n // 2, :, half:half + CHUNK] = y.T[SA_HEAD_DIM:, :].astype(BF16)
            yi = _rope(miscf_ref[0, rows, :], ck, sk, s1, s2)
            kia = jnp.where(col < IDX_DIM, yi, 0.0)
            kia_s[rows, :] = kia.astype(BF16)
            kib_s[rows, :] = pltpu.roll(kia, IDX_DIM, 1).astype(BF16)

    c1 = cos_s[pl.ds(r0, QBLK), :]
    sn1 = sin_s[pl.ds(r0, QBLK), :]
    rep = lambda a, k: jnp.concatenate([a] * k, axis=1)
    nq = SA_HEADS * SA_HEAD_DIM // LANES
    xq = q_ref[0]
    yq = _rope(xq * qg_ref[...], rep(c1, nq), rep(sn1, nq), rep(s1, nq), rep(s2, nq))
    ss = jnp.dot((xq * xq).astype(BF16), hsum_ref[...], preferred_element_type=F32)
    r = lax.rsqrt(ss * (1.0 / SA_HEAD_DIM) + RMS_EPS)
    qn = (yq * (r * (SA_HEAD_DIM ** -0.5 * LOG2E))).astype(BF16)
    for pr in range(nq):
        q2_s[pr * QBLK:(pr + 1) * QBLK, :] = qn[:, pr * LANES:(pr + 1) * LANES]
    ni = IDX_HEADS * IDX_DIM // LANES
    yi = _rope(qi_ref[0], rep(c1, ni), rep(sn1, ni), rep(s1, ni), rep(s2, ni)).astype(BF16)
    for pr in range(ni):
        qi2_s[pr * QBLK:(pr + 1) * QBLK, :] = yi[:, pr * LANES:(pr + 1) * LANES]
    mt = misct_ref[0].T
    wscale = (IDX_HEADS ** -0.5) * (IDX_DIM ** -0.5)
    wrow = jnp.concatenate([mt[M_W + hh:M_W + hh + 1, :] for hh in _even_odd(IDX_HEADS)],
                           axis=1) * wscale

    krow = lax.broadcasted_iota(I32, (KPAIR, QBLK), 0)
    qpos = r0 + lax.broadcasted_iota(I32, (KPAIR, QBLK), 1)
    big = -NEG_INF

    def fold8(x, op):
        parts = [x[i:i + SUBLANES] for i in range(0, x.shape[0], SUBLANES)]
        while len(parts) > 1:
            parts = [op(parts[i], parts[i + 1]) for i in range(0, len(parts), 2)]
        return parts[0]

    def idx_pair(c, bounds):
        k0 = pl.multiple_of(c * KPAIR, KPAIR)
        terms = []
        for keys_s in (kia_s, kib_s):
            keys = keys_s[pl.ds(k0, KPAIR), :]
            for pr in range(IDX_HEADS // 2):
                lt = _mm_nt(keys, qi2_s[pr * QBLK:(pr + 1) * QBLK, :])
                yield
                j = len(terms)
                terms.append(jnp.maximum(lt, 0.0) * wrow[:, j * QBLK:(j + 1) * QBLK])
        acc = (terms[0] + terms[1]) + (terms[2] + terms[3])
        causal = k0 + krow <= qpos
        sc = jnp.where(causal, acc, NEG_INF)
        sc_s[c] = sc
        bounds.append((fold8(jnp.where(causal, acc, big), jnp.minimum), fold8(sc, jnp.maximum)))

    def idx_pairs(pairs, carry):
        bounds = []
        _weave(*[idx_pair(c, bounds) for c in pairs])
        lo8, hi8 = carry
        for lo_c, hi_c in bounds:
            lo8 = jnp.minimum(lo8, lo_c)
            hi8 = jnp.maximum(hi8, hi_c)
        return lo8, hi8

    nsteps = npair // 2
    odd = npair % 2 == 1
    lo8, hi8 = lax.fori_loop(0, nsteps, lambda i, carry: idx_pairs((2 * i, 2 * i + 1), carry),
                             (jnp.full((SUBLANES, QBLK), big, F32),
                              jnp.full((SUBLANES, QBLK), NEG_INF, F32)))
    lo8, hi8 = lax.cond(odd, lambda carry: idx_pairs((npair - 1,), carry), lambda carry: carry,
                        (lo8, hi8))
    lo = jnp.min(lo8, axis=0, keepdims=True)
    hi = jnp.max(hi8, axis=0, keepdims=True)

    def count(pred):
        one = lambda c: fold8(jnp.where(pred(sc_s[c]), 1.0, 0.0), jnp.add)
        acc = lax.fori_loop(0, nsteps, lambda i, acc: acc + (one(2 * i) + one(2 * i + 1)),
                            jnp.zeros((SUBLANES, QBLK), F32))
        acc = lax.cond(odd, lambda acc: acc + one(npair - 1), lambda acc: acc, acc)
        return jnp.sum(acc, axis=0, keepdims=True)

    def max_where(pred):
        def body(c, acc):
            s = sc_s[c]
            return jnp.maximum(acc, fold8(jnp.where(pred(s), s, NEG_INF), jnp.maximum))
        acc = lax.fori_loop(0, npair, body, jnp.full((SUBLANES, QBLK), NEG_INF, F32))
        return jnp.max(acc, axis=0, keepdims=True)

    kf = jnp.minimum(float(topk), (qpos[0:1, :] + 1).astype(F32))

    def bisect(_, bounds):
        lo, hi = bounds
        mid = lo + 0.5 * (hi - lo)
        above = count(lambda s: s > mid) >= kf
        return jnp.where(above, mid, lo), jnp.where(above, hi, mid)

    lo, hi = lax.fori_loop(0, BISECT_STEPS, bisect, (lo, hi))

    thr = max_where(lambda s: s <= hi)
    cge = count(lambda s: s >= thr)

    def short(cge):
        return jnp.max(jnp.where(cge < kf, 1.0, 0.0)) > 0.5

    def peel(state):
        thr, cge, _ = state
        nxt = max_where(lambda s: s < thr)
        cnx = count(lambda s: s >= nxt)
        step = cge < kf
        thr = jnp.where(step, nxt, thr)
        cge = jnp.where(step, cnx, cge)
        return thr, cge, short(cge)

    thr, cge, _ = lax.while_loop(lambda st: st[2], peel, (thr, cge, short(cge)))
    need = kf - count(lambda s: s > thr)

    lower = jnp.where(lax.broadcasted_iota(I32, (KPAIR, KPAIR), 0)
                      >= lax.broadcasted_iota(I32, (KPAIR, KPAIR), 1), 1.0, 0.0).astype(BF16)

    def sel_pair(c, shared):
        k0 = pl.multiple_of(c * KPAIR, KPAIR)
        s = sc_s[c]
        eqf = jnp.where(s == thr, 1.0, 0.0)
        pref = jnp.dot(lower, eqf.astype(BF16), preferred_element_type=F32)
        yield
        off = shared["off"]
        shared["off"] = off + pref[KPAIR - 1:KPAIR, :]
        tie = jnp.where(pref + off <= need, eqf, 0.0)
        keep = jnp.where(k0 + krow <= qpos, jnp.where(s > thr, 1.0, tie), 0.0)
        bias = jnp.where(keep > 0.5, 0.0, NEG_INF)
        tops = []
        for keys_s in (ka_s, kb_s):
            keys = keys_s[pl.ds(k0, KPAIR), :]
            for pr in range(SA_HEADS // 2):
                lg = _mm_nt(keys, q2_s[pr * QBLK:(pr + 1) * QBLK, :])
                yield
                lg = lg + bias
                lg_s[c, :, len(tops) * QBLK:(len(tops) + 1) * QBLK] = lg
                tops.append(fold8(lg, jnp.maximum))
        shared["m8"] = jnp.maximum(shared["m8"], jnp.concatenate(tops, axis=1))

    def sel_pairs(pairs, carry):
        shared = {"off": carry[0], "m8": carry[1]}
        _weave(*[sel_pair(c, shared) for c in pairs])
        return shared["off"], shared["m8"]

    carry = lax.fori_loop(0, nsteps, lambda i, carry: sel_pairs((2 * i, 2 * i + 1), carry),
                          (jnp.zeros((1, QBLK), F32),
                           jnp.full((SUBLANES, SA_HEADS * QBLK), NEG_INF, F32)))
    _, m8 = lax.cond(odd, lambda carry: sel_pairs((npair - 1,), carry), lambda carry: carry, carry)
    mrow = jnp.max(m8, axis=0, keepdims=True)

    def probs(c, dst):
        slab = 2 * SUBLANES
        l8 = l8_s[...]
        for r in range(0, KPAIR, slab):
            p = jnp.exp2(lg_s[c, r:r + slab, :] - mrow)
            dst[r:r + slab, :] = p.astype(BF16)
            l8 = l8 + (p[:SUBLANES] + p[SUBLANES:])
        l8_s[...] = l8

    def pv_acc(c, src):
        acc_s[...] += jnp.dot(vt_s[c], src[...], preferred_element_type=F32)

    acc_s[...] = jnp.zeros(acc_s.shape, F32)
    l8_s[...] = jnp.zeros(l8_s.shape, F32)
    probs(0, pa_s)

    def p2(i, carry):
        c = 2 * i + 1
        pv_acc(c - 1, pa_s)
        probs(c, pb_s)
        pv_acc(c, pb_s)
        probs(c + 1, pa_s)
        return carry

    last = npair - 1
    lax.fori_loop(0, last // 2, p2, 0)

    @pl.when(last % 2 == 0)
    def _():
        pv_acc(last, pa_s)

    @pl.when(last % 2 == 1)
    def _():
        pv_acc(last - 1, pa_s)
        probs(last, pb_s)
        pv_acc(last, pb_s)

    out_t = acc_s[...] / jnp.sum(l8_s[...], axis=0, keepdims=True)
    order = _even_odd(SA_HEADS)
    for pr in range(SA_HEADS // 2):
        ev, od = order.index(2 * pr), order.index(2 * pr + 1)
        two = jnp.concatenate([out_t[:, ev * QBLK:(ev + 1) * QBLK],
                               out_t[:, od * QBLK:(od + 1) * QBLK]], axis=0)
        o_ref[0, :, pr * LANES:(pr + 1) * LANES] = two.T.astype(o_ref.dtype)


def _dsa(proj, pos3, qg_t, kg_p, rope_pat, topk):
    bsz, seq, _ = proj.shape
    nc = seq // CHUNK
    qw = SA_HEADS * SA_HEAD_DIM
    iw = IDX_HEADS * IDX_DIM
    assert seq % QBLK == 0 and QBLK == KPAIR
    npairs = seq // KPAIR
    pairs = lambda: pltpu.VMEM((npairs, KPAIR, QBLK), F32)
    head_of = jnp.arange(qw) // SA_HEAD_DIM
    head_sum = (head_of[:, None] == head_of[None, :]).astype(BF16)
    return pl.pallas_call(
        functools.partial(_dsa_kernel, topk),
        grid=(bsz, seq // QBLK),
        in_specs=[pl.BlockSpec((1, QBLK, qw), lambda b, j: (b, j, C_SAQ // qw)),
                  pl.BlockSpec((1, QBLK, iw), lambda b, j: (b, j, C_IXQ // iw)),
                  pl.BlockSpec((1, QBLK, LANES), lambda b, j: (b, j, C_MISC // LANES)),
                  pl.BlockSpec((1, seq, LANES), lambda b, j: (b, 0, C_KV // LANES)),
                  pl.BlockSpec((1, seq, LANES), lambda b, j: (b, 0, C_MISC // LANES)),
                  pl.BlockSpec((1, seq * ROT // LANES, LANES), lambda b, j: (b, 0, 0)),
                  _const_spec((1, qw)), _const_spec((1, LANES)), _const_spec((SUBLANES, LANES)),
                  _const_spec((qw, qw))],
        out_specs=pl.BlockSpec((1, QBLK, qw), lambda b, j: (b, j, 0)),
        out_shape=jax.ShapeDtypeStruct((bsz, seq, qw), BF16),
        scratch_shapes=[pltpu.VMEM((seq, LANES), F32), pltpu.VMEM((seq, LANES), F32),
                        pltpu.VMEM((seq, LANES), BF16), pltpu.VMEM((seq, LANES), BF16),
                        pltpu.VMEM((npairs, SA_HEAD_DIM, KPAIR), BF16),
                        pltpu.VMEM((seq, LANES), BF16), pltpu.VMEM((seq, LANES), BF16),
                        pltpu.VMEM((qw // LANES * QBLK, LANES), BF16),
                        pltpu.VMEM((iw // LANES * QBLK, LANES), BF16),
                        pairs(),
                        pltpu.VMEM((npairs, KPAIR, SA_HEADS * QBLK), F32),
                        pltpu.VMEM((KPAIR, SA_HEADS * QBLK), BF16),
                        pltpu.VMEM((KPAIR, SA_HEADS * QBLK), BF16),
                        pltpu.VMEM((SUBLANES, SA_HEADS * QBLK), F32),
                        pltpu.VMEM((SA_HEAD_DIM, SA_HEADS * QBLK), F32)],
        compiler_params=pltpu.CompilerParams(dimension_semantics=("arbitrary", "arbitrary"),
                                             vmem_limit_bytes=VMEM_LIMIT),
        name="dsa",
    )(proj, proj, proj, proj, proj, pos3, qg_t, kg_p, rope_pat, head_sum)


def _merge_kernel(x_ref, oa_ref, ob_ref, g_ref, wg_ref, bg_ref, wa_ref, wb_ref, wo_ref, o_ref):
    x = x_ref[...]
    hb = _rms(x, g_ref[...]).astype(BF16)
    oa = oa_ref[...]
    ob = ob_ref[...]
    acc = x
    step = PROJ_COLS
    for n in range(0, D_MODEL, step):
        ga = _sigmoid(jnp.dot(hb, wg_ref[:, n:n + step], preferred_element_type=F32)
                      + bg_ref[:, n:n + step])
        gb = _sigmoid(jnp.dot(hb, wg_ref[:, D_MODEL + n:D_MODEL + n + step],
                              preferred_element_type=F32)
                      + bg_ref[:, D_MODEL + n:D_MODEL + n + step])
        ya = jnp.dot(oa, wa_ref[:, n:n + step], preferred_element_type=F32)
        yb = jnp.dot(ob, wb_ref[:, n:n + step], preferred_element_type=F32)
        merged = (ga * ya + gb * yb).astype(BF16)
        acc = acc + jnp.dot(merged, wo_ref[n:n + step, :], preferred_element_type=F32)
    o_ref[...] = acc


def _merge(x2d, oa2d, ob2d, gain, w_gate, b_gate, w_a, w_b, w_o, tm):
    t = x2d.shape[0]
    tok = lambda w: pl.BlockSpec((tm, w), lambda i: (i, 0))
    return pl.pallas_call(
        _merge_kernel,
        grid=(t // tm,),
        in_specs=[tok(D_MODEL), tok(DN_V), tok(SA_HEADS * SA_HEAD_DIM),
                  _const_spec((1, D_MODEL)), _const_spec((D_MODEL, 2 * D_MODEL)),
                  _const_spec((1, 2 * D_MODEL)), _const_spec((DN_V, D_MODEL)),
                  _const_spec((SA_HEADS * SA_HEAD_DIM, D_MODEL)), _const_spec((D_MODEL, D_MODEL))],
        out_specs=tok(D_MODEL),
        out_shape=jax.ShapeDtypeStruct((t, D_MODEL), F32),
        compiler_params=pltpu.CompilerParams(dimension_semantics=("arbitrary",),
                                             vmem_limit_bytes=VMEM_LIMIT),
        name="merge",
    )(x2d, oa2d, ob2d, gain, w_gate, b_gate, w_a, w_b, w_o)


def _ffn_kernel(x_ref, p_ref, g2_ref, wg_ref, wu_ref, wd_ref, g3_ref, wpg_ref, wp_ref, o_ref):
    x = x_ref[...]
    hb = _rms(x, g2_ref[...]).astype(BF16)
    acc = x
    step = FF_COLS
    for n in range(0, D_FF, step):
        gate = jnp.dot(hb, wg_ref[:, n:n + step], preferred_element_type=F32)
        up = jnp.dot(hb, wu_ref[:, n:n + step], preferred_element_type=F32)
        acc = acc + jnp.dot((_silu(gate) * up).astype(BF16), wd_ref[n:n + step, :],
                            preferred_element_type=F32)
    h3 = _rms(acc, g3_ref[...]).astype(BF16)
    pg = _sigmoid(jnp.dot(h3, wpg_ref[...], preferred_element_type=F32))
    pe = jnp.dot(p_ref[...].astype(BF16), wp_ref[...], preferred_element_type=F32)
    o_ref[...] = acc + pg * pe


def _ffn(x2d, p2d, g2, w_g, w_u, w_d, g3, w_pg, w_p, tm):
    t = x2d.shape[0]
    tok = lambda w: pl.BlockSpec((tm, w), lambda i: (i, 0))
    return pl.pallas_call(
        _ffn_kernel,
        grid=(t // tm,),
        in_specs=[tok(D_MODEL), tok(PLE_DIM), _const_spec((1, D_MODEL)),
                  _const_spec((D_MODEL, D_FF)), _const_spec((D_MODEL, D_FF)),
                  _const_spec((D_FF, D_MODEL)), _const_spec((1, D_MODEL)),
                  _const_spec((D_MODEL, D_MODEL)), _const_spec((PLE_DIM, D_MODEL))],
        out_specs=tok(D_MODEL),
        out_shape=jax.ShapeDtypeStruct((t, D_MODEL), F32),
        compiler_params=pltpu.CompilerParams(dimension_semantics=("arbitrary",),
                                             vmem_limit_bytes=VMEM_LIMIT),
        name="ffn_ple",
    )(x2d, p2d, g2, w_g, w_u, w_d, g3, w_pg, w_p)


def _regroup_w_in(w):
    sizes = (DN_CONV_CH, DN_V, DN_HEADS, DN_HEADS, SA_HEADS * SA_HEAD_DIM, SA_HEAD_DIM, SA_HEAD_DIM,
             IDX_HEADS * IDX_DIM, IDX_DIM, IDX_HEADS)
    parts, off = [], 0
    for s in sizes:
        parts.append(w[:, off:off + s])
        off += s
    qkv, z, b, a, saq, sak, sav, ixq, ixk, ixw = parts
    cols = [qkv, z, saq, ixq, sak, sav, ixk, b, a, ixw]
    used = sum(c.shape[1] for c in cols)
    cols.append(jnp.zeros((w.shape[0], PROJ_W - used), w.dtype))
    return jnp.concatenate(cols, axis=1)


def _rope_pattern():
    lane = jnp.arange(LANES) % SA_HEAD_DIM
    inv_freq = ROPE_THETA ** (-jnp.arange(0, ROT, 2, dtype=F32) / ROT)
    freq = inv_freq[jnp.arange(LANES) % HALF]
    s1 = jnp.where(lane < HALF, -1.0, 0.0)
    s2 = jnp.where((lane >= HALF) & (lane < ROT), 1.0, 0.0)
    pat = jnp.zeros((SUBLANES, LANES), F32)
    return pat.at[0].set(freq).at[1].set(s1).at[2].set(s2)


def _layer(x, p, positions, attn_norm, w_in, conv_w, a_log, dt_bias, dn_norm, q_norm, k_norm,
           w_branch_a, w_branch_b, w_gate, b_gate, w_o, ffn_norm, w_ffn_gate, w_ffn_up, w_ffn_down,
           ple_norm, w_ple_gate, w_ple):
    bsz, seq, _ = x.shape
    t = bsz * seq
    tm = TOKEN_TILE
    assert t % tm == 0
    topk = min(IDX_TOPK_MAX, seq // 4)
    row = lambda v: v.reshape(1, -1).astype(F32)
    x2d = x.reshape(t, D_MODEL)

    proj = _in_proj(x2d, row(attn_norm), _regroup_w_in(w_in).astype(BF16), tm)
    proj = proj.reshape(bsz, seq, PROJ_W)

    pad4 = jnp.zeros((DN_HEADS,), F32)
    per_head = lambda v: jnp.broadcast_to(jnp.concatenate([pad4, v.astype(F32)])[:, None],
                                          (SUBLANES, LANES))
    o_a = _deltanet(proj, conv_w.astype(F32), per_head(a_log), per_head(dt_bias), row(dn_norm))

    qg_t = jnp.tile(q_norm.astype(F32), SA_HEADS).reshape(1, -1)
    kg_p = jnp.concatenate([k_norm.astype(F32), jnp.ones((LANES - SA_HEAD_DIM,), F32)]).reshape(1, -1)
    per_row = LANES // ROT
    pos_packed = jnp.repeat(positions.astype(I32).reshape(bsz, seq // per_row, per_row), ROT, axis=2)
    o_b = _dsa(proj, pos_packed, qg_t, kg_p, _rope_pattern(), topk)

    x1 = _merge(x2d, o_a.reshape(t, DN_V), o_b.reshape(t, -1), row(attn_norm),
                w_gate.astype(BF16), row(b_gate), w_branch_a.astype(BF16), w_branch_b.astype(BF16),
                w_o.astype(BF16), tm)
    x2 = _ffn(x1, p.reshape(t, PLE_DIM), row(ffn_norm), w_ffn_gate.astype(BF16),
              w_ffn_up.astype(BF16), w_ffn_down.astype(BF16), row(ple_norm),
              w_ple_gate.astype(BF16), w_ple.astype(BF16), tm)
    return x2.reshape(bsz, seq, D_MODEL)


def kernel(x, p, positions, attn_norm, w_in, conv_w, a_log, dt_bias, dn_norm, q_norm, k_norm,
           w_branch_a, w_branch_b, w_gate, b_gate, w_o, ffn_norm, w_ffn_gate, w_ffn_up, w_ffn_down,
           ple_norm, w_ple_gate, w_ple):
    depth = w_in.shape[0]
    for i in range(depth):
        x = _layer(x, p[i], positions, attn_norm[i], w_in[i], conv_w[i], a_log[i], dt_bias[i],
                   dn_norm[i], q_norm[i], k_norm[i], w_branch_a[i], w_branch_b[i], w_gate[i],
                   b_gate[i], w_o[i], ffn_norm[i], w_ffn_gate[i], w_ffn_up[i], w_ffn_down[i],
                   ple_norm[i], w_ple_gate[i], w_ple[i])
    return x
```
